```python
import math
import jax, jax.numpy as jnp
from jax import lax
import numpy as np

D_MODEL = 2048
BATCH = 2
SEQ = 8192
DEPTH = 1

BLOCK = 128
N_META = 16
PAD = BLOCK - N_META
HEAD_DIM = 64
ATTN_WIDTH = D_MODEL // 2
N_Q_HEADS = ATTN_WIDTH // HEAD_DIM
N_KV_HEADS = N_Q_HEADS // 8
Q_PER_KV = N_Q_HEADS // N_KV_HEADS
KV_WIDTH = N_KV_HEADS * HEAD_DIM
WINDOW = 128
POOL_WINDOWS = (2, 4, 8, 16)
N_POOL_GROUPS = len(POOL_WINDOWS)
POOL_WIDTH = D_MODEL // 2
POOL_GROUP_WIDTH = POOL_WIDTH // N_POOL_GROUPS
N_BUCKETS = 32
MAX_DISTANCE = 128
N_GROUPS = 8
EXPERTS_PER_GROUP = 8
N_EXPERTS = N_GROUPS * EXPERTS_PER_GROUP
TOP_K = 2
D_EXPERT = D_MODEL // 4
MOE_BLOCK = 128
RMS_EPS = 1e-6
IN_WIDTH = ATTN_WIDTH + 2 * KV_WIDTH + POOL_WIDTH + 2 * D_MODEL

kernel_name = "hybrid_swa_pool_hmoe_block"


def rms_norm(x, g):
    xf = x.astype(jnp.float32)
    y = xf * lax.rsqrt(jnp.mean(xf * xf, axis=-1, keepdims=True) + RMS_EPS)
    return (y * g.astype(jnp.float32)).astype(x.dtype)


def t5_bucket(dist):
    max_exact = N_BUCKETS // 2
    d = jnp.maximum(dist, 0)
    large = max_exact + (jnp.log(jnp.maximum(d, 1).astype(jnp.float32) / max_exact)
                         / math.log(MAX_DISTANCE / max_exact) * (N_BUCKETS - max_exact)).astype(jnp.int32)
    large = jnp.minimum(large, N_BUCKETS - 1)
    return jnp.where(d < max_exact, d, large)


def sliding_window_attention(q, k, v, sinks, rel_bias):
    b, lp = q.shape[:2]
    nb = lp // BLOCK
    scale = HEAD_DIM ** -0.5
    qb = q.reshape(b, nb, BLOCK, N_KV_HEADS, Q_PER_KV, HEAD_DIM)
    kb = k.reshape(b, nb, BLOCK, N_KV_HEADS, HEAD_DIM)
    vb = v.reshape(b, nb, BLOCK, N_KV_HEADS, HEAD_DIM)
    zk = jnp.zeros_like(kb[:, :1])
    kwin = jnp.concatenate([jnp.concatenate([zk, kb[:, :-1]], axis=1), kb], axis=2)
    vwin = jnp.concatenate([jnp.concatenate([zk, vb[:, :-1]], axis=1), vb], axis=2)
    kmeta = k[:, PAD:BLOCK]
    vmeta = v[:, PAD:BLOCK]

    blk = jnp.arange(nb)[:, None, None]
    qi = jnp.arange(BLOCK)
    kj = jnp.arange(2 * BLOCK)
    q_pos = blk * BLOCK + qi[None, :, None]
    k_pos = (blk - 1) * BLOCK + kj[None, None, :]
    dist = q_pos - k_pos
    win_ok = (dist >= 0) & (dist < WINDOW) & (k_pos >= PAD)
    dist_local = qi[:, None] + BLOCK - kj[None, :]
    rb = rel_bias.astype(jnp.float32)
    win_bias = rb[t5_bucket(dist_local)].reshape(BLOCK, 2 * BLOCK, N_KV_HEADS, Q_PER_KV).transpose(2, 3, 0, 1)
    m_pos = PAD + jnp.arange(N_META)
    mdist = q_pos - m_pos[None, None, :]
    meta_ok = mdist >= WINDOW
    meta_bias = rb[t5_bucket(mdist)].reshape(nb, BLOCK, N_META, N_KV_HEADS, Q_PER_KV).transpose(0, 3, 4, 1, 2)

    s_win = jnp.einsum('bnqhgd,bnkhd->bnhgqk', qb, kwin).astype(jnp.float32) * scale + win_bias[None, None]
    s_win = jnp.where(win_ok[None, :, None, None], s_win, -jnp.inf)
    s_meta = jnp.einsum('bnqhgd,bmhd->bnhgqm', qb, kmeta).astype(jnp.float32) * scale + meta_bias[None]
    s_meta = jnp.where(meta_ok[None, :, None, None], s_meta, -jnp.inf)
    sink = sinks.astype(jnp.float32).reshape(N_KV_HEADS, Q_PER_KV)[None, None, :, :, None, None]
    mx = jnp.maximum(jnp.maximum(s_win.max(-1, keepdims=True), s_meta.max(-1, keepdims=True)), sink)
    p_win = jnp.exp(s_win - mx)
    p_meta = jnp.exp(s_meta - mx)
    denom = p_win.sum(-1, keepdims=True) + p_meta.sum(-1, keepdims=True) + jnp.exp(sink - mx)
    p_win = (p_win / denom).astype(v.dtype)
    p_meta = (p_meta / denom).astype(v.dtype)
    o = (jnp.einsum('bnhgqk,bnkhd->bnqhgd', p_win, vwin)
         + jnp.einsum('bnhgqm,bmhd->bnqhgd', p_meta, vmeta))
    return o.reshape(b, lp, ATTN_WIDTH)


def multiscale_pool(u, valid, w_group, scale):
    b, lp, _ = u.shape
    ug = (u.astype(jnp.float32) * valid[None, :, None]).reshape(b, lp, N_POOL_GROUPS, POOL_GROUP_WIDTH)
    c = jnp.cumsum(ug, axis=1)
    cnt = jnp.cumsum(valid.astype(jnp.float32))
    outs = []
    for gi, w in enumerate(POOL_WINDOWS):
        cp = jnp.pad(c[:, :, gi], ((0, 0), (w, 0), (0, 0)))
        win_sum = cp[:, w:] - cp[:, :lp]
        cntp = jnp.pad(cnt, (w, 0))
        n_valid = jnp.maximum(cntp[w:] - cntp[:lp], 1.0)
        outs.append(win_sum / n_valid[None, :, None] - ug[:, :, gi])
    mixed = jnp.stack(outs, axis=2).astype(u.dtype)
    y = jnp.einsum('bsgc,gcd->bsgd', mixed, w_group)
    return y.reshape(b, lp, POOL_WIDTH) * scale


def hierarchical_moe(x, w_rg, b_rg, w_re, b_re, w_gate, w_up, w_down):
    n, d = x.shape
    xf = x.astype(jnp.float32)
    g_logits = xf @ w_rg.astype(jnp.float32) + b_rg.astype(jnp.float32)
    g_prob = jax.nn.softmax(g_logits, axis=-1)
    grp = jnp.argmax(g_logits, axis=-1)
    p_grp = jnp.take_along_axis(g_prob, grp[:, None], axis=1)[:, 0]
    e_logits = (xf @ w_re.astype(jnp.float32) + b_re.astype(jnp.float32)).reshape(n, N_GROUPS, EXPERTS_PER_GROUP)
    e_in_grp = jnp.take_along_axis(e_logits, grp[:, None, None], axis=1)[:, 0]
    top_l, top_i = lax.top_k(e_in_grp, TOP_K)
    top_w = jax.nn.softmax(top_l, axis=-1) * p_grp[:, None]
    expert = grp[:, None] * EXPERTS_PER_GROUP + top_i

    a = n * TOP_K
    e_flat = expert.reshape(a)
    w_flat = top_w.reshape(a)
    tok = jnp.repeat(jnp.arange(n, dtype=jnp.int32), TOP_K)
    order = jnp.argsort(e_flat)
    e_sorted = e_flat[order]
    counts = jnp.bincount(e_flat, length=N_EXPERTS)
    starts = jnp.cumsum(counts) - counts
    padded = (counts + MOE_BLOCK - 1) // MOE_BLOCK * MOE_BLOCK
    pad_ends = jnp.cumsum(padded)
    pad_starts = pad_ends - padded
    dest = pad_starts[e_sorted] + (jnp.arange(a) - starts[e_sorted])
    n_blocks = a // MOE_BLOCK + N_EXPERTS
    slots = n_blocks * MOE_BLOCK
    slot_tok = jnp.full((slots,), n, jnp.int32).at[dest].set(tok[order])
    slot_w = jnp.zeros((slots,), x.dtype).at[dest].set(w_flat[order].astype(x.dtype))
    block_expert = jnp.minimum(jnp.searchsorted(pad_ends, jnp.arange(n_blocks) * MOE_BLOCK, side='right'),
                               N_EXPERTS - 1).astype(jnp.int32)
    x_pad = jnp.concatenate([x, jnp.zeros((1, d), x.dtype)], axis=0)

    def run_block(args):
        e, toks, ws = args
        xb = x_pad[toks]
        hb = jax.nn.silu(xb @ w_gate[e]) * (xb @ w_up[e])
        return (hb @ w_down[e]) * ws[:, None]

    yb = lax.map(run_block, (block_expert, slot_tok.reshape(n_blocks, MOE_BLOCK),
                             slot_w.reshape(n_blocks, MOE_BLOCK)))
    y = jax.ops.segment_sum(yb.reshape(slots, d), slot_tok, num_segments=n + 1)
    return y[:n]


def setup_inputs(seed: int = 0) -> dict:
    key = jax.random.key(seed)
    ks = jax.random.split(key, 20)

    def nrm(k, shape, s):
        return jax.random.normal(k, shape, jnp.float32) * s

    return {
        "x": nrm(ks[0], (BATCH, SEQ, D_MODEL), 1.0),
        "meta_tokens": nrm(ks[1], (N_META, D_MODEL), 1.0),
        "rel_bias": nrm(ks[2], (N_BUCKETS, N_Q_HEADS), 0.5),
        "norm_mix": 1.0 + nrm(ks[3], (DEPTH, D_MODEL), 0.02),
        "w_in": nrm(ks[4], (DEPTH, D_MODEL, IN_WIDTH), D_MODEL ** -0.5),
        "attn_sinks": nrm(ks[5], (DEPTH, N_Q_HEADS), 0.5),
        "pool_mix": nrm(ks[6], (DEPTH, N_POOL_GROUPS, POOL_GROUP_WIDTH, POOL_GROUP_WIDTH), POOL_GROUP_WIDTH ** -0.5),
        "pool_scale": 1.0 + nrm(ks[7], (DEPTH, POOL_WIDTH), 0.02),
        "w_attn_branch": nrm(ks[8], (DEPTH, ATTN_WIDTH, D_MODEL), ATTN_WIDTH ** -0.5),
        "w_pool_branch": nrm(ks[9], (DEPTH, POOL_WIDTH, D_MODEL), POOL_WIDTH ** -0.5),
        "w_out": nrm(ks[10], (DEPTH, D_MODEL, D_MODEL), D_MODEL ** -0.5),
        "norm_ffn": 1.0 + nrm(ks[11], (DEPTH, D_MODEL), 0.02),
        "w_router_group": nrm(ks[12], (DEPTH, D_MODEL, N_GROUPS), D_MODEL ** -0.5),
        "b_router_group": nrm(ks[13], (DEPTH, N_GROUPS), 0.01),
        "w_router_expert": nrm(ks[14], (DEPTH, D_MODEL, N_EXPERTS), D_MODEL ** -0.5),
        "b_router_expert": nrm(ks[15], (DEPTH, N_EXPERTS), 0.01),
        "w_gate": nrm(ks[16], (DEPTH, N_EXPERTS, D_MODEL, D_EXPERT), D_MODEL ** -0.5),
        "w_up": nrm(ks[17], (DEPTH, N_EXPERTS, D_MODEL, D_EXPERT), D_MODEL ** -0.5),
        "w_down": nrm(ks[18], (DEPTH, N_EXPERTS, D_EXPERT, D_MODEL), D_EXPERT ** -0.5),
        "norm_final": 1.0 + nrm(ks[19], (D_MODEL,), 0.02),
    }


def reference(x, meta_tokens, rel_bias, norm_mix, w_in, attn_sinks, pool_mix, pool_scale,
              w_attn_branch, w_pool_branch, w_out, norm_ffn, w_router_group, b_router_group,
              w_router_expert, b_router_expert, w_gate, w_up, w_down, norm_final):
    b, s, d = x.shape
    lead = jnp.concatenate([jnp.zeros((PAD, d), x.dtype), meta_tokens.astype(x.dtype)], axis=0)
    h = jnp.concatenate([jnp.broadcast_to(lead[None], (b, BLOCK, d)), x], axis=1)
    lp = s + BLOCK
    valid = jnp.arange(lp) >= PAD
    cuts = np.cumsum([ATTN_WIDTH, KV_WIDTH, KV_WIDTH, POOL_WIDTH, D_MODEL]).tolist()
    for layer in range(DEPTH):
        hn = rms_norm(h, norm_mix[layer])
        proj = hn @ w_in[layer]
        q, k, v, u, g_a, g_p = jnp.split(proj, cuts, axis=-1)
        attn = sliding_window_attention(q.reshape(b, lp, N_Q_HEADS, HEAD_DIM),
                                        k.reshape(b, lp, N_KV_HEADS, HEAD_DIM),
                                        v.reshape(b, lp, N_KV_HEADS, HEAD_DIM),
                                        attn_sinks[layer], rel_bias)
        pool = multiscale_pool(u, valid, pool_mix[layer], pool_scale[layer])
        merged = (jax.nn.sigmoid(g_a) * (attn @ w_attn_branch[layer])
                  + jax.nn.sigmoid(g_p) * (pool @ w_pool_branch[layer]))
        h = h + merged @ w_out[layer]

        hv = h[:, PAD:]
        hn2 = rms_norm(hv, norm_ffn[layer]).reshape(b * (lp - PAD), d)
        ffn = hierarchical_moe(hn2, w_router_group[layer], b_router_group[layer],
                               w_router_expert[layer], b_router_expert[layer],
                               w_gate[layer], w_up[layer], w_down[layer])
        h = h.at[:, PAD:].add(ffn.reshape(b, lp - PAD, d))
    return rms_norm(h[:, BLOCK:], norm_final)
```

```python
import functools
import math

import numpy as np
import jax
import jax.numpy as jnp
from jax import lax
from jax.experimental import pallas as pl
from jax.experimental.pallas import tpu as pltpu

F32 = jnp.float32
BF16 = jnp.bfloat16
I32 = jnp.int32
U32 = jnp.uint32

BLOCK = 128
N_META = 16
PAD = BLOCK - N_META
HEAD_DIM = 64
N_KV_HEADS = 2
Q_PER_KV = 8
N_Q_HEADS = N_KV_HEADS * Q_PER_KV
WINDOW = 128
POOL_WINDOWS = (2, 4, 8, 16)
POOL_HALO = 16
N_BUCKETS = 32
MAX_DISTANCE = 128
N_GROUPS = 8
EXPERTS_PER_GROUP = 8
N_EXPERTS = N_GROUPS * EXPERTS_PER_GROUP
RMS_EPS = 1e-6
LANES = 128
VMEM_LIMIT = 56 * 1024 * 1024

ROW_TILE = 256
EXPERT_ROWS = 256
COL_E0, COL_E1, COL_W0, COL_W1, COL_R0, COL_R1 = 0, 1, 2, 3, 4, 5


def _params(sem, vmem=VMEM_LIMIT):
    return pltpu.CompilerParams(dimension_semantics=sem, vmem_limit_bytes=vmem)


def _resident(shape):
    nd = len(shape)
    return pl.BlockSpec(shape, lambda *_: (0,) * nd, pipeline_mode=pl.Buffered(1))


def _inproj_kernel(h_ref, g_ref, w_ref, q_ref, kv_ref, u_ref, ga_ref, gp_ref):
    x = h_ref[...]
    ms = jnp.mean(x * x, axis=-1, keepdims=True)
    hn = (x * lax.rsqrt(ms + RMS_EPS) * g_ref[...]).astype(BF16)
    off = 0
    for ref in (q_ref, kv_ref, u_ref, ga_ref, gp_ref):
        width = ref.shape[1]
        for c in range(0, width, 1024):
            cw = min(1024, width - c)
            ref[:, c:c + cw] = jnp.dot(hn, w_ref[:, off + c:off + c + cw],
                                       preferred_element_type=F32).astype(BF16)
        off += width


def _inproj(h, g, w, widths):
    rows, d = h.shape
    tm = ROW_TILE
    outs = [jax.ShapeDtypeStruct((rows, wd), BF16) for wd in widths]
    return pl.pallas_call(
        _inproj_kernel,
        grid=(rows // tm,),
        in_specs=[pl.BlockSpec((tm, d), lambda i: (i, 0)),
                  _resident((1, d)),
                  _resident(w.shape)],
        out_specs=[pl.BlockSpec((tm, wd), lambda i: (i, 0)) for wd in widths],
        out_shape=outs,
        compiler_params=_params(("parallel",)),
        name="inproj",
    )(h, g, w)


def _attn_kernel(sink_ref, mbias_ref, q_ref, kvp_ref, kvc_ref, kvm_ref, bias_ref, o_ref):
    n = pl.program_id(1)
    scale = HEAD_DIM ** -0.5
    qi = lax.broadcasted_iota(I32, (BLOCK, 2 * BLOCK), 0)
    kj = lax.broadcasted_iota(I32, (BLOCK, 2 * BLOCK), 1)
    dist = qi + BLOCK - kj
    k_pos = (n - 1) * BLOCK + kj
    win_ok = (dist >= 0) & (dist < WINDOW) & (k_pos >= PAD)
    mq = lax.broadcasted_iota(I32, (BLOCK, BLOCK), 0)
    mj = lax.broadcasted_iota(I32, (BLOCK, BLOCK), 1)
    meta_ok = (mj >= PAD) & (n * BLOCK + mq - mj >= WINDOW)
    kw = N_KV_HEADS * HEAD_DIM
    for hk in range(N_KV_HEADS):
        ks = slice(hk * HEAD_DIM, (hk + 1) * HEAD_DIM)
        vs = slice(kw + hk * HEAD_DIM, kw + (hk + 1) * HEAD_DIM)
        keys = jnp.concatenate([kvp_ref[:, ks], kvc_ref[:, ks], kvm_ref[:, ks]], axis=0)
        vals = jnp.concatenate([kvp_ref[:, vs], kvc_ref[:, vs], kvm_ref[:, vs]], axis=0)
        for g in range(Q_PER_KV):
            h = hk * Q_PER_KV + g
            qh = q_ref[:, h * HEAD_DIM:(h + 1) * HEAD_DIM]
            s = lax.dot_general(qh, keys, (((1,), (1,)), ((), ())),
                                preferred_element_type=F32) * scale
            s_win = jnp.where(win_ok, s[:, :2 * BLOCK] + bias_ref[h], -jnp.inf)
            s_meta = jnp.where(meta_ok, s[:, 2 * BLOCK:] + mbias_ref[h], -jnp.inf)
            sink = sink_ref[h]
            mx = jnp.maximum(jnp.maximum(jnp.max(s_win, axis=-1, keepdims=True),
                                         jnp.max(s_meta, axis=-1, keepdims=True)), sink)
            p_win = jnp.exp(s_win - mx)
            p_meta = jnp.exp(s_meta - mx)
            denom = (jnp.sum(p_win, axis=-1, keepdims=True) + jnp.sum(p_meta, axis=-1, keepdims=True)
                     + jnp.exp(sink - mx))
            p = (jnp.concatenate([p_win, p_meta], axis=1) / denom).astype(BF16)
            o = jnp.dot(p, vals, preferred_element_type=F32)
            o_ref[:, h * HEAD_DIM:(h + 1) * HEAD_DIM] = o.astype(o_ref.dtype)


def _attention(q, kv, win_bias, meta_bias, sinks, batch, nb):
    rows, aw = q.shape
    kvw = kv.shape[1]
    smem = pl.BlockSpec(memory_space=pltpu.SMEM)
    return pl.pallas_call(
        _attn_kernel,
        grid=(batch, nb),
        in_specs=[smem, smem,
                  pl.BlockSpec((BLOCK, aw), lambda b, n: (b * nb + n, 0)),
                  pl.BlockSpec((BLOCK, kvw), lambda b, n: (b * nb + jnp.maximum(n - 1, 0), 0)),
                  pl.BlockSpec((BLOCK, kvw), lambda b, n: (b * nb + n, 0)),
                  pl.BlockSpec((BLOCK, kvw), lambda b, n: (b * nb, 0)),
                  _resident(win_bias.shape)],
        out_specs=pl.BlockSpec((BLOCK, aw), lambda b, n: (b * nb + n, 0)),
        out_shape=jax.ShapeDtypeStruct((rows, aw), BF16),
        compiler_params=_params(("parallel", "parallel")),
        name="attention",
    )(sinks, meta_bias, q, kv, kv, kv, win_bias)


def _window_bias(rel_bias):
    qi = np.arange(BLOCK)[:, None]
    kj = np.arange(2 * BLOCK)[None, :]
    d = np.maximum(qi + BLOCK - kj, 0)
    max_exact = N_BUCKETS // 2
    large = max_exact + (np.log(np.maximum(d, 1).astype(np.float32) / max_exact)
                         / math.log(MAX_DISTANCE / max_exact) * (N_BUCKETS - max_exact)).astype(np.int32)
    bucket = np.where(d < max_exact, d, np.minimum(large, N_BUCKETS - 1))
    return jnp.transpose(rel_bias.astype(F32)[bucket], (2, 0, 1))


def _mixer_kernel(lp, attn_ref, u_ref, uh_ref, ga_ref, gp_ref, h_ref,
                  pmix_ref, pscale_ref, wa_ref, wp_ref, wo_ref, g2_ref, wr_ref, br_ref,
                  h2_ref, xp_ref, route_ref, cnt_ref, run_ref):
    i = pl.program_id(0)
    tm = h_ref.shape[0]

    @pl.when(i == 0)
    def _():
        run_ref[...] = jnp.zeros_like(run_ref)

    t = (i * tm + lax.broadcasted_iota(I32, (tm, 1), 0)) % lp
    valid = t >= PAD
    tx = (i * tm - POOL_HALO + lp + lax.broadcasted_iota(I32, (tm + POOL_HALO, 1), 0)) % lp
    uext = jnp.concatenate([uh_ref[...], u_ref[...]], axis=0).astype(F32)
    ug = jnp.where(tx >= PAD, uext, 0.0)
    n_rows = (t - PAD + 1).astype(F32)
    gw = ug.shape[1] // len(POOL_WINDOWS)
    pooled = []
    for gi, w in enumerate(POOL_WINDOWS):
        c = ug[:, gi * gw:(gi + 1) * gw]
        s, span = c, 1
        while span < w:
            s = s[span:] + s[:-span]
            span *= 2
        win = s[POOL_HALO + 1 - w:POOL_HALO + 1 - w + tm]
        n_valid = jnp.clip(n_rows, 1.0, float(w))
        mixed = jnp.where(valid, win / n_valid - c[POOL_HALO:], 0.0)
        pooled.append(jnp.dot(mixed.astype(BF16), pmix_ref[gi], preferred_element_type=F32))
    pool = jnp.concatenate(pooled, axis=1) * pscale_ref[...]

    a = jnp.dot(attn_ref[...], wa_ref[...], preferred_element_type=F32)
    p = jnp.dot(pool.astype(BF16), wp_ref[...], preferred_element_type=F32)
    merged = (jax.nn.sigmoid(ga_ref[...].astype(F32)) * a
              + jax.nn.sigmoid(gp_ref[...].astype(F32)) * p)
    h2 = h_ref[...] + jnp.dot(merged.astype(BF16), wo_ref[...], preferred_element_type=F32)
    h2_ref[...] = h2

    ms = jnp.mean(h2 * h2, axis=-1, keepdims=True)
    hn2 = h2 * lax.rsqrt(ms + RMS_EPS) * g2_ref[...]
    half = hn2.shape[1] // 2
    lo = lax.bitcast_convert_type(hn2[:, :half].astype(BF16).astype(F32), U32)
    hi = lax.bitcast_convert_type(hn2[:, half:].astype(BF16).astype(F32), U32)
    xp_ref[...] = (hi & jnp.uint32(0xFFFF0000)) | (lo >> 16)

    logits = jnp.dot(hn2, wr_ref[...], preferred_element_type=F32,
                     precision=lax.Precision.HIGHEST) + br_ref[...]
    col = lax.broadcasted_iota(I32, logits.shape, 1).astype(F32)
    neg = -jnp.inf
    gl = jnp.where(col < N_GROUPS, logits, neg)
    gmax = jnp.max(gl, axis=-1, keepdims=True)
    grp = jnp.min(jnp.where(gl == gmax, col, float(LANES)), axis=-1, keepdims=True)
    p_grp = 1.0 / jnp.sum(jnp.exp(gl - gmax), axis=-1, keepdims=True)
    e_lo = N_GROUPS + grp * EXPERTS_PER_GROUP
    el = jnp.where((col >= e_lo) & (col < e_lo + EXPERTS_PER_GROUP), logits, neg)
    m1 = jnp.max(el, axis=-1, keepdims=True)
    i1 = jnp.min(jnp.where(el == m1, col, float(LANES)), axis=-1, keepdims=True)
    el2 = jnp.where(col == i1, neg, el)
    m2 = jnp.max(el2, axis=-1, keepdims=True)
    i2 = jnp.min(jnp.where(el2 == m2, col, float(LANES)), axis=-1, keepdims=True)
    z = jnp.exp(m2 - m1)
    w1 = p_grp / (1.0 + z)
    w2 = p_grp * z / (1.0 + z)
    e1 = i1 - N_GROUPS
    e2 = i2 - N_GROUPS

    oh1 = jnp.where((col == e1) & valid, 1.0, 0.0)
    oh2 = jnp.where((col == e2) & valid, 1.0, 0.0)
    lower = (lax.broadcasted_iota(I32, (tm, tm), 0) > lax.broadcasted_iota(I32, (tm, tm), 1))
    lower = jnp.where(lower, 1.0, 0.0).astype(BF16)
    before1 = jnp.dot(lower, oh1.astype(BF16), preferred_element_type=F32)
    before2 = jnp.dot(lower, oh2.astype(BF16), preferred_element_type=F32)
    tot1 = jnp.sum(oh1, axis=0, keepdims=True)
    tot2 = jnp.sum(oh2, axis=0, keepdims=True)
    run = run_ref[...]
    r1 = jnp.sum(oh1 * (run + before1), axis=-1, keepdims=True)
    r2 = jnp.sum(oh2 * (run + tot1 + before2), axis=-1, keepdims=True)
    run = run + tot1 + tot2
    run_ref[...] = run
    cnt_ref[...] = run

    slab = jnp.zeros(logits.shape, F32)
    for cidx, val in ((COL_E0, jnp.where(valid, e1, -1.0)),
                      (COL_E1, jnp.where(valid, e2, -1.0)),
                      (COL_W0, w1), (COL_W1, w2), (COL_R0, r1), (COL_R1, r2)):
        slab = jnp.where(col == cidx, val, slab)
    route_ref[...] = slab


def _mixer(lp, attn, u, ga, gp, h, pmix, pscale, wa, wp, wo, g2, wr, br):
    rows, d = h.shape
    tm = ROW_TILE
    halo_blocks = tm // POOL_HALO
    row = lambda i: (i, 0)
    return pl.pallas_call(
        functools.partial(_mixer_kernel, lp),
        grid=(rows // tm,),
        in_specs=[pl.BlockSpec((tm, attn.shape[1]), row),
                  pl.BlockSpec((tm, u.shape[1]), row),
                  pl.BlockSpec((POOL_HALO, u.shape[1]),
                               lambda i: (jnp.maximum(i * halo_blocks - 1, 0), 0)),
                  pl.BlockSpec((tm, d), row),
                  pl.BlockSpec((tm, d), row),
                  pl.BlockSpec((tm, d), row),
                  _resident(pmix.shape), _resident(pscale.shape), _resident(wa.shape),
                  _resident(wp.shape), _resident(wo.shape), _resident(g2.shape),
                  _resident(wr.shape), _resident(br.shape)],
        out_specs=[pl.BlockSpec((tm, d), row),
                   pl.BlockSpec((tm, d // 2), row),
                   pl.BlockSpec((tm, LANES), row),
                   pl.BlockSpec((1, LANES), lambda i: (0, 0))],
        out_shape=[jax.ShapeDtypeStruct((rows, d), F32),
                   jax.ShapeDtypeStruct((rows, d // 2), U32),
                   jax.ShapeDtypeStruct((rows, LANES), F32),
                   jax.ShapeDtypeStruct((1, LANES), F32)],
        scratch_shapes=[pltpu.VMEM((1, LANES), F32)],
        compiler_params=_params(("arbitrary",)),
        name="mixer",
    )(attn, u, u, ga, gp, h, pmix, pscale, wa, wp, wo, g2, wr, br)


def _scatter_kernel(pos_ref, x_ref, init_ref, xs_ref, sem):
    del init_ref
    i = pl.program_id(0)
    tm = x_ref.shape[0]

    def row_copy(r, p):
        return pltpu.make_async_copy(x_ref.at[pl.ds(r, 1), :], xs_ref.at[pl.ds(p, 1), :], sem)

    def start(r, carry):
        for k in range(2):
            p = pos_ref[(i * tm + r) * 2 + k]

            @pl.when(p >= 0)
            def _():
                row_copy(r, p).start()
        return carry

    def wait(r, carry):
        for k in range(2):
            p = pos_ref[(i * tm + r) * 2 + k]

            @pl.when(p >= 0)
            def _():
                row_copy(r, p).wait()
        return carry

    lax.fori_loop(0, tm, start, 0)
    lax.fori_loop(0, tm, wait, 0)


def _scatter(pos, xp, slots):
    rows, half = xp.shape
    tm = ROW_TILE
    return pl.pallas_call(
        _scatter_kernel,
        grid_spec=pltpu.PrefetchScalarGridSpec(
            num_scalar_prefetch=1,
            grid=(rows // tm,),
            in_specs=[pl.BlockSpec((tm, half), lambda i, pos: (i, 0)),
                      pl.BlockSpec(memory_space=pl.ANY)],
            out_specs=pl.BlockSpec(memory_space=pl.ANY),
            scratch_shapes=[pltpu.SemaphoreType.DMA(())],
        ),
        out_shape=jax.ShapeDtypeStruct((slots, half), U32),
        input_output_aliases={2: 0},
        compiler_params=_params(("arbitrary",)),
        name="scatter",
    )(pos, xp, jnp.zeros((slots, half), U32))


def _expert_kernel(be_ref, nrow_ref, last_ref, xs_ref, wg_ref, wu_ref, wd_ref, ys_ref, wg_s, wu_s, wd_s):
    b = pl.program_id(0)
    n_valid_rows = nrow_ref[b]
    prev = be_ref[jnp.maximum(b - 1, 0)]

    @pl.when((b == 0) | (be_ref[b] != prev))
    def _():
        wg_s[...] = wg_ref[0].astype(BF16)
        wu_s[...] = wu_ref[0].astype(BF16)
        wd_s[...] = wd_ref[0].astype(BF16)

    @pl.when(n_valid_rows == 0)
    def _():
        ys_ref[...] = jnp.zeros_like(ys_ref)

    @pl.when(n_valid_rows > 0)
    def _():
        packed = xs_ref[...]
        lo = lax.bitcast_convert_type(packed << 16, F32)
        hi = lax.bitcast_convert_type(packed & jnp.uint32(0xFFFF0000), F32)
        x = jnp.concatenate([lo, hi], axis=1).astype(BF16)
        gate = jnp.dot(x, wg_s[...], preferred_element_type=F32)
        up = jnp.dot(x, wu_s[...], preferred_element_type=F32)
        hb = (jax.nn.silu(gate) * up).astype(BF16)
        ys_ref[...] = jnp.dot(hb, wd_s[...], preferred_element_type=F32)


def _experts(block_expert, block_rows, last_block, xs, w_gate, w_up, w_down):
    slots, half = xs.shape
    n_exp, d, de = w_gate.shape
    bm = EXPERT_ROWS
    n_blocks = slots // bm
    blk = lambda b, be, nr, last: (jnp.minimum(b, last[0]), 0)
    wsel = lambda b, be, nr, last: (be[b], 0, 0)
    return pl.pallas_call(
        _expert_kernel,
        grid_spec=pltpu.PrefetchScalarGridSpec(
            num_scalar_prefetch=3,
            grid=(n_blocks,),
            in_specs=[pl.BlockSpec((bm, half), blk),
                      pl.BlockSpec((1, d, de), wsel),
                      pl.BlockSpec((1, d, de), wsel),
                      pl.BlockSpec((1, de, d), wsel)],
            out_specs=pl.BlockSpec((bm, d), lambda b, be, nr, last: (b, 0)),
            scratch_shapes=[pltpu.VMEM((d, de), BF16), pltpu.VMEM((d, de), BF16),
                            pltpu.VMEM((de, d), BF16)],
        ),
        out_shape=jax.ShapeDtypeStruct((slots, d), F32),
        compiler_params=_params(("arbitrary",)),
        name="experts",
    )(block_expert, block_rows, last_block, xs, w_gate, w_up, w_down)


def _combine_kernel(nb, pos_ref, h2_ref, route_ref, g_ref, ys_ref, o_ref, y0_ref, y1_ref, sem):
    b = pl.program_id(0)
    j = pl.program_id(1)
    tm = h2_ref.shape[0]
    base = (b * nb + 1 + j) * tm
    bufs = (y0_ref, y1_ref)

    def row_copy(r, k):
        p = pos_ref[(base + r) * 2 + k]
        return pltpu.make_async_copy(ys_ref.at[pl.ds(p, 1), :], bufs[k].at[pl.ds(r, 1), :], sem)

    def start(r, carry):
        for k in range(2):
            row_copy(r, k).start()
        return carry

    def wait(r, carry):
        for k in range(2):
            row_copy(r, k).wait()
        return carry

    lax.fori_loop(0, tm, start, 0)
    lax.fori_loop(0, tm, wait, 0)
    route = route_ref[...]
    w0 = route[:, COL_W0:COL_W0 + 1]
    w1 = route[:, COL_W1:COL_W1 + 1]
    h = h2_ref[...] + (w0 * y0_ref[...] + w1 * y1_ref[...])
    ms = jnp.mean(h * h, axis=-1, keepdims=True)
    o_ref[0] = h * lax.rsqrt(ms + RMS_EPS) * g_ref[...]


def _combine(pos, h2, route, g, ys, batch, nb, seq):
    rows, d = h2.shape
    tm = BLOCK
    per_batch = seq // tm
    tile = lambda b, j, pos: (b * nb + 1 + j, 0)
    return pl.pallas_call(
        functools.partial(_combine_kernel, nb),
        grid_spec=pltpu.PrefetchScalarGridSpec(
            num_scalar_prefetch=1,
            grid=(batch, per_batch),
            in_specs=[pl.BlockSpec((tm, d), tile),
                      pl.BlockSpec((tm, LANES), tile),
                      pl.BlockSpec((1, d), lambda b, j, pos: (0, 0)),
                      pl.BlockSpec(memory_space=pl.ANY)],
            out_specs=pl.BlockSpec((1, tm, d), lambda b, j, pos: (b, j, 0)),
            scratch_shapes=[pltpu.VMEM((tm, d), F32), pltpu.VMEM((tm, d), F32),
                            pltpu.SemaphoreType.DMA(())],
        ),
        out_shape=jax.ShapeDtypeStruct((batch, seq, d), F32),
        compiler_params=_params(("arbitrary", "arbitrary")),
        name="combine",
    )(pos, h2, route, g, ys)


def kernel(x, meta_tokens, rel_bias, norm_mix, w_in, attn_sinks, pool_mix, pool_scale,
           w_attn_branch, w_pool_branch, w_out, norm_ffn, w_router_group, b_router_group,
           w_router_expert, b_router_expert, w_gate, w_up, w_down, norm_final):
    batch, seq, d = x.shape
    depth = w_in.shape[0]
    assert depth == 1, "single-layer stack"
    aw = w_attn_branch.shape[1]
    pw = w_pool_branch.shape[1]
    kvw = 2 * N_KV_HEADS * HEAD_DIM
    assert aw == N_Q_HEADS * HEAD_DIM and w_in.shape[2] == aw + kvw + pw + 2 * d
    assert seq % BLOCK == 0
    lp = seq + BLOCK
    nb = lp // BLOCK
    rows = batch * lp
    assert rows % ROW_TILE == 0

    lead = jnp.concatenate([jnp.zeros((PAD, d), x.dtype), meta_tokens.astype(x.dtype)], axis=0)
    h = jnp.concatenate([jnp.broadcast_to(lead[None], (batch, BLOCK, d)), x], axis=1).reshape(rows, d)

    q, kv, u, ga, gp = _inproj(h, norm_mix[0][None], w_in[0].astype(BF16), (aw, kvw, pw, d, d))

    rb = rel_bias.astype(F32)
    attn = _attention(q, kv, _window_bias(rb), rb[N_BUCKETS - 1], attn_sinks[0].astype(F32), batch, nb)

    wr = jnp.zeros((d, LANES), F32)
    wr = wr.at[:, :N_GROUPS].set(w_router_group[0]).at[:, N_GROUPS:N_GROUPS + N_EXPERTS].set(w_router_expert[0])
    br = jnp.zeros((1, LANES), F32)
    br = br.at[0, :N_GROUPS].set(b_router_group[0]).at[0, N_GROUPS:N_GROUPS + N_EXPERTS].set(b_router_expert[0])
    h2, xp, route, counts = _mixer(
        lp, attn, u, ga, gp, h, pool_mix[0].astype(BF16), pool_scale[0][None].astype(F32),
        w_attn_branch[0].astype(BF16), w_pool_branch[0].astype(BF16), w_out[0].astype(BF16),
        norm_ffn[0][None], wr, br)

    bm = EXPERT_ROWS
    n_tok = batch * (seq + N_META)
    n_blocks = (2 * n_tok) // bm + N_EXPERTS
    cnt = counts[0, :N_EXPERTS].astype(I32)
    blocks_e = (cnt + bm - 1) // bm
    bend = jnp.cumsum(blocks_e)
    bstart = bend - blocks_e
    blk = jnp.arange(n_blocks, dtype=I32)
    last_block = jnp.maximum(bend[-1] - 1, 0).astype(I32)
    block_expert = jnp.searchsorted(bend, jnp.minimum(blk, last_block), side="right")
    block_expert = jnp.minimum(block_expert, N_EXPERTS - 1).astype(I32)
    block_rows = jnp.clip(cnt[block_expert] - (blk - bstart[block_expert]) * bm, 0, bm)
    block_rows = jnp.where(blk < bend[-1], block_rows, 0).astype(I32)
    last_block = last_block.reshape(1)
    e = route[:, COL_E0:COL_E1 + 1].astype(I32)
    rank = route[:, COL_R0:COL_R1 + 1].astype(I32)
    pos = jnp.where(e >= 0, bstart[jnp.maximum(e, 0)] * bm + rank, -1).astype(I32).reshape(-1)

    xs = _scatter(pos, xp, n_blocks * bm)
    ys = _experts(block_expert, block_rows, last_block, xs, w_gate[0], w_up[0], w_down[0])
    return _combine(pos, h2, route, norm_final[None].astype(F32), ys, batch, nb, seq)
```

```python
import functools
import math

import numpy as np
import jax
import jax.numpy as jnp
from jax import lax
from jax.experimental import pallas as pl
from jax.experimental.pallas import tpu as pltpu

F32 = jnp.float32
BF16 = jnp.bfloat16
I32 = jnp.int32
U32 = jnp.uint32

BLOCK = 128
N_META = 16
PAD = BLOCK - N_META
HEAD_DIM = 64
N_KV_HEADS = 2
Q_PER_KV = 8
N_Q_HEADS = N_KV_HEADS * Q_PER_KV
WINDOW = 128
POOL_WINDOWS = (2, 4, 8, 16)
POOL_HALO = 16
N_BUCKETS = 32
MAX_DISTANCE = 128
N_GROUPS = 8
EXPERTS_PER_GROUP = 8
N_EXPERTS = N_GROUPS * EXPERTS_PER_GROUP
RMS_EPS = 1e-6
LANES = 128
VMEM_LIMIT = 56 * 1024 * 1024

ROW_TILE = 256
EXPERT_ROWS = 256
DMA_UNROLL = 8
COL_E0, COL_E1, COL_W0, COL_W1, COL_R0, COL_R1 = 0, 1, 2, 3, 4, 5


def _params(sem, vmem=VMEM_LIMIT):
    return pltpu.CompilerParams(dimension_semantics=sem, vmem_limit_bytes=vmem)


def _resident(shape):
    nd = len(shape)
    return pl.BlockSpec(shape, lambda *_: (0,) * nd, pipeline_mode=pl.Buffered(1))


def _inproj_kernel(h_ref, g_ref, w_ref, q_ref, kv_ref, u_ref, ga_ref, gp_ref):
    x = h_ref[...]
    ms = jnp.mean(x * x, axis=-1, keepdims=True)
    hn = (x * lax.rsqrt(ms + RMS_EPS) * g_ref[...]).astype(BF16)
    off = 0
    for ref in (q_ref, kv_ref, u_ref, ga_ref, gp_ref):
        width = ref.shape[1]
        for c in range(0, width, 1024):
            cw = min(1024, width - c)
            ref[:, c:c + cw] = jnp.dot(hn, w_ref[:, off + c:off + c + cw],
                                       preferred_element_type=F32).astype(BF16)
        off += width


def _inproj(h, g, w, widths):
    rows, d = h.shape
    tm = ROW_TILE
    outs = [jax.ShapeDtypeStruct((rows, wd), BF16) for wd in widths]
    return pl.pallas_call(
        _inproj_kernel,
        grid=(rows // tm,),
        in_specs=[pl.BlockSpec((tm, d), lambda i: (i, 0)),
                  _resident((1, d)),
                  _resident(w.shape)],
        out_specs=[pl.BlockSpec((tm, wd), lambda i: (i, 0)) for wd in widths],
        out_shape=outs,
        compiler_params=_params(("parallel",)),
        name="inproj",
    )(h, g, w)


PAIR = 2 * HEAD_DIM
KEYS = 3 * BLOCK
N_PAIRS = N_Q_HEADS // 2


def _attn_kernel(sink_ref, q_ref, kvp_ref, kvc_ref, kvm_ref, bias_ref, o_ref):
    kw = N_KV_HEADS * HEAD_DIM
    zeros = jnp.zeros((KEYS, HEAD_DIM), BF16)
    ones = jnp.ones((KEYS, HEAD_DIM), BF16)
    lane = lax.broadcasted_iota(I32, (BLOCK, PAIR), 1)
    scale = jnp.asarray(HEAD_DIM ** -0.5, BF16)
    for hk in range(N_KV_HEADS):
        ks = slice(hk * HEAD_DIM, (hk + 1) * HEAD_DIM)
        vs = slice(kw + hk * HEAD_DIM, kw + (hk + 1) * HEAD_DIM)
        k3 = jnp.concatenate([kvp_ref[:, ks], kvc_ref[:, ks], kvm_ref[:, ks]], axis=0) * scale
        v3 = jnp.concatenate([kvp_ref[:, vs], kvc_ref[:, vs], kvm_ref[:, vs]], axis=0)
        kbd = jnp.concatenate([jnp.concatenate([k3, zeros], axis=1),
                               jnp.concatenate([zeros, k3], axis=1)], axis=0)
        vext = jnp.concatenate([jnp.concatenate([v3, zeros, ones, zeros], axis=1),
                                jnp.concatenate([zeros, v3, zeros, ones], axis=1)], axis=0)
        for jp in range(Q_PER_KV // 2):
            j = hk * (Q_PER_KV // 2) + jp
            qp = q_ref[:, j * PAIR:(j + 1) * PAIR]
            s = lax.dot_general(qp, kbd, (((1,), (1,)), ((), ())),
                                preferred_element_type=F32) + bias_ref[0, j]
            probs, sink_terms = [], []
            for side in range(2):
                ss = s[:, side * KEYS:(side + 1) * KEYS]
                sink = sink_ref[2 * j + side]
                mx = jnp.maximum(jnp.max(ss, axis=-1, keepdims=True), sink)
                probs.append(jnp.exp(ss - mx))
                sink_terms.append(jnp.exp(sink - mx))
            p = jnp.concatenate(probs, axis=1).astype(BF16)
            r = jnp.dot(p, vext, preferred_element_type=F32)
            den = r[:, PAIR:] + jnp.where(lane < HEAD_DIM, sink_terms[0], sink_terms[1])
            o_ref[:, j * PAIR:(j + 1) * PAIR] = (r[:, :PAIR] / den).astype(o_ref.dtype)


def _attention(q, kv, bias, sinks, batch, nb):
    rows, aw = q.shape
    kvw = kv.shape[1]
    return pl.pallas_call(
        _attn_kernel,
        grid=(batch, nb),
        in_specs=[pl.BlockSpec(memory_space=pltpu.SMEM),
                  pl.BlockSpec((BLOCK, aw), lambda b, n: (b * nb + n, 0)),
                  pl.BlockSpec((BLOCK, kvw), lambda b, n: (b * nb + jnp.maximum(n - 1, 0), 0)),
                  pl.BlockSpec((BLOCK, kvw), lambda b, n: (b * nb + n, 0)),
                  pl.BlockSpec((BLOCK, kvw), lambda b, n: (b * nb, 0)),
                  pl.BlockSpec((1,) + bias.shape[1:], lambda b, n: (jnp.minimum(n, 2), 0, 0, 0))],
        out_specs=pl.BlockSpec((BLOCK, aw), lambda b, n: (b * nb + n, 0)),
        out_shape=jax.ShapeDtypeStruct((rows, aw), BF16),
        compiler_params=_params(("parallel", "arbitrary")),
        name="attention",
    )(sinks, q, kv, kv, kv, bias)


def _attn_bias_tables(rel_bias):
    max_exact = N_BUCKETS // 2
    qi = np.arange(BLOCK)[:, None]
    kj = np.arange(2 * BLOCK)[None, :]
    mj = np.arange(BLOCK)[None, :]
    dist = qi + BLOCK - kj
    d = np.maximum(dist, 0)
    large = max_exact + (np.log(np.maximum(d, 1).astype(np.float32) / max_exact)
                         / math.log(MAX_DISTANCE / max_exact) * (N_BUCKETS - max_exact)).astype(np.int32)
    bucket = np.where(d < max_exact, d, np.minimum(large, N_BUCKETS - 1))
    onehot = (bucket[..., None] == np.arange(N_BUCKETS)).astype(np.float32)
    rb = rel_bias.astype(F32)
    win = jnp.einsum("qkb,bh->hqk", onehot, rb, precision=lax.Precision.HIGHEST)
    meta = jnp.broadcast_to(rb[N_BUCKETS - 1][:, None, None], (N_Q_HEADS, BLOCK, BLOCK))
    full = jnp.concatenate([win, meta], axis=2)
    masks = []
    for n in range(3):
        win_ok = (dist >= 0) & (dist < WINDOW) & ((n - 1) * BLOCK + kj >= PAD)
        meta_ok = (mj >= PAD) & (n * BLOCK + qi - mj >= WINDOW)
        masks.append(np.concatenate([win_ok, meta_ok], axis=1))
    t = jnp.where(np.stack(masks)[:, None], full[None], -jnp.inf)
    t = t.reshape(3, N_PAIRS, 2, BLOCK, KEYS).transpose(0, 1, 3, 2, 4)
    return t.reshape(3, N_PAIRS, BLOCK, 2 * KEYS)


def _mixer_kernel(lp, attn_ref, u_ref, uh_ref, ga_ref, gp_ref, h_ref,
                  pmix_ref, pscale_ref, wa_ref, wp_ref, wo_ref, g2_ref, wr_ref, br_ref,
                  h2_ref, xp_ref, route_ref, cnt_ref, run_ref):
    i = pl.program_id(0)
    tm = h_ref.shape[0]

    @pl.when(i == 0)
    def _():
        run_ref[...] = jnp.zeros_like(run_ref)

    t = (i * tm + lax.broadcasted_iota(I32, (tm, 1), 0)) % lp
    valid = t >= PAD
    tx = (i * tm - POOL_HALO + lp + lax.broadcasted_iota(I32, (tm + POOL_HALO, 1), 0)) % lp
    uext = jnp.concatenate([uh_ref[...], u_ref[...]], axis=0).astype(F32)
    ug = jnp.where(tx >= PAD, uext, 0.0)
    n_rows = (t - PAD + 1).astype(F32)
    gw = ug.shape[1] // len(POOL_WINDOWS)
    pooled = []
    for gi, w in enumerate(POOL_WINDOWS):
        c = ug[:, gi * gw:(gi + 1) * gw]
        s, span = c, 1
        while span < w:
            s = s[span:] + s[:-span]
            span *= 2
        win = s[POOL_HALO + 1 - w:POOL_HALO + 1 - w + tm]
        n_valid = jnp.clip(n_rows, 1.0, float(w))
        mixed = jnp.where(valid, win / n_valid - c[POOL_HALO:], 0.0)
        pooled.append(jnp.dot(mixed.astype(BF16), pmix_ref[gi], preferred_element_type=F32))
    pool = jnp.concatenate(pooled, axis=1) * pscale_ref[...]

    a = jnp.dot(attn_ref[...], wa_ref[...], preferred_element_type=F32)
    p = jnp.dot(pool.astype(BF16), wp_ref[...], preferred_element_type=F32)
    merged = (jax.nn.sigmoid(ga_ref[...].astype(F32)) * a
              + jax.nn.sigmoid(gp_ref[...].astype(F32)) * p)
    h2 = h_ref[...] + jnp.dot(merged.astype(BF16), wo_ref[...], preferred_element_type=F32)
    h2_ref[...] = h2

    ms = jnp.mean(h2 * h2, axis=-1, keepdims=True)
    hn2 = h2 * lax.rsqrt(ms + RMS_EPS) * g2_ref[...]
    x_hi = hn2.astype(BF16)
    x_hi32 = x_hi.astype(F32)
    half = hn2.shape[1] // 2
    lo = lax.bitcast_convert_type(x_hi32[:, :half], U32)
    hi = lax.bitcast_convert_type(x_hi32[:, half:], U32)
    xp_ref[...] = (hi & jnp.uint32(0xFFFF0000)) | (lo >> 16)

    x_lo = (hn2 - x_hi32).astype(BF16)
    hi_prod = jnp.dot(x_hi, wr_ref[...], preferred_element_type=F32)
    lo_prod = jnp.dot(x_lo, wr_ref[:, :LANES], preferred_element_type=F32)
    logits = hi_prod[:, :LANES] + (hi_prod[:, LANES:] + lo_prod) + br_ref[...]
    col = lax.broadcasted_iota(I32, logits.shape, 1).astype(F32)
    neg = -jnp.inf
    gl = jnp.where(col < N_GROUPS, logits, neg)
    gmax = jnp.max(gl, axis=-1, keepdims=True)
    grp = jnp.min(jnp.where(gl == gmax, col, float(LANES)), axis=-1, keepdims=True)
    p_grp = 1.0 / jnp.sum(jnp.exp(gl - gmax), axis=-1, keepdims=True)
    e_lo = N_GROUPS + grp * EXPERTS_PER_GROUP
    el = jnp.where((col >= e_lo) & (col < e_lo + EXPERTS_PER_GROUP), logits, neg)
    m1 = jnp.max(el, axis=-1, keepdims=True)
    i1 = jnp.min(jnp.where(el == m1, col, float(LANES)), axis=-1, keepdims=True)
    el2 = jnp.where(col == i1, neg, el)
    m2 = jnp.max(el2, axis=-1, keepdims=True)
    i2 = jnp.min(jnp.where(el2 == m2, col, float(LANES)), axis=-1, keepdims=True)
    z = jnp.exp(m2 - m1)
    w1 = p_grp / (1.0 + z)
    w2 = p_grp * z / (1.0 + z)
    e1 = i1 - N_GROUPS
    e2 = i2 - N_GROUPS

    oh1 = jnp.where((col == e1) & valid, 1.0, 0.0)
    oh2 = jnp.where((col == e2) & valid, 1.0, 0.0)
    lower = (lax.broadcasted_iota(I32, (tm, tm), 0) > lax.broadcasted_iota(I32, (tm, tm), 1))
    lower = jnp.where(lower, 1.0, 0.0).astype(BF16)
    before1 = jnp.dot(lower, oh1.astype(BF16), preferred_element_type=F32)
    before2 = jnp.dot(lower, oh2.astype(BF16), preferred_element_type=F32)
    tot1 = jnp.sum(oh1, axis=0, keepdims=True)
    tot2 = jnp.sum(oh2, axis=0, keepdims=True)
    run = run_ref[...]
    r1 = jnp.sum(oh1 * (run + before1), axis=-1, keepdims=True)
    r2 = jnp.sum(oh2 * (run + tot1 + before2), axis=-1, keepdims=True)
    run = run + tot1 + tot2
    run_ref[...] = run
    cnt_ref[...] = run

    slab = jnp.zeros(logits.shape, F32)
    for cidx, val in ((COL_E0, jnp.where(valid, e1, -1.0)),
                      (COL_E1, jnp.where(valid, e2, -1.0)),
                      (COL_W0, w1), (COL_W1, w2), (COL_R0, r1), (COL_R1, r2)):
        slab = jnp.where(col == cidx, val, slab)
    route_ref[...] = slab


def _mixer(lp, attn, u, ga, gp, h, pmix, pscale, wa, wp, wo, g2, wr, br):
    rows, d = h.shape
    tm = ROW_TILE
    halo_blocks = tm // POOL_HALO
    row = lambda i: (i, 0)
    return pl.pallas_call(
        functools.partial(_mixer_kernel, lp),
        grid=(rows // tm,),
        in_specs=[pl.BlockSpec((tm, attn.shape[1]), row),
                  pl.BlockSpec((tm, u.shape[1]), row),
                  pl.BlockSpec((POOL_HALO, u.shape[1]),
                               lambda i: (jnp.maximum(i * halo_blocks - 1, 0), 0)),
                  pl.BlockSpec((tm, d), row),
                  pl.BlockSpec((tm, d), row),
                  pl.BlockSpec((tm, d), row),
                  _resident(pmix.shape), _resident(pscale.shape), _resident(wa.shape),
                  _resident(wp.shape), _resident(wo.shape), _resident(g2.shape),
                  _resident(wr.shape), _resident(br.shape)],
        out_specs=[pl.BlockSpec((tm, d), row),
                   pl.BlockSpec((tm, d // 2), row),
                   pl.BlockSpec((tm, LANES), row),
                   pl.BlockSpec((1, LANES), lambda i: (0, 0))],
        out_shape=[jax.ShapeDtypeStruct((rows, d), F32),
                   jax.ShapeDtypeStruct((rows, d // 2), U32),
                   jax.ShapeDtypeStruct((rows, LANES), F32),
                   jax.ShapeDtypeStruct((1, LANES), F32)],
        scratch_shapes=[pltpu.VMEM((1, LANES), F32)],
        compiler_params=_params(("arbitrary",)),
        name="mixer",
    )(attn, u, u, ga, gp, h, pmix, pscale, wa, wp, wo, g2, wr, br)


def _scatter_kernel(pos_ref, x_ref, init_ref, xs_ref, sem):
    del init_ref
    i = pl.program_id(0)
    tm = x_ref.shape[0]

    def start(it, carry):
        r0 = pl.multiple_of(it * DMA_UNROLL, DMA_UNROLL)
        for j in range(DMA_UNROLL):
            for k in range(2):
                p = pos_ref[(i * tm + r0 + j) * 2 + k]
                pltpu.make_async_copy(x_ref.at[pl.ds(r0 + j, 1), :], xs_ref.at[pl.ds(p, 1), :],
                                      sem).start(priority=k)
        return carry

    lax.fori_loop(0, tm // DMA_UNROLL, start, 0)
    for k in range(2):
        pltpu.make_async_copy(x_ref, xs_ref.at[pl.ds(0, tm), :], sem).wait()


def _scatter(pos, xp, slots):
    rows, half = xp.shape
    tm = ROW_TILE
    return pl.pallas_call(
        _scatter_kernel,
        grid_spec=pltpu.PrefetchScalarGridSpec(
            num_scalar_prefetch=1,
            grid=(rows // tm,),
            in_specs=[pl.BlockSpec((tm, half), lambda i, pos: (i, 0)),
                      pl.BlockSpec(memory_space=pl.ANY)],
            out_specs=pl.BlockSpec(memory_space=pl.ANY),
            scratch_shapes=[pltpu.SemaphoreType.DMA(())],
        ),
        out_shape=jax.ShapeDtypeStruct((slots + 2 * tm, half), U32),
        input_output_aliases={2: 0},
        compiler_params=_params(("arbitrary",)),
        name="scatter",
    )(pos, xp, jnp.zeros((slots + 2 * tm, half), U32))


def _expert_kernel(be_ref, nrow_ref, last_ref, xs_ref, wg_ref, wu_ref, wd_ref, ys_ref, wg_s, wu_s, wd_s):
    b = pl.program_id(0)
    n_valid_rows = nrow_ref[b]
    prev = be_ref[jnp.maximum(b - 1, 0)]

    @pl.when((b == 0) | (be_ref[b] != prev))
    def _():
        wg_s[...] = wg_ref[0].astype(BF16)
        wu_s[...] = wu_ref[0].astype(BF16)
        wd_s[...] = wd_ref[0].astype(BF16)

    @pl.when(n_valid_rows == 0)
    def _():
        ys_ref[...] = jnp.zeros_like(ys_ref)

    @pl.when(n_valid_rows > 0)
    def _():
        packed = xs_ref[...]
        lo = lax.bitcast_convert_type(packed << 16, F32)
        hi = lax.bitcast_convert_type(packed & jnp.uint32(0xFFFF0000), F32)
        x = jnp.concatenate([lo, hi], axis=1).astype(BF16)
        gate = jnp.dot(x, wg_s[...], preferred_element_type=F32)
        up = jnp.dot(x, wu_s[...], preferred_element_type=F32)
        hb = (jax.nn.silu(gate) * up).astype(BF16)
        ys_ref[...] = jnp.dot(hb, wd_s[...], preferred_element_type=F32)


def _experts(block_expert, block_rows, last_block, xs, w_gate, w_up, w_down):
    half = xs.shape[1]
    n_exp, d, de = w_gate.shape
    bm = EXPERT_ROWS
    n_blocks = block_expert.shape[0]
    slots = n_blocks * bm
    blk = lambda b, be, nr, last: (jnp.minimum(b, last[0]), 0)
    wsel = lambda b, be, nr, last: (be[b], 0, 0)
    return pl.pallas_call(
        _expert_kernel,
        grid_spec=pltpu.PrefetchScalarGridSpec(
            num_scalar_prefetch=3,
            grid=(n_blocks,),
            in_specs=[pl.BlockSpec((bm, half), blk),
                      pl.BlockSpec((1, d, de), wsel),
                      pl.BlockSpec((1, d, de), wsel),
                      pl.BlockSpec((1, de, d), wsel)],
            out_specs=pl.BlockSpec((bm, d), lambda b, be, nr, last: (b, 0)),
            scratch_shapes=[pltpu.VMEM((d, de), BF16), pltpu.VMEM((d, de), BF16),
                            pltpu.VMEM((de, d), BF16)],
        ),
        out_shape=jax.ShapeDtypeStruct((slots, d), F32),
        compiler_params=_params(("arbitrary",)),
        name="experts",
    )(block_expert, block_rows, last_block, xs, w_gate, w_up, w_down)


def _combine_kernel(nb, per_batch, pos_ref, h2_ref, route_ref, g_ref, ys_ref, o_ref, y_ref, sem):
    t = pl.program_id(0)
    tm = h2_ref.shape[0]

    def issue(tile, slot):
        base = ((tile // per_batch) * nb + 1 + tile % per_batch) * tm

        def body(it, carry):
            r0 = pl.multiple_of(it * DMA_UNROLL, DMA_UNROLL)
            for j in range(DMA_UNROLL):
                for k in range(2):
                    p = pos_ref[(base + r0 + j) * 2 + k]
                    pltpu.make_async_copy(ys_ref.at[pl.ds(p, 1), :], y_ref.at[slot, k, pl.ds(r0 + j, 1), :],
                                          sem.at[slot]).start(priority=k)
            return carry

        lax.fori_loop(0, tm // DMA_UNROLL, body, 0)

    @pl.when(t == 0)
    def _():
        issue(0, 0)

    @pl.when(t + 1 < pl.num_programs(0))
    def _():
        issue(t + 1, (t + 1) % 2)

    slot = t % 2
    for k in range(2):
        pltpu.make_async_copy(ys_ref.at[pl.ds(0, tm), :], y_ref.at[slot, k], sem.at[slot]).wait()
    route = route_ref[...]
    w0 = route[:, COL_W0:COL_W0 + 1]
    w1 = route[:, COL_W1:COL_W1 + 1]
    h = h2_ref[...] + (w0 * y_ref[slot, 0] + w1 * y_ref[slot, 1])
    ms = jnp.mean(h * h, axis=-1, keepdims=True)
    o_ref[0] = h * lax.rsqrt(ms + RMS_EPS) * g_ref[...]


def _combine(pos, h2, route, g, ys, batch, nb, seq):
    rows, d = h2.shape
    tm = BLOCK
    per_batch = seq // tm
    tile = lambda t, pos: ((t // per_batch) * nb + 1 + t % per_batch, 0)
    return pl.pallas_call(
        functools.partial(_combine_kernel, nb, per_batch),
        grid_spec=pltpu.PrefetchScalarGridSpec(
            num_scalar_prefetch=1,
            grid=(batch * per_batch,),
            in_specs=[pl.BlockSpec((tm, d), tile),
                      pl.BlockSpec((tm, LANES), tile),
                      pl.BlockSpec((1, d), lambda t, pos: (0, 0)),
                      pl.BlockSpec(memory_space=pl.ANY)],
            out_specs=pl.BlockSpec((1, tm, d), lambda t, pos: (t // per_batch, t % per_batch, 0)),
            scratch_shapes=[pltpu.VMEM((2, 2, tm, d), F32), pltpu.SemaphoreType.DMA((2,))],
        ),
        out_shape=jax.ShapeDtypeStruct((batch, seq, d), F32),
        compiler_params=_params(("arbitrary",)),
        name="combine",
    )(pos, h2, route, g, ys)


def kernel(x, meta_tokens, rel_bias, norm_mix, w_in, attn_sinks, pool_mix, pool_scale,
           w_attn_branch, w_pool_branch, w_out, norm_ffn, w_router_group, b_router_group,
           w_router_expert, b_router_expert, w_gate, w_up, w_down, norm_final):
    batch, seq, d = x.shape
    depth = w_in.shape[0]
    assert depth == 1, "single-layer stack"
    aw = w_attn_branch.shape[1]
    pw = w_pool_branch.shape[1]
    kvw = 2 * N_KV_HEADS * HEAD_DIM
    assert aw == N_Q_HEADS * HEAD_DIM and w_in.shape[2] == aw + kvw + pw + 2 * d
    assert seq % BLOCK == 0
    lp = seq + BLOCK
    nb = lp // BLOCK
    rows = batch * lp
    assert rows % ROW_TILE == 0

    lead = jnp.concatenate([jnp.zeros((PAD, d), x.dtype), meta_tokens.astype(x.dtype)], axis=0)
    h = jnp.concatenate([jnp.broadcast_to(lead[None], (batch, BLOCK, d)), x], axis=1).reshape(rows, d)

    q, kv, u, ga, gp = _inproj(h, norm_mix[0][None], w_in[0].astype(BF16), (aw, kvw, pw, d, d))

    attn = _attention(q, kv, _attn_bias_tables(rel_bias), attn_sinks[0].astype(F32), batch, nb)

    wr = jnp.zeros((d, LANES), F32)
    wr = wr.at[:, :N_GROUPS].set(w_router_group[0]).at[:, N_GROUPS:N_GROUPS + N_EXPERTS].set(w_router_expert[0])
    br = jnp.zeros((1, LANES), F32)
    br = br.at[0, :N_GROUPS].set(b_router_group[0]).at[0, N_GROUPS:N_GROUPS + N_EXPERTS].set(b_router_expert[0])
    wr_hi = wr.astype(BF16)
    wr_split = jnp.concatenate([wr_hi, (wr - wr_hi.astype(F32)).astype(BF16)], axis=1)
    h2, xp, route, counts = _mixer(
        lp, attn, u, ga, gp, h, pool_mix[0].astype(BF16), pool_scale[0][None].astype(F32),
        w_attn_branch[0].astype(BF16), w_pool_branch[0].astype(BF16), w_out[0].astype(BF16),
        norm_ffn[0][None], wr_split, br)

    bm = EXPERT_ROWS
    n_tok = batch * (seq + N_META)
    n_blocks = (2 * n_tok) // bm + N_EXPERTS
    cnt = counts[0, :N_EXPERTS].astype(I32)
    blocks_e = (cnt + bm - 1) // bm
    bend = jnp.cumsum(blocks_e)
    bstart = bend - blocks_e
    blk = jnp.arange(n_blocks, dtype=I32)
    last_block = jnp.maximum(bend[-1] - 1, 0).astype(I32)
    block_expert = jnp.searchsorted(bend, jnp.minimum(blk, last_block), side="right")
    block_expert = jnp.minimum(block_expert, N_EXPERTS - 1).astype(I32)
    block_rows = jnp.clip(cnt[block_expert] - (blk - bstart[block_expert]) * bm, 0, bm)
    block_rows = jnp.where(blk < bend[-1], block_rows, 0).astype(I32)
    last_block = last_block.reshape(1)
    e = route[:, COL_E0:COL_E1 + 1].astype(I32)
    rank = route[:, COL_R0:COL_R1 + 1].astype(I32)
    first_row = jnp.sum(jnp.where(e[..., None] == jnp.arange(N_EXPERTS, dtype=I32), bstart * bm, 0), axis=-1)
    spare = n_blocks * bm + jnp.arange(2 * rows, dtype=I32).reshape(rows, 2) % (2 * ROW_TILE)
    pos = jnp.where(e >= 0, first_row + rank, spare).astype(I32).reshape(-1)

    xs = _scatter(pos, xp, n_blocks * bm)
    ys = _experts(block_expert, block_rows, last_block, xs, w_gate[0], w_up[0], w_down[0])
    return _combine(pos, h2, route, norm_final[None].astype(F32), ys, batch, nb, seq)
```

```python
import functools
import math

import numpy as np
import jax
import jax.numpy as jnp
from jax import lax
from jax.experimental import pallas as pl
from jax.experimental.pallas import tpu as pltpu

F32 = jnp.float32
BF16 = jnp.bfloat16
I32 = jnp.int32
U32 = jnp.uint32

BLOCK = 128
N_META = 16
PAD = BLOCK - N_META
HEAD_DIM = 64
N_KV_HEADS = 2
Q_PER_KV = 8
N_Q_HEADS = N_KV_HEADS * Q_PER_KV
WINDOW = 128
POOL_WINDOWS = (2, 4, 8, 16)
POOL_HALO = 16
N_BUCKETS = 32
MAX_DISTANCE = 128
N_GROUPS = 8
EXPERTS_PER_GROUP = 8
N_EXPERTS = N_GROUPS * EXPERTS_PER_GROUP
RMS_EPS = 1e-6
LANES = 128
VMEM_LIMIT = 56 * 1024 * 1024

ROW_TILE = 256
EXPERT_ROWS = 256
DMA_UNROLL = 8
COL_E0, COL_E1, COL_W0, COL_W1, COL_R0, COL_R1 = 0, 1, 2, 3, 4, 5


def _params(sem, vmem=VMEM_LIMIT):
    return pltpu.CompilerParams(dimension_semantics=sem, vmem_limit_bytes=vmem)


def _resident(shape):
    nd = len(shape)
    return pl.BlockSpec(shape, lambda *_: (0,) * nd, pipeline_mode=pl.Buffered(1))


def _tile_block_specs(nb, per_batch, d):
    def spec(half):
        def index(i, *_):
            g = 2 * i + half
            return ((g // nb) * per_batch + jnp.maximum(g % nb - 1, 0), 0)
        return pl.BlockSpec((BLOCK, d), index)
    return [spec(0), spec(1)]


def _tile_rows(nb, lead_ref, xa_ref, xb_ref):
    i = pl.program_id(0)
    halves = [jnp.where((2 * i + half) % nb == 0, lead_ref[...], ref[...])
              for half, ref in enumerate((xa_ref, xb_ref))]
    return jnp.concatenate(halves, axis=0)


def _inproj_kernel(nb, lead_ref, xa_ref, xb_ref, g_ref, w_ref, q_ref, kv_ref, u_ref, ga_ref, gp_ref):
    x = _tile_rows(nb, lead_ref, xa_ref, xb_ref)
    ms = jnp.mean(x * x, axis=-1, keepdims=True)
    hn = (x * lax.rsqrt(ms + RMS_EPS) * g_ref[...]).astype(BF16)
    off = 0
    for ref in (q_ref, kv_ref, u_ref, ga_ref, gp_ref):
        width = ref.shape[1]
        for c in range(0, width, 1024):
            cw = min(1024, width - c)
            ref[:, c:c + cw] = jnp.dot(hn, w_ref[:, off + c:off + c + cw],
                                       preferred_element_type=F32).astype(BF16)
        off += width


def _inproj(lead, x2, nb, g, w, widths):
    d = x2.shape[1]
    tm = ROW_TILE
    per_batch = nb - 1
    rows = x2.shape[0] // per_batch * nb
    outs = [jax.ShapeDtypeStruct((rows, wd), BF16) for wd in widths]
    return pl.pallas_call(
        functools.partial(_inproj_kernel, nb),
        grid=(rows // tm,),
        in_specs=[_resident(lead.shape)] + _tile_block_specs(nb, per_batch, d)
                 + [_resident((1, d)), _resident(w.shape)],
        out_specs=[pl.BlockSpec((tm, wd), lambda i: (i, 0)) for wd in widths],
        out_shape=outs,
        compiler_params=_params(("parallel",)),
        name="inproj",
    )(lead, x2, x2, g, w)


PAIR = 2 * HEAD_DIM
KEYS = 3 * BLOCK
N_PAIRS = N_Q_HEADS // 2


def _attn_kernel(sink_ref, q_ref, kvp_ref, kvc_ref, kvm_ref, bias_ref, o_ref):
    kw = N_KV_HEADS * HEAD_DIM
    zeros = jnp.zeros((KEYS, HEAD_DIM), BF16)
    ones = jnp.ones((KEYS, HEAD_DIM), BF16)
    lane = lax.broadcasted_iota(I32, (BLOCK, PAIR), 1)
    scale = jnp.asarray(HEAD_DIM ** -0.5, BF16)
    for hk in range(N_KV_HEADS):
        ks = slice(hk * HEAD_DIM, (hk + 1) * HEAD_DIM)
        vs = slice(kw + hk * HEAD_DIM, kw + (hk + 1) * HEAD_DIM)
        k3 = jnp.concatenate([kvp_ref[:, ks], kvc_ref[:, ks], kvm_ref[:, ks]], axis=0) * scale
        v3 = jnp.concatenate([kvp_ref[:, vs], kvc_ref[:, vs], kvm_ref[:, vs]], axis=0)
        kbd = jnp.concatenate([jnp.concatenate([k3, zeros], axis=1),
                               jnp.concatenate([zeros, k3], axis=1)], axis=0)
        vext = jnp.concatenate([jnp.concatenate([v3, zeros, ones, zeros], axis=1),
                                jnp.concatenate([zeros, v3, zeros, ones], axis=1)], axis=0)
        for jp in range(Q_PER_KV // 2):
            j = hk * (Q_PER_KV // 2) + jp
            qp = q_ref[:, j * PAIR:(j + 1) * PAIR]
            s = lax.dot_general(qp, kbd, (((1,), (1,)), ((), ())),
                                preferred_element_type=F32) + bias_ref[0, j]
            probs, sink_terms = [], []
            for side in range(2):
                ss = s[:, side * KEYS:(side + 1) * KEYS]
                sink = sink_ref[2 * j + side]
                mx = jnp.maximum(jnp.max(ss, axis=-1, keepdims=True), sink)
                probs.append(jnp.exp(ss - mx))
                sink_terms.append(jnp.exp(sink - mx))
            p = jnp.concatenate(probs, axis=1).astype(BF16)
            r = jnp.dot(p, vext, preferred_element_type=F32)
            den = r[:, PAIR:] + jnp.where(lane < HEAD_DIM, sink_terms[0], sink_terms[1])
            o_ref[:, j * PAIR:(j + 1) * PAIR] = (r[:, :PAIR] / den).astype(o_ref.dtype)


def _attention(q, kv, bias, sinks, batch, nb):
    rows, aw = q.shape
    kvw = kv.shape[1]
    return pl.pallas_call(
        _attn_kernel,
        grid=(batch, nb),
        in_specs=[pl.BlockSpec(memory_space=pltpu.SMEM),
                  pl.BlockSpec((BLOCK, aw), lambda b, n: (b * nb + n, 0)),
                  pl.BlockSpec((BLOCK, kvw), lambda b, n: (b * nb + jnp.maximum(n - 1, 0), 0)),
                  pl.BlockSpec((BLOCK, kvw), lambda b, n: (b * nb + n, 0)),
                  pl.BlockSpec((BLOCK, kvw), lambda b, n: (b * nb, 0)),
                  pl.BlockSpec((1,) + bias.shape[1:], lambda b, n: (jnp.minimum(n, 2), 0, 0, 0))],
        out_specs=pl.BlockSpec((BLOCK, aw), lambda b, n: (b * nb + n, 0)),
        out_shape=jax.ShapeDtypeStruct((rows, aw), BF16),
        compiler_params=_params(("parallel", "arbitrary")),
        name="attention",
    )(sinks, q, kv, kv, kv, bias)


def _attn_bias_tables(rel_bias):
    max_exact = N_BUCKETS // 2
    qi = np.arange(BLOCK)[:, None]
    kj = np.arange(2 * BLOCK)[None, :]
    mj = np.arange(BLOCK)[None, :]
    dist = qi + BLOCK - kj
    d = np.maximum(dist, 0)
    large = max_exact + (np.log(np.maximum(d, 1).astype(np.float32) / max_exact)
                         / math.log(MAX_DISTANCE / max_exact) * (N_BUCKETS - max_exact)).astype(np.int32)
    bucket = np.where(d < max_exact, d, np.minimum(large, N_BUCKETS - 1))
    onehot = (bucket[..., None] == np.arange(N_BUCKETS)).astype(np.float32)
    rb = rel_bias.astype(F32)
    win = jnp.einsum("qkb,bh->hqk", onehot, rb, precision=lax.Precision.HIGHEST)
    meta = jnp.broadcast_to(rb[N_BUCKETS - 1][:, None, None], (N_Q_HEADS, BLOCK, BLOCK))
    full = jnp.concatenate([win, meta], axis=2)
    masks = []
    for n in range(3):
        win_ok = (dist >= 0) & (dist < WINDOW) & ((n - 1) * BLOCK + kj >= PAD)
        meta_ok = (mj >= PAD) & (n * BLOCK + qi - mj >= WINDOW)
        masks.append(np.concatenate([win_ok, meta_ok], axis=1))
    t = jnp.where(np.stack(masks)[:, None], full[None], -jnp.inf)
    t = t.reshape(3, N_PAIRS, 2, BLOCK, KEYS).transpose(0, 1, 3, 2, 4)
    return t.reshape(3, N_PAIRS, BLOCK, 2 * KEYS)


def _mixer_kernel(lp, attn_ref, u_ref, uh_ref, ga_ref, gp_ref, lead_ref, xa_ref, xb_ref,
                  pmix_ref, pscale_ref, wa_ref, wp_ref, wo_ref, g2_ref, wr_ref, br_ref,
                  h2_ref, xp_ref, route_ref, cnt_ref, run_ref):
    i = pl.program_id(0)
    tm = h2_ref.shape[0]

    @pl.when(i == 0)
    def _():
        run_ref[...] = jnp.zeros_like(run_ref)

    t = (i * tm + lax.broadcasted_iota(I32, (tm, 1), 0)) % lp
    valid = t >= PAD
    tx = (i * tm - POOL_HALO + lp + lax.broadcasted_iota(I32, (tm + POOL_HALO, 1), 0)) % lp
    uext = jnp.concatenate([uh_ref[...], u_ref[...]], axis=0).astype(F32)
    ug = jnp.where(tx >= PAD, uext, 0.0)
    n_rows = (t - PAD + 1).astype(F32)
    gw = ug.shape[1] // len(POOL_WINDOWS)
    pooled = []
    for gi, w in enumerate(POOL_WINDOWS):
        c = ug[:, gi * gw:(gi + 1) * gw]
        s, span = c, 1
        while span < w:
            s = s[span:] + s[:-span]
            span *= 2
        win = s[POOL_HALO + 1 - w:POOL_HALO + 1 - w + tm]
        n_valid = jnp.clip(n_rows, 1.0, float(w))
        mixed = jnp.where(valid, win / n_valid - c[POOL_HALO:], 0.0)
        pooled.append(jnp.dot(mixed.astype(BF16), pmix_ref[gi], preferred_element_type=F32))
    pool = jnp.concatenate(pooled, axis=1) * pscale_ref[...]

    a = jnp.dot(attn_ref[...], wa_ref[...], preferred_element_type=F32)
    p = jnp.dot(pool.astype(BF16), wp_ref[...], preferred_element_type=F32)
    merged = (jax.nn.sigmoid(ga_ref[...].astype(F32)) * a
              + jax.nn.sigmoid(gp_ref[...].astype(F32)) * p)
    h2 = (_tile_rows(lp // BLOCK, lead_ref, xa_ref, xb_ref)
          + jnp.dot(merged.astype(BF16), wo_ref[...], preferred_element_type=F32))
    h2_ref[...] = h2

    ms = jnp.mean(h2 * h2, axis=-1, keepdims=True)
    hn2 = h2 * lax.rsqrt(ms + RMS_EPS) * g2_ref[...]
    x_hi = hn2.astype(BF16)
    x_hi32 = x_hi.astype(F32)
    half = hn2.shape[1] // 2
    lo = lax.bitcast_convert_type(x_hi32[:, :half], U32)
    hi = lax.bitcast_convert_type(x_hi32[:, half:], U32)
    xp_ref[...] = (hi & jnp.uint32(0xFFFF0000)) | (lo >> 16)

    x_lo = (hn2 - x_hi32).astype(BF16)
    hi_prod = jnp.dot(x_hi, wr_ref[...], preferred_element_type=F32)
    lo_prod = jnp.dot(x_lo, wr_ref[:, :LANES], preferred_element_type=F32)
    logits = hi_prod[:, :LANES] + (hi_prod[:, LANES:] + lo_prod) + br_ref[...]
    col = lax.broadcasted_iota(I32, logits.shape, 1).astype(F32)
    neg = -jnp.inf
    gl = jnp.where(col < N_GROUPS, logits, neg)
    gmax = jnp.max(gl, axis=-1, keepdims=True)
    grp = jnp.min(jnp.where(gl == gmax, col, float(LANES)), axis=-1, keepdims=True)
    p_grp = 1.0 / jnp.sum(jnp.exp(gl - gmax), axis=-1, keepdims=True)
    e_lo = N_GROUPS + grp * EXPERTS_PER_GROUP
    el = jnp.where((col >= e_lo) & (col < e_lo + EXPERTS_PER_GROUP), logits, neg)
    m1 = jnp.max(el, axis=-1, keepdims=True)
    i1 = jnp.min(jnp.where(el == m1, col, float(LANES)), axis=-1, keepdims=True)
    el2 = jnp.where(col == i1, neg, el)
    m2 = jnp.max(el2, axis=-1, keepdims=True)
    i2 = jnp.min(jnp.where(el2 == m2, col, float(LANES)), axis=-1, keepdims=True)
    z = jnp.exp(m2 - m1)
    w1 = p_grp / (1.0 + z)
    w2 = p_grp * z / (1.0 + z)
    e1 = i1 - N_GROUPS
    e2 = i2 - N_GROUPS

    oh1 = jnp.where((col == e1) & valid, 1.0, 0.0)
    oh2 = jnp.where((col == e2) & valid, 1.0, 0.0)
    lower = (lax.broadcasted_iota(I32, (tm, tm), 0) > lax.broadcasted_iota(I32, (tm, tm), 1))
    lower = jnp.where(lower, 1.0, 0.0).astype(BF16)
    before1 = jnp.dot(lower, oh1.astype(BF16), preferred_element_type=F32)
    before2 = jnp.dot(lower, oh2.astype(BF16), preferred_element_type=F32)
    tot1 = jnp.sum(oh1, axis=0, keepdims=True)
    tot2 = jnp.sum(oh2, axis=0, keepdims=True)
    run = run_ref[...]
    r1 = jnp.sum(oh1 * (run + before1), axis=-1, keepdims=True)
    r2 = jnp.sum(oh2 * (run + tot1 + before2), axis=-1, keepdims=True)
    run = run + tot1 + tot2
    run_ref[...] = run
    cnt_ref[...] = run

    slab = jnp.zeros(logits.shape, F32)
    for cidx, val in ((COL_E0, jnp.where(valid, e1, -1.0)),
                      (COL_E1, jnp.where(valid, e2, -1.0)),
                      (COL_W0, w1), (COL_W1, w2), (COL_R0, r1), (COL_R1, r2)):
        slab = jnp.where(col == cidx, val, slab)
    route_ref[...] = slab


def _mixer(lp, attn, u, ga, gp, lead, x2, pmix, pscale, wa, wp, wo, g2, wr, br):
    rows = attn.shape[0]
    d = x2.shape[1]
    tm = ROW_TILE
    nb = lp // BLOCK
    halo_blocks = tm // POOL_HALO
    row = lambda i: (i, 0)
    return pl.pallas_call(
        functools.partial(_mixer_kernel, lp),
        grid=(rows // tm,),
        in_specs=[pl.BlockSpec((tm, attn.shape[1]), row),
                  pl.BlockSpec((tm, u.shape[1]), row),
                  pl.BlockSpec((POOL_HALO, u.shape[1]),
                               lambda i: (jnp.maximum(i * halo_blocks - 1, 0), 0)),
                  pl.BlockSpec((tm, d), row),
                  pl.BlockSpec((tm, d), row),
                  _resident(lead.shape)] + _tile_block_specs(nb, nb - 1, d) + [
                  _resident(pmix.shape), _resident(pscale.shape), _resident(wa.shape),
                  _resident(wp.shape), _resident(wo.shape), _resident(g2.shape),
                  _resident(wr.shape), _resident(br.shape)],
        out_specs=[pl.BlockSpec((tm, d), row),
                   pl.BlockSpec((tm, d // 2), row),
                   pl.BlockSpec((tm, LANES), row),
                   pl.BlockSpec((1, LANES), lambda i: (0, 0))],
        out_shape=[jax.ShapeDtypeStruct((rows, d), F32),
                   jax.ShapeDtypeStruct((rows, d // 2), U32),
                   jax.ShapeDtypeStruct((rows, LANES), F32),
                   jax.ShapeDtypeStruct((1, LANES), F32)],
        scratch_shapes=[pltpu.VMEM((1, LANES), F32)],
        compiler_params=_params(("arbitrary",)),
        name="mixer",
    )(attn, u, u, ga, gp, lead, x2, x2, pmix, pscale, wa, wp, wo, g2, wr, br)


def _scatter_kernel(pos_ref, x_ref, init_ref, xs_ref, sem):
    del init_ref
    i = pl.program_id(0)
    tm = x_ref.shape[0]

    def start(it, carry):
        r0 = pl.multiple_of(it * DMA_UNROLL, DMA_UNROLL)
        for j in range(DMA_UNROLL):
            for k in range(2):
                p = pos_ref[(i * tm + r0 + j) * 2 + k]
                pltpu.make_async_copy(x_ref.at[pl.ds(r0 + j, 1), :], xs_ref.at[pl.ds(p, 1), :],
                                      sem).start(priority=k)
        return carry

    lax.fori_loop(0, tm // DMA_UNROLL, start, 0)
    for k in range(2):
        pltpu.make_async_copy(x_ref, xs_ref.at[pl.ds(0, tm), :], sem).wait()


def _scatter(pos, xp, slots):
    rows, half = xp.shape
    tm = ROW_TILE
    return pl.pallas_call(
        _scatter_kernel,
        grid_spec=pltpu.PrefetchScalarGridSpec(
            num_scalar_prefetch=1,
            grid=(rows // tm,),
            in_specs=[pl.BlockSpec((tm, half), lambda i, pos: (i, 0)),
                      pl.BlockSpec(memory_space=pl.ANY)],
            out_specs=pl.BlockSpec(memory_space=pl.ANY),
            scratch_shapes=[pltpu.SemaphoreType.DMA(())],
        ),
        out_shape=jax.ShapeDtypeStruct((slots + 2 * tm, half), U32),
        input_output_aliases={2: 0},
        compiler_params=_params(("arbitrary",)),
        name="scatter",
    )(pos, xp, jnp.zeros((slots + 2 * tm, half), U32))


def _expert_kernel(be_ref, nrow_ref, last_ref, ord_ref, next_ref, xs_ref, wg_hbm, wu_hbm, wd_hbm, ys_ref,
                   wg_f, wu_f, wd_f, wg_s, wu_s, wd_s, sem):
    b = pl.program_id(0)
    n_valid_rows = nrow_ref[b]
    expert = be_ref[b]
    slot = ord_ref[b] % 2

    def weight_copies(e, s):
        return [pltpu.make_async_copy(hbm.at[e], buf.at[s], sem.at[s, n])
                for n, (hbm, buf) in enumerate(((wg_hbm, wg_f), (wu_hbm, wu_f), (wd_hbm, wd_f)))]

    @pl.when(b == 0)
    def _():
        for c in weight_copies(expert, slot):
            c.start()

    @pl.when((b == 0) | (expert != be_ref[jnp.maximum(b - 1, 0)]))
    def _():
        nxt = next_ref[b]

        @pl.when(nxt >= 0)
        def _():
            for c in weight_copies(nxt, 1 - slot):
                c.start()

        for c in weight_copies(expert, slot):
            c.wait()
        wg_s[...] = wg_f[slot].astype(BF16)
        wu_s[...] = wu_f[slot].astype(BF16)
        wd_s[...] = wd_f[slot].astype(BF16)

    @pl.when(n_valid_rows == 0)
    def _():
        ys_ref[...] = jnp.zeros_like(ys_ref)

    @pl.when(n_valid_rows > 0)
    def _():
        packed = xs_ref[...]
        lo = lax.bitcast_convert_type(packed << 16, F32)
        hi = lax.bitcast_convert_type(packed & jnp.uint32(0xFFFF0000), F32)
        x = jnp.concatenate([lo, hi], axis=1).astype(BF16)
        gate = jnp.dot(x, wg_s[...], preferred_element_type=F32)
        up = jnp.dot(x, wu_s[...], preferred_element_type=F32)
        hb = (jax.nn.silu(gate) * up).astype(BF16)
        ys_ref[...] = jnp.dot(hb, wd_s[...], preferred_element_type=F32)


def _experts(block_expert, block_rows, last_block, block_ord, block_next, xs, w_gate, w_up, w_down):
    half = xs.shape[1]
    n_exp, d, de = w_gate.shape
    bm = EXPERT_ROWS
    n_blocks = block_expert.shape[0]
    slots = n_blocks * bm
    any_space = pl.BlockSpec(memory_space=pl.ANY)
    return pl.pallas_call(
        _expert_kernel,
        grid_spec=pltpu.PrefetchScalarGridSpec(
            num_scalar_prefetch=5,
            grid=(n_blocks,),
            in_specs=[pl.BlockSpec((bm, half), lambda b, be, nr, last, *_: (jnp.minimum(b, last[0]), 0)),
                      any_space, any_space, any_space],
            out_specs=pl.BlockSpec((bm, d), lambda b, *_: (b, 0)),
            scratch_shapes=[pltpu.VMEM((2, d, de), F32), pltpu.VMEM((2, d, de), F32),
                            pltpu.VMEM((2, de, d), F32),
                            pltpu.VMEM((d, de), BF16), pltpu.VMEM((d, de), BF16),
                            pltpu.VMEM((de, d), BF16),
                            pltpu.SemaphoreType.DMA((2, 3))],
        ),
        out_shape=jax.ShapeDtypeStruct((slots, d), F32),
        compiler_params=_params(("arbitrary",)),
        name="experts",
    )(block_expert, block_rows, last_block, block_ord, block_next, xs, w_gate, w_up, w_down)


def _combine_kernel(nb, per_batch, pos_ref, h2_ref, route_ref, g_ref, ys_ref, o_ref, y_ref, sem):
    t = pl.program_id(0)
    tm = h2_ref.shape[0]

    def issue(tile, slot):
        base = ((tile // per_batch) * nb + 1 + tile % per_batch) * tm

        def body(it, carry):
            r0 = pl.multiple_of(it * DMA_UNROLL, DMA_UNROLL)
            for j in range(DMA_UNROLL):
                for k in range(2):
                    p = pos_ref[(base + r0 + j) * 2 + k]
                    pltpu.make_async_copy(ys_ref.at[pl.ds(p, 1), :], y_ref.at[slot, k, pl.ds(r0 + j, 1), :],
                                          sem.at[slot]).start(priority=k)
            return carry

        lax.fori_loop(0, tm // DMA_UNROLL, body, 0)

    @pl.when(t == 0)
    def _():
        issue(0, 0)

    @pl.when(t + 1 < pl.num_programs(0))
    def _():
        issue(t + 1, (t + 1) % 2)

    slot = t % 2
    for k in range(2):
        pltpu.make_async_copy(ys_ref.at[pl.ds(0, tm), :], y_ref.at[slot, k], sem.at[slot]).wait()
    route = route_ref[...]
    w0 = route[:, COL_W0:COL_W0 + 1]
    w1 = route[:, COL_W1:COL_W1 + 1]
    h = h2_ref[...] + (w0 * y_ref[slot, 0] + w1 * y_ref[slot, 1])
    ms = jnp.mean(h * h, axis=-1, keepdims=True)
    o_ref[0] = h * lax.rsqrt(ms + RMS_EPS) * g_ref[...]


def _combine(pos, h2, route, g, ys, batch, nb, seq):
    rows, d = h2.shape
    tm = BLOCK
    per_batch = seq // tm
    tile = lambda t, pos: ((t // per_batch) * nb + 1 + t % per_batch, 0)
    return pl.pallas_call(
        functools.partial(_combine_kernel, nb, per_batch),
        grid_spec=pltpu.PrefetchScalarGridSpec(
            num_scalar_prefetch=1,
            grid=(batch * per_batch,),
            in_specs=[pl.BlockSpec((tm, d), tile),
                      pl.BlockSpec((tm, LANES), tile),
                      pl.BlockSpec((1, d), lambda t, pos: (0, 0)),
                      pl.BlockSpec(memory_space=pl.ANY)],
            out_specs=pl.BlockSpec((1, tm, d), lambda t, pos: (t // per_batch, t % per_batch, 0)),
            scratch_shapes=[pltpu.VMEM((2, 2, tm, d), F32), pltpu.SemaphoreType.DMA((2,))],
        ),
        out_shape=jax.ShapeDtypeStruct((batch, seq, d), F32),
        compiler_params=_params(("arbitrary",)),
        name="combine",
    )(pos, h2, route, g, ys)


def kernel(x, meta_tokens, rel_bias, norm_mix, w_in, attn_sinks, pool_mix, pool_scale,
           w_attn_branch, w_pool_branch, w_out, norm_ffn, w_router_group, b_router_group,
           w_router_expert, b_router_expert, w_gate, w_up, w_down, norm_final):
    batch, seq, d = x.shape
    depth = w_in.shape[0]
    assert depth == 1, "single-layer stack"
    aw = w_attn_branch.shape[1]
    pw = w_pool_branch.shape[1]
    kvw = 2 * N_KV_HEADS * HEAD_DIM
    assert aw == N_Q_HEADS * HEAD_DIM and w_in.shape[2] == aw + kvw + pw + 2 * d
    assert seq % BLOCK == 0
    lp = seq + BLOCK
    nb = lp // BLOCK
    rows = batch * lp
    assert rows % ROW_TILE == 0

    assert ROW_TILE == 2 * BLOCK
    lead = jnp.concatenate([jnp.zeros((PAD, d), x.dtype), meta_tokens.astype(x.dtype)], axis=0)
    x2 = x.reshape(batch * seq, d)

    q, kv, u, ga, gp = _inproj(lead, x2, nb, norm_mix[0][None], w_in[0].astype(BF16), (aw, kvw, pw, d, d))

    attn = _attention(q, kv, _attn_bias_tables(rel_bias), attn_sinks[0].astype(F32), batch, nb)

    n_router = N_GROUPS + N_EXPERTS
    wr = jnp.concatenate([w_router_group[0], w_router_expert[0], jnp.zeros((d, LANES - n_router), F32)], axis=1)
    br = jnp.concatenate([b_router_group[0], b_router_expert[0], jnp.zeros((LANES - n_router,), F32)])[None]
    wr_hi = wr.astype(BF16)
    wr_split = jnp.concatenate([wr_hi, (wr - wr_hi.astype(F32)).astype(BF16)], axis=1)
    h2, xp, route, counts = _mixer(
        lp, attn, u, ga, gp, lead, x2, pool_mix[0].astype(BF16), pool_scale[0][None].astype(F32),
        w_attn_branch[0].astype(BF16), w_pool_branch[0].astype(BF16), w_out[0].astype(BF16),
        norm_ffn[0][None], wr_split, br)

    bm = EXPERT_ROWS
    n_tok = batch * (seq + N_META)
    n_blocks = (2 * n_tok) // bm + N_EXPERTS
    cnt = counts[0, :N_EXPERTS].astype(I32)
    blocks_e = (cnt + bm - 1) // bm
    bend = jnp.cumsum(blocks_e)
    bstart = bend - blocks_e
    ord_e = jnp.cumsum((blocks_e > 0).astype(I32)) - 1
    n_used = bend[-1]
    last_block = jnp.maximum(n_used - 1, 0)
    blk = jnp.arange(n_blocks, dtype=I32)

    def expert_of(block):
        return jnp.minimum(jnp.sum((bend[None, :] <= block[:, None]).astype(I32), axis=1), N_EXPERTS - 1)

    block_expert = expert_of(jnp.minimum(blk, last_block))
    own = block_expert[:, None] == jnp.arange(N_EXPERTS, dtype=I32)[None, :]
    pick = lambda v: jnp.sum(jnp.where(own, v[None, :], 0), axis=1)
    block_rows = jnp.clip(pick(cnt) - (blk - pick(bstart)) * bm, 0, bm)
    block_rows = jnp.where(blk < n_used, block_rows, 0).astype(I32)
    block_ord = pick(ord_e).astype(I32)
    next_first = pick(bend)
    block_next = jnp.where(next_first < n_used, expert_of(next_first), -1).astype(I32)
    e = route[:, COL_E0:COL_E1 + 1].astype(I32)
    rank = route[:, COL_R0:COL_R1 + 1].astype(I32)
    first_row = jnp.sum(jnp.where(e[..., None] == jnp.arange(N_EXPERTS, dtype=I32), bstart * bm, 0), axis=-1)
    spare = n_blocks * bm + jnp.arange(2 * rows, dtype=I32).reshape(rows, 2) % (2 * ROW_TILE)
    pos = jnp.where(e >= 0, first_row + rank, spare).astype(I32).reshape(-1)

    xs = _scatter(pos, xp, n_blocks * bm)
    ys = _experts(block_expert.astype(I32), block_rows, last_block.astype(I32).reshape(1), block_ord, block_next,
                  xs, w_gate[0], w_up[0], w_down[0])
    return _combine(pos, h2, route, norm_final[None].astype(F32), ys, batch, nb, seq)
```

```python
import functools
import math

import numpy as np
import jax
import jax.numpy as jnp
from jax import lax
from jax.experimental import pallas as pl
from jax.experimental.pallas import tpu as pltpu

F32 = jnp.float32
BF16 = jnp.bfloat16
I32 = jnp.int32
U32 = jnp.uint32

BLOCK = 128
N_META = 16
PAD = BLOCK - N_META
HEAD_DIM = 64
N_KV_HEADS = 2
Q_PER_KV = 8
N_Q_HEADS = N_KV_HEADS * Q_PER_KV
WINDOW = 128
POOL_WINDOWS = (2, 4, 8, 16)
POOL_HALO = 16
N_BUCKETS = 32
MAX_DISTANCE = 128
N_GROUPS = 8
EXPERTS_PER_GROUP = 8
N_EXPERTS = N_GROUPS * EXPERTS_PER_GROUP
RMS_EPS = 1e-6
LANES = 128
VMEM_LIMIT = 56 * 1024 * 1024

ROW_TILE = 256
EXPERT_ROWS = 256
DMA_UNROLL = 8
SPARE_ROWS = 4 * ROW_TILE
COL_E0, COL_E1, COL_W0, COL_W1, COL_R0, COL_R1 = 0, 1, 2, 3, 4, 5


def _params(sem, vmem=VMEM_LIMIT):
    return pltpu.CompilerParams(dimension_semantics=sem, vmem_limit_bytes=vmem)


def _resident(shape):
    nd = len(shape)
    return pl.BlockSpec(shape, lambda *_: (0,) * nd, pipeline_mode=pl.Buffered(1))


def _tile_block_specs(nb, per_batch, d):
    def spec(half):
        def index(i, *_):
            g = 2 * i + half
            return ((g // nb) * per_batch + jnp.maximum(g % nb - 1, 0), 0)
        return pl.BlockSpec((BLOCK, d), index)
    return [spec(0), spec(1)]


def _tile_rows(nb, lead_ref, xa_ref, xb_ref):
    i = pl.program_id(0)
    halves = [jnp.where((2 * i + half) % nb == 0, lead_ref[...], ref[...])
              for half, ref in enumerate((xa_ref, xb_ref))]
    return jnp.concatenate(halves, axis=0)


def _inproj_kernel(nb, lead_ref, xa_ref, xb_ref, g_ref, w_ref, q_ref, kv_ref, u_ref, ga_ref, gp_ref):
    x = _tile_rows(nb, lead_ref, xa_ref, xb_ref)
    ms = jnp.mean(x * x, axis=-1, keepdims=True)
    hn = (x * lax.rsqrt(ms + RMS_EPS) * g_ref[...]).astype(BF16)
    off = 0
    for ref in (q_ref, kv_ref, u_ref, ga_ref, gp_ref):
        width = ref.shape[1]
        for c in range(0, width, 1024):
            cw = min(1024, width - c)
            ref[:, c:c + cw] = jnp.dot(hn, w_ref[:, off + c:off + c + cw],
                                       preferred_element_type=F32).astype(BF16)
        off += width


def _inproj(lead, x2, nb, g, w, widths):
    d = x2.shape[1]
    tm = ROW_TILE
    per_batch = nb - 1
    rows = x2.shape[0] // per_batch * nb
    outs = [jax.ShapeDtypeStruct((rows, wd), BF16) for wd in widths]
    return pl.pallas_call(
        functools.partial(_inproj_kernel, nb),
        grid=(rows // tm,),
        in_specs=[_resident(lead.shape)] + _tile_block_specs(nb, per_batch, d)
                 + [_resident((1, d)), _resident(w.shape)],
        out_specs=[pl.BlockSpec((tm, wd), lambda i: (i, 0)) for wd in widths],
        out_shape=outs,
        compiler_params=_params(("parallel",)),
        name="inproj",
    )(lead, x2, x2, g, w)


PAIR = 2 * HEAD_DIM
KEYS = 3 * BLOCK
N_PAIRS = N_Q_HEADS // 2


def _attn_kernel(sink_ref, q_ref, kvp_ref, kvc_ref, kvm_ref, bias_ref, o_ref):
    kw = N_KV_HEADS * HEAD_DIM
    zeros = jnp.zeros((KEYS, HEAD_DIM), BF16)
    ones = jnp.ones((KEYS, HEAD_DIM), BF16)
    lane = lax.broadcasted_iota(I32, (BLOCK, PAIR), 1)
    scale = jnp.asarray(HEAD_DIM ** -0.5, BF16)
    for hk in range(N_KV_HEADS):
        ks = slice(hk * HEAD_DIM, (hk + 1) * HEAD_DIM)
        vs = slice(kw + hk * HEAD_DIM, kw + (hk + 1) * HEAD_DIM)
        k3 = jnp.concatenate([kvp_ref[:, ks], kvc_ref[:, ks], kvm_ref[:, ks]], axis=0) * scale
        v3 = jnp.concatenate([kvp_ref[:, vs], kvc_ref[:, vs], kvm_ref[:, vs]], axis=0)
        kbd = jnp.concatenate([jnp.concatenate([k3, zeros], axis=1),
                               jnp.concatenate([zeros, k3], axis=1)], axis=0)
        vext = jnp.concatenate([jnp.concatenate([v3, zeros, ones, zeros], axis=1),
                                jnp.concatenate([zeros, v3, zeros, ones], axis=1)], axis=0)
        for jp in range(Q_PER_KV // 2):
            j = hk * (Q_PER_KV // 2) + jp
            qp = q_ref[:, j * PAIR:(j + 1) * PAIR]
            s = lax.dot_general(qp, kbd, (((1,), (1,)), ((), ())),
                                preferred_element_type=F32) + bias_ref[0, j]
            probs, sink_terms = [], []
            for side in range(2):
                ss = s[:, side * KEYS:(side + 1) * KEYS]
                sink = sink_ref[2 * j + side]
                mx = jnp.maximum(jnp.max(ss, axis=-1, keepdims=True), sink)
                probs.append(jnp.exp(ss - mx))
                sink_terms.append(jnp.exp(sink - mx))
            p = jnp.concatenate(probs, axis=1).astype(BF16)
            r = jnp.dot(p, vext, preferred_element_type=F32)
            den = r[:, PAIR:] + jnp.where(lane < HEAD_DIM, sink_terms[0], sink_terms[1])
            o_ref[:, j * PAIR:(j + 1) * PAIR] = (r[:, :PAIR] / den).astype(o_ref.dtype)


def _attention(q, kv, bias, sinks, batch, nb):
    rows, aw = q.shape
    kvw = kv.shape[1]
    return pl.pallas_call(
        _attn_kernel,
        grid=(batch, nb),
        in_specs=[pl.BlockSpec(memory_space=pltpu.SMEM),
                  pl.BlockSpec((BLOCK, aw), lambda b, n: (b * nb + n, 0)),
                  pl.BlockSpec((BLOCK, kvw), lambda b, n: (b * nb + jnp.maximum(n - 1, 0), 0)),
                  pl.BlockSpec((BLOCK, kvw), lambda b, n: (b * nb + n, 0)),
                  pl.BlockSpec((BLOCK, kvw), lambda b, n: (b * nb, 0)),
                  pl.BlockSpec((1,) + bias.shape[1:], lambda b, n: (jnp.minimum(n, 2), 0, 0, 0))],
        out_specs=pl.BlockSpec((BLOCK, aw), lambda b, n: (b * nb + n, 0)),
        out_shape=jax.ShapeDtypeStruct((rows, aw), BF16),
        compiler_params=_params(("parallel", "arbitrary")),
        name="attention",
    )(sinks, q, kv, kv, kv, bias)


def _attn_bias_tables(rel_bias):
    max_exact = N_BUCKETS // 2
    qi = np.arange(BLOCK)[:, None]
    kj = np.arange(2 * BLOCK)[None, :]
    mj = np.arange(BLOCK)[None, :]
    dist = qi + BLOCK - kj
    d = np.maximum(dist, 0)
    large = max_exact + (np.log(np.maximum(d, 1).astype(np.float32) / max_exact)
                         / math.log(MAX_DISTANCE / max_exact) * (N_BUCKETS - max_exact)).astype(np.int32)
    bucket = np.where(d < max_exact, d, np.minimum(large, N_BUCKETS - 1))
    onehot = (bucket[..., None] == np.arange(N_BUCKETS)).astype(np.float32)
    rb = rel_bias.astype(F32)
    win = jnp.einsum("qkb,bh->hqk", onehot, rb, precision=lax.Precision.HIGHEST)
    meta = jnp.broadcast_to(rb[N_BUCKETS - 1][:, None, None], (N_Q_HEADS, BLOCK, BLOCK))
    full = jnp.concatenate([win, meta], axis=2)
    masks = []
    for n in range(3):
        win_ok = (dist >= 0) & (dist < WINDOW) & ((n - 1) * BLOCK + kj >= PAD)
        meta_ok = (mj >= PAD) & (n * BLOCK + qi - mj >= WINDOW)
        masks.append(np.concatenate([win_ok, meta_ok], axis=1))
    t = jnp.where(np.stack(masks)[:, None], full[None], -jnp.inf)
    t = t.reshape(3, N_PAIRS, 2, BLOCK, KEYS).transpose(0, 1, 3, 2, 4)
    return t.reshape(3, N_PAIRS, BLOCK, 2 * KEYS)


def _mixer_kernel(lp, attn_ref, u_ref, uh_ref, ga_ref, gp_ref, lead_ref, xa_ref, xb_ref,
                  pmix_ref, pscale_ref, wa_ref, wp_ref, wo_ref, g2_ref, wr_ref, br_ref,
                  h2_ref, xp_ref, route_ref, cnt_ref, run_ref):
    i = pl.program_id(0)
    tm = h2_ref.shape[0]

    @pl.when(i == 0)
    def _():
        run_ref[...] = jnp.zeros_like(run_ref)

    t = (i * tm + lax.broadcasted_iota(I32, (tm, 1), 0)) % lp
    valid = t >= PAD
    tx = (i * tm - POOL_HALO + lp + lax.broadcasted_iota(I32, (tm + POOL_HALO, 1), 0)) % lp
    uext = jnp.concatenate([uh_ref[...], u_ref[...]], axis=0).astype(F32)
    ug = jnp.where(tx >= PAD, uext, 0.0)
    n_rows = (t - PAD + 1).astype(F32)
    gw = ug.shape[1] // len(POOL_WINDOWS)
    pooled = []
    for gi, w in enumerate(POOL_WINDOWS):
        c = ug[:, gi * gw:(gi + 1) * gw]
        s, span = c, 1
        while span < w:
            s = s[span:] + s[:-span]
            span *= 2
        win = s[POOL_HALO + 1 - w:POOL_HALO + 1 - w + tm]
        n_valid = jnp.clip(n_rows, 1.0, float(w))
        mixed = jnp.where(valid, win / n_valid - c[POOL_HALO:], 0.0)
        pooled.append(jnp.dot(mixed.astype(BF16), pmix_ref[gi], preferred_element_type=F32))
    pool = jnp.concatenate(pooled, axis=1) * pscale_ref[...]

    a = jnp.dot(attn_ref[...], wa_ref[...], preferred_element_type=F32)
    p = jnp.dot(pool.astype(BF16), wp_ref[...], preferred_element_type=F32)
    merged = (jax.nn.sigmoid(ga_ref[...].astype(F32)) * a
              + jax.nn.sigmoid(gp_ref[...].astype(F32)) * p)
    h2 = (_tile_rows(lp // BLOCK, lead_ref, xa_ref, xb_ref)
          + jnp.dot(merged.astype(BF16), wo_ref[...], preferred_element_type=F32))
    h2_ref[...] = h2

    ms = jnp.mean(h2 * h2, axis=-1, keepdims=True)
    hn2 = h2 * lax.rsqrt(ms + RMS_EPS) * g2_ref[...]
    x_hi = hn2.astype(BF16)
    x_hi32 = x_hi.astype(F32)
    half = hn2.shape[1] // 2
    lo = lax.bitcast_convert_type(x_hi32[:, :half], U32)
    hi = lax.bitcast_convert_type(x_hi32[:, half:], U32)
    xp_ref[...] = (hi & jnp.uint32(0xFFFF0000)) | (lo >> 16)

    x_lo = (hn2 - x_hi32).astype(BF16)
    hi_prod = jnp.dot(x_hi, wr_ref[...], preferred_element_type=F32)
    lo_prod = jnp.dot(x_lo, wr_ref[:, :LANES], preferred_element_type=F32)
    logits = hi_prod[:, :LANES] + (hi_prod[:, LANES:] + lo_prod) + br_ref[...]
    col = lax.broadcasted_iota(I32, logits.shape, 1).astype(F32)
    neg = -jnp.inf
    gl = jnp.where(col < N_GROUPS, logits, neg)
    gmax = jnp.max(gl, axis=-1, keepdims=True)
    grp = jnp.min(jnp.where(gl == gmax, col, float(LANES)), axis=-1, keepdims=True)
    p_grp = 1.0 / jnp.sum(jnp.exp(gl - gmax), axis=-1, keepdims=True)
    e_lo = N_GROUPS + grp * EXPERTS_PER_GROUP
    el = jnp.where((col >= e_lo) & (col < e_lo + EXPERTS_PER_GROUP), logits, neg)
    m1 = jnp.max(el, axis=-1, keepdims=True)
    i1 = jnp.min(jnp.where(el == m1, col, float(LANES)), axis=-1, keepdims=True)
    el2 = jnp.where(col == i1, neg, el)
    m2 = jnp.max(el2, axis=-1, keepdims=True)
    i2 = jnp.min(jnp.where(el2 == m2, col, float(LANES)), axis=-1, keepdims=True)
    z = jnp.exp(m2 - m1)
    w1 = p_grp / (1.0 + z)
    w2 = p_grp * z / (1.0 + z)
    e1 = i1 - N_GROUPS
    e2 = i2 - N_GROUPS

    oh1 = jnp.where((col == e1) & valid, 1.0, 0.0)
    oh2 = jnp.where((col == e2) & valid, 1.0, 0.0)
    lower = (lax.broadcasted_iota(I32, (tm, tm), 0) > lax.broadcasted_iota(I32, (tm, tm), 1))
    lower = jnp.where(lower, 1.0, 0.0).astype(BF16)
    before1 = jnp.dot(lower, oh1.astype(BF16), preferred_element_type=F32)
    before2 = jnp.dot(lower, oh2.astype(BF16), preferred_element_type=F32)
    tot1 = jnp.sum(oh1, axis=0, keepdims=True)
    tot2 = jnp.sum(oh2, axis=0, keepdims=True)
    run = run_ref[...]
    r1 = jnp.sum(oh1 * (run + before1), axis=-1, keepdims=True)
    r2 = jnp.sum(oh2 * (run + tot1 + before2), axis=-1, keepdims=True)
    run = run + tot1 + tot2
    run_ref[...] = run
    cnt_ref[...] = run

    slab = jnp.zeros(logits.shape, F32)
    for cidx, val in ((COL_E0, jnp.where(valid, e1, -1.0)),
                      (COL_E1, jnp.where(valid, e2, -1.0)),
                      (COL_W0, w1), (COL_W1, w2), (COL_R0, r1), (COL_R1, r2)):
        slab = jnp.where(col == cidx, val, slab)
    route_ref[...] = slab


def _mixer(lp, attn, u, ga, gp, lead, x2, pmix, pscale, wa, wp, wo, g2, wr, br):
    rows = attn.shape[0]
    d = x2.shape[1]
    tm = ROW_TILE
    nb = lp // BLOCK
    halo_blocks = tm // POOL_HALO
    row = lambda i: (i, 0)
    return pl.pallas_call(
        functools.partial(_mixer_kernel, lp),
        grid=(rows // tm,),
        in_specs=[pl.BlockSpec((tm, attn.shape[1]), row),
                  pl.BlockSpec((tm, u.shape[1]), row),
                  pl.BlockSpec((POOL_HALO, u.shape[1]),
                               lambda i: (jnp.maximum(i * halo_blocks - 1, 0), 0)),
                  pl.BlockSpec((tm, d), row),
                  pl.BlockSpec((tm, d), row),
                  _resident(lead.shape)] + _tile_block_specs(nb, nb - 1, d) + [
                  _resident(pmix.shape), _resident(pscale.shape), _resident(wa.shape),
                  _resident(wp.shape), _resident(wo.shape), _resident(g2.shape),
                  _resident(wr.shape), _resident(br.shape)],
        out_specs=[pl.BlockSpec((tm, d), row),
                   pl.BlockSpec((tm, d // 2), row),
                   pl.BlockSpec((tm, LANES), row),
                   pl.BlockSpec((1, LANES), lambda i: (0, 0))],
        out_shape=[jax.ShapeDtypeStruct((rows, d), F32),
                   jax.ShapeDtypeStruct((rows, d // 2), U32),
                   jax.ShapeDtypeStruct((rows, LANES), F32),
                   jax.ShapeDtypeStruct((1, LANES), F32)],
        scratch_shapes=[pltpu.VMEM((1, LANES), F32)],
        compiler_params=_params(("arbitrary",)),
        name="mixer",
    )(attn, u, u, ga, gp, lead, x2, x2, pmix, pscale, wa, wp, wo, g2, wr, br)


def _scatter_kernel(pos_ref, x_ref, init_ref, xs_ref, stage_ref, sem):
    del init_ref
    i = pl.program_id(0)
    last = pl.num_programs(0) - 1
    tm = x_ref.shape[0]
    slot = i % 2

    def retire(s):
        for k in range(2):
            pltpu.make_async_copy(stage_ref.at[s], xs_ref.at[pl.ds(0, tm), :], sem.at[s]).wait()

    @pl.when(i >= 2)
    def _():
        retire(slot)

    stage_ref[slot] = x_ref[...]

    def start(it, carry):
        r0 = pl.multiple_of(it * DMA_UNROLL, DMA_UNROLL)
        for j in range(DMA_UNROLL):
            for k in range(2):
                p = pos_ref[(i * tm + r0 + j) * 2 + k]
                pltpu.make_async_copy(stage_ref.at[slot, pl.ds(r0 + j, 1), :], xs_ref.at[pl.ds(p, 1), :],
                                      sem.at[slot]).start(priority=k)
        return carry

    lax.fori_loop(0, tm // DMA_UNROLL, start, 0)

    @pl.when(i == last)
    def _():
        @pl.when(i >= 1)
        def _():
            retire(1 - slot)
        retire(slot)


def _scatter(pos, xp, slots):
    rows, half = xp.shape
    tm = ROW_TILE
    return pl.pallas_call(
        _scatter_kernel,
        grid_spec=pltpu.PrefetchScalarGridSpec(
            num_scalar_prefetch=1,
            grid=(rows // tm,),
            in_specs=[pl.BlockSpec((tm, half), lambda i, pos: (i, 0)),
                      pl.BlockSpec(memory_space=pl.ANY)],
            out_specs=pl.BlockSpec(memory_space=pl.ANY),
            scratch_shapes=[pltpu.VMEM((2, tm, half), U32), pltpu.SemaphoreType.DMA((2,))],
        ),
        out_shape=jax.ShapeDtypeStruct((slots + SPARE_ROWS, half), U32),
        input_output_aliases={2: 0},
        compiler_params=_params(("arbitrary",)),
        name="scatter",
    )(pos, xp, jnp.zeros((slots + SPARE_ROWS, half), U32))


def _pack_halves(x):
    n = x.shape[1] // 2
    r = x.astype(BF16).astype(F32)
    lo = lax.bitcast_convert_type(r[:, :n], U32)
    hi = lax.bitcast_convert_type(r[:, n:], U32)
    return (hi & jnp.uint32(0xFFFF0000)) | (lo >> 16)


def _unpack_halves(p):
    return (lax.bitcast_convert_type(p << 16, F32),
            lax.bitcast_convert_type(p & jnp.uint32(0xFFFF0000), F32))


def _expert_kernel(be_ref, nrow_ref, last_ref, ord_ref, next_ref, xs_ref, wg_hbm, wu_hbm, wd_hbm, ys_ref,
                   wg_f, wu_f, wd_f, sem):
    b = pl.program_id(0)
    n_valid_rows = nrow_ref[b]
    expert = be_ref[b]
    slot = ord_ref[b] % 2

    def weight_copies(e, s):
        return [pltpu.make_async_copy(hbm.at[e], buf.at[s], sem.at[s, n])
                for n, (hbm, buf) in enumerate(((wg_hbm, wg_f), (wu_hbm, wu_f), (wd_hbm, wd_f)))]

    @pl.when(b == 0)
    def _():
        for c in weight_copies(expert, slot):
            c.start()

    @pl.when((b == 0) | (expert != be_ref[jnp.maximum(b - 1, 0)]))
    def _():
        nxt = next_ref[b]

        @pl.when(nxt >= 0)
        def _():
            for c in weight_copies(nxt, 1 - slot):
                c.start()

        for c in weight_copies(expert, slot):
            c.wait()

    @pl.when(n_valid_rows == 0)
    def _():
        ys_ref[...] = jnp.zeros_like(ys_ref)

    @pl.when(n_valid_rows > 0)
    def _():
        lo, hi = _unpack_halves(xs_ref[...])
        x = jnp.concatenate([lo, hi], axis=1).astype(BF16)
        gate = jnp.dot(x, wg_f[slot].astype(BF16), preferred_element_type=F32)
        up = jnp.dot(x, wu_f[slot].astype(BF16), preferred_element_type=F32)
        hb = (jax.nn.silu(gate) * up).astype(BF16)
        ys_ref[...] = _pack_halves(jnp.dot(hb, wd_f[slot].astype(BF16), preferred_element_type=F32))


def _experts(block_expert, block_rows, last_block, block_ord, block_next, xs, w_gate, w_up, w_down):
    half = xs.shape[1]
    n_exp, d, de = w_gate.shape
    bm = EXPERT_ROWS
    n_blocks = block_expert.shape[0]
    slots = n_blocks * bm
    any_space = pl.BlockSpec(memory_space=pl.ANY)
    return pl.pallas_call(
        _expert_kernel,
        grid_spec=pltpu.PrefetchScalarGridSpec(
            num_scalar_prefetch=5,
            grid=(n_blocks,),
            in_specs=[pl.BlockSpec((bm, half), lambda b, be, nr, last, *_: (jnp.minimum(b, last[0]), 0)),
                      any_space, any_space, any_space],
            out_specs=pl.BlockSpec((bm, half), lambda b, *_: (b, 0)),
            scratch_shapes=[pltpu.VMEM((2, d, de), F32), pltpu.VMEM((2, d, de), F32),
                            pltpu.VMEM((2, de, d), F32),
                            pltpu.SemaphoreType.DMA((2, 3))],
        ),
        out_shape=jax.ShapeDtypeStruct((slots, half), U32),
        compiler_params=_params(("arbitrary",)),
        name="experts",
    )(block_expert, block_rows, last_block, block_ord, block_next, xs, w_gate, w_up, w_down)


def _combine_kernel(nb, per_batch, pos_ref, h2_ref, route_ref, g_ref, ys_ref, o_ref, y_ref, sem):
    t = pl.program_id(0)
    tm = h2_ref.shape[0]

    def issue(tile, slot):
        base = ((tile // per_batch) * nb + 1 + tile % per_batch) * tm

        def body(it, carry):
            r0 = pl.multiple_of(it * DMA_UNROLL, DMA_UNROLL)
            for j in range(DMA_UNROLL):
                for k in range(2):
                    p = pos_ref[(base + r0 + j) * 2 + k]
                    pltpu.make_async_copy(ys_ref.at[pl.ds(p, 1), :], y_ref.at[slot, k, pl.ds(r0 + j, 1), :],
                                          sem.at[slot]).start(priority=k)
            return carry

        lax.fori_loop(0, tm // DMA_UNROLL, body, 0)

    @pl.when(t == 0)
    def _():
        issue(0, 0)

    @pl.when(t + 1 < pl.num_programs(0))
    def _():
        issue(t + 1, (t + 1) % 2)

    slot = t % 2
    for k in range(2):
        pltpu.make_async_copy(ys_ref.at[pl.ds(0, tm), :], y_ref.at[slot, k], sem.at[slot]).wait()
    route = route_ref[...]
    w0 = route[:, COL_W0:COL_W0 + 1]
    w1 = route[:, COL_W1:COL_W1 + 1]
    half = y_ref.shape[-1]
    y0 = _unpack_halves(y_ref[slot, 0])
    y1 = _unpack_halves(y_ref[slot, 1])
    hs = [h2_ref[:, s * half:(s + 1) * half] + (w0 * y0[s] + w1 * y1[s]) for s in range(2)]
    ms = sum(jnp.sum(h * h, axis=-1, keepdims=True) for h in hs) / (2 * half)
    inv = lax.rsqrt(ms + RMS_EPS)
    for s in range(2):
        o_ref[0, :, s * half:(s + 1) * half] = hs[s] * inv * g_ref[:, s * half:(s + 1) * half]


def _combine(pos, h2, route, g, ys, batch, nb, seq):
    rows, d = h2.shape
    tm = BLOCK
    per_batch = seq // tm
    tile = lambda t, pos: ((t // per_batch) * nb + 1 + t % per_batch, 0)
    return pl.pallas_call(
        functools.partial(_combine_kernel, nb, per_batch),
        grid_spec=pltpu.PrefetchScalarGridSpec(
            num_scalar_prefetch=1,
            grid=(batch * per_batch,),
            in_specs=[pl.BlockSpec((tm, d), tile),
                      pl.BlockSpec((tm, LANES), tile),
                      pl.BlockSpec((1, d), lambda t, pos: (0, 0)),
                      pl.BlockSpec(memory_space=pl.ANY)],
            out_specs=pl.BlockSpec((1, tm, d), lambda t, pos: (t // per_batch, t % per_batch, 0)),
            scratch_shapes=[pltpu.VMEM((2, 2, tm, ys.shape[1]), U32), pltpu.SemaphoreType.DMA((2,))],
        ),
        out_shape=jax.ShapeDtypeStruct((batch, seq, d), F32),
        compiler_params=_params(("arbitrary",)),
        name="combine",
    )(pos, h2, route, g, ys)


def kernel(x, meta_tokens, rel_bias, norm_mix, w_in, attn_sinks, pool_mix, pool_scale,
           w_attn_branch, w_pool_branch, w_out, norm_ffn, w_router_group, b_router_group,
           w_router_expert, b_router_expert, w_gate, w_up, w_down, norm_final):
    batch, seq, d = x.shape
    depth = w_in.shape[0]
    assert depth == 1, "single-layer stack"
    aw = w_attn_branch.shape[1]
    pw = w_pool_branch.shape[1]
    kvw = 2 * N_KV_HEADS * HEAD_DIM
    assert aw == N_Q_HEADS * HEAD_DIM and w_in.shape[2] == aw + kvw + pw + 2 * d
    assert seq % BLOCK == 0
    lp = seq + BLOCK
    nb = lp // BLOCK
    rows = batch * lp
    assert rows % ROW_TILE == 0

    assert ROW_TILE == 2 * BLOCK
    lead = jnp.concatenate([jnp.zeros((PAD, d), x.dtype), meta_tokens.astype(x.dtype)], axis=0)
    x2 = x.reshape(batch * seq, d)

    q, kv, u, ga, gp = _inproj(lead, x2, nb, norm_mix[0][None], w_in[0].astype(BF16), (aw, kvw, pw, d, d))

    attn = _attention(q, kv, _attn_bias_tables(rel_bias), attn_sinks[0].astype(F32), batch, nb)

    n_router = N_GROUPS + N_EXPERTS
    wr = jnp.concatenate([w_router_group[0], w_router_expert[0], jnp.zeros((d, LANES - n_router), F32)], axis=1)
    br = jnp.concatenate([b_router_group[0], b_router_expert[0], jnp.zeros((LANES - n_router,), F32)])[None]
    wr_hi = wr.astype(BF16)
    wr_split = jnp.concatenate([wr_hi, (wr - wr_hi.astype(F32)).astype(BF16)], axis=1)
    h2, xp, route, counts = _mixer(
        lp, attn, u, ga, gp, lead, x2, pool_mix[0].astype(BF16), pool_scale[0][None].astype(F32),
        w_attn_branch[0].astype(BF16), w_pool_branch[0].astype(BF16), w_out[0].astype(BF16),
        norm_ffn[0][None], wr_split, br)

    bm = EXPERT_ROWS
    n_tok = batch * (seq + N_META)
    n_blocks = (2 * n_tok) // bm + N_EXPERTS
    cnt = counts[0, :N_EXPERTS].astype(I32)
    blocks_e = (cnt + bm - 1) // bm
    bend = jnp.cumsum(blocks_e)
    bstart = bend - blocks_e
    ord_e = jnp.cumsum((blocks_e > 0).astype(I32)) - 1
    n_used = bend[-1]
    last_block = jnp.maximum(n_used - 1, 0)
    blk = jnp.arange(n_blocks, dtype=I32)

    def expert_of(block):
        return jnp.minimum(jnp.sum((bend[None, :] <= block[:, None]).astype(I32), axis=1), N_EXPERTS - 1)

    block_expert = expert_of(jnp.minimum(blk, last_block))
    own = block_expert[:, None] == jnp.arange(N_EXPERTS, dtype=I32)[None, :]
    pick = lambda v: jnp.sum(jnp.where(own, v[None, :], 0), axis=1)
    block_rows = jnp.clip(pick(cnt) - (blk - pick(bstart)) * bm, 0, bm)
    block_rows = jnp.where(blk < n_used, block_rows, 0).astype(I32)
    block_ord = pick(ord_e).astype(I32)
    next_first = pick(bend)
    block_next = jnp.where(next_first < n_used, expert_of(next_first), -1).astype(I32)
    e = route[:, COL_E0:COL_E1 + 1].astype(I32)
    rank = route[:, COL_R0:COL_R1 + 1].astype(I32)
    first_row = jnp.sum(jnp.where(e[..., None] == jnp.arange(N_EXPERTS, dtype=I32), bstart * bm, 0), axis=-1)
    spare = n_blocks * bm + jnp.arange(2 * rows, dtype=I32).reshape(rows, 2) % SPARE_ROWS
    pos = jnp.where(e >= 0, first_row + rank, spare).astype(I32).reshape(-1)

    xs = _scatter(pos, xp, n_blocks * bm)
    ys = _experts(block_expert.astype(I32), block_rows, last_block.astype(I32).reshape(1), block_ord, block_next,
                  xs, w_gate[0], w_up[0], w_down[0])
    return _combine(pos, h2, route, norm_final[None].astype(F32), ys, batch, nb, seq)
```

```python
import functools
import math

import numpy as np
import jax
import jax.numpy as jnp
from jax import lax
from jax.experimental import pallas as pl
from jax.experimental.pallas import tpu as pltpu

F32 = jnp.float32
BF16 = jnp.bfloat16
I32 = jnp.int32
U32 = jnp.uint32

BLOCK = 128
N_META = 16
PAD = BLOCK - N_META
HEAD_DIM = 64
N_KV_HEADS = 2
Q_PER_KV = 8
N_Q_HEADS = N_KV_HEADS * Q_PER_KV
WINDOW = 128
POOL_WINDOWS = (2, 4, 8, 16)
POOL_HALO = 16
N_BUCKETS = 32
MAX_DISTANCE = 128
N_GROUPS = 8
EXPERTS_PER_GROUP = 8
N_EXPERTS = N_GROUPS * EXPERTS_PER_GROUP
RMS_EPS = 1e-6
LANES = 128
VMEM_LIMIT = 56 * 1024 * 1024

ROW_TILE = 256
EXPERT_ROWS = 256
DMA_UNROLL = 8
WEIGHT_LOOKAHEAD = 2
SPARE_ROWS = 4 * ROW_TILE
COL_E0, COL_E1, COL_W0, COL_W1, COL_R0, COL_R1 = 0, 1, 2, 3, 4, 5


def _params(sem, vmem=VMEM_LIMIT):
    return pltpu.CompilerParams(dimension_semantics=sem, vmem_limit_bytes=vmem)


def _resident(shape):
    nd = len(shape)
    return pl.BlockSpec(shape, lambda *_: (0,) * nd, pipeline_mode=pl.Buffered(1))


def _tile_block_specs(nb, per_batch, d):
    def spec(half):
        def index(i, *_):
            g = 2 * i + half
            return ((g // nb) * per_batch + jnp.maximum(g % nb - 1, 0), 0)
        return pl.BlockSpec((BLOCK, d), index)
    return [spec(0), spec(1)]


def _tile_rows(nb, lead_ref, xa_ref, xb_ref):
    i = pl.program_id(0)
    halves = [jnp.where((2 * i + half) % nb == 0, lead_ref[...], ref[...])
              for half, ref in enumerate((xa_ref, xb_ref))]
    return jnp.concatenate(halves, axis=0)


def _inproj_kernel(nb, lead_ref, xa_ref, xb_ref, g_ref, w_ref, q_ref, kv_ref, u_ref, ga_ref, gp_ref):
    x = _tile_rows(nb, lead_ref, xa_ref, xb_ref)
    ms = jnp.mean(x * x, axis=-1, keepdims=True)
    hn = (x * lax.rsqrt(ms + RMS_EPS) * g_ref[...]).astype(BF16)
    off = 0
    for ref in (q_ref, kv_ref, u_ref, ga_ref, gp_ref):
        width = ref.shape[1]
        for c in range(0, width, 1024):
            cw = min(1024, width - c)
            ref[:, c:c + cw] = jnp.dot(hn, w_ref[:, off + c:off + c + cw],
                                       preferred_element_type=F32).astype(BF16)
        off += width


def _inproj(lead, x2, nb, g, w, widths):
    d = x2.shape[1]
    tm = ROW_TILE
    per_batch = nb - 1
    rows = x2.shape[0] // per_batch * nb
    outs = [jax.ShapeDtypeStruct((rows, wd), BF16) for wd in widths]
    return pl.pallas_call(
        functools.partial(_inproj_kernel, nb),
        grid=(rows // tm,),
        in_specs=[_resident(lead.shape)] + _tile_block_specs(nb, per_batch, d)
                 + [_resident((1, d)), _resident(w.shape)],
        out_specs=[pl.BlockSpec((tm, wd), lambda i: (i, 0)) for wd in widths],
        out_shape=outs,
        compiler_params=_params(("parallel",)),
        name="inproj",
    )(lead, x2, x2, g, w)


PAIR = 2 * HEAD_DIM
KEYS = 3 * BLOCK
N_PAIRS = N_Q_HEADS // 2


def _attn_kernel(sink_ref, q_ref, kvp_ref, kvc_ref, kvm_ref, bias_ref, o_ref):
    kw = N_KV_HEADS * HEAD_DIM
    zeros = jnp.zeros((KEYS, HEAD_DIM), BF16)
    ones = jnp.ones((KEYS, HEAD_DIM), BF16)
    lane = lax.broadcasted_iota(I32, (BLOCK, PAIR), 1)
    scale = jnp.asarray(HEAD_DIM ** -0.5, BF16)
    for hk in range(N_KV_HEADS):
        ks = slice(hk * HEAD_DIM, (hk + 1) * HEAD_DIM)
        vs = slice(kw + hk * HEAD_DIM, kw + (hk + 1) * HEAD_DIM)
        k3 = jnp.concatenate([kvp_ref[:, ks], kvc_ref[:, ks], kvm_ref[:, ks]], axis=0) * scale
        v3 = jnp.concatenate([kvp_ref[:, vs], kvc_ref[:, vs], kvm_ref[:, vs]], axis=0)
        kbd = jnp.concatenate([jnp.concatenate([k3, zeros], axis=1),
                               jnp.concatenate([zeros, k3], axis=1)], axis=0)
        vext = jnp.concatenate([jnp.concatenate([v3, zeros, ones, zeros], axis=1),
                                jnp.concatenate([zeros, v3, zeros, ones], axis=1)], axis=0)
        for jp in range(Q_PER_KV // 2):
            j = hk * (Q_PER_KV // 2) + jp
            qp = q_ref[:, j * PAIR:(j + 1) * PAIR]
            s = lax.dot_general(qp, kbd, (((1,), (1,)), ((), ())),
                                preferred_element_type=F32) + bias_ref[0, j]
            probs, sink_terms = [], []
            for side in range(2):
                ss = s[:, side * KEYS:(side + 1) * KEYS]
                sink = sink_ref[2 * j + side]
                mx = jnp.maximum(jnp.max(ss, axis=-1, keepdims=True), sink)
                probs.append(jnp.exp(ss - mx))
                sink_terms.append(jnp.exp(sink - mx))
            p = jnp.concatenate(probs, axis=1).astype(BF16)
            r = jnp.dot(p, vext, preferred_element_type=F32)
            den = r[:, PAIR:] + jnp.where(lane < HEAD_DIM, sink_terms[0], sink_terms[1])
            o_ref[:, j * PAIR:(j + 1) * PAIR] = (r[:, :PAIR] / den).astype(o_ref.dtype)


def _attention(q, kv, bias, sinks, batch, nb):
    rows, aw = q.shape
    kvw = kv.shape[1]
    return pl.pallas_call(
        _attn_kernel,
        grid=(batch, nb),
        in_specs=[pl.BlockSpec(memory_space=pltpu.SMEM),
                  pl.BlockSpec((BLOCK, aw), lambda b, n: (b * nb + n, 0)),
                  pl.BlockSpec((BLOCK, kvw), lambda b, n: (b * nb + jnp.maximum(n - 1, 0), 0)),
                  pl.BlockSpec((BLOCK, kvw), lambda b, n: (b * nb + n, 0)),
                  pl.BlockSpec((BLOCK, kvw), lambda b, n: (b * nb, 0)),
                  pl.BlockSpec((1,) + bias.shape[1:], lambda b, n: (jnp.minimum(n, 2), 0, 0, 0))],
        out_specs=pl.BlockSpec((BLOCK, aw), lambda b, n: (b * nb + n, 0)),
        out_shape=jax.ShapeDtypeStruct((rows, aw), BF16),
        compiler_params=_params(("parallel", "arbitrary")),
        name="attention",
    )(sinks, q, kv, kv, kv, bias)


def _attn_bias_tables(rel_bias):
    max_exact = N_BUCKETS // 2
    qi = np.arange(BLOCK)[:, None]
    kj = np.arange(2 * BLOCK)[None, :]
    mj = np.arange(BLOCK)[None, :]
    dist = qi + BLOCK - kj
    d = np.maximum(dist, 0)
    large = max_exact + (np.log(np.maximum(d, 1).astype(np.float32) / max_exact)
                         / math.log(MAX_DISTANCE / max_exact) * (N_BUCKETS - max_exact)).astype(np.int32)
    bucket = np.where(d < max_exact, d, np.minimum(large, N_BUCKETS - 1))
    onehot = (bucket[..., None] == np.arange(N_BUCKETS)).astype(np.float32)
    rb = rel_bias.astype(F32)
    win = jnp.einsum("qkb,bh->hqk", onehot, rb, precision=lax.Precision.HIGHEST)
    meta = jnp.broadcast_to(rb[N_BUCKETS - 1][:, None, None], (N_Q_HEADS, BLOCK, BLOCK))
    full = jnp.concatenate([win, meta], axis=2)
    masks = []
    for n in range(3):
        win_ok = (dist >= 0) & (dist < WINDOW) & ((n - 1) * BLOCK + kj >= PAD)
        meta_ok = (mj >= PAD) & (n * BLOCK + qi - mj >= WINDOW)
        masks.append(np.concatenate([win_ok, meta_ok], axis=1))
    t = jnp.where(np.stack(masks)[:, None], full[None], -jnp.inf)
    t = t.reshape(3, N_PAIRS, 2, BLOCK, KEYS).transpose(0, 1, 3, 2, 4)
    return t.reshape(3, N_PAIRS, BLOCK, 2 * KEYS)


def _mixer_kernel(lp, attn_ref, u_ref, uh_ref, ga_ref, gp_ref, lead_ref, xa_ref, xb_ref,
                  pmix_ref, pscale_ref, wa_ref, wp_ref, wo_ref, g2_ref, wr_ref, br_ref,
                  h2_ref, xp_ref, route_ref, cnt_ref, run_ref):
    i = pl.program_id(0)
    tm = h2_ref.shape[0]

    @pl.when(i == 0)
    def _():
        run_ref[...] = jnp.zeros_like(run_ref)

    t = (i * tm + lax.broadcasted_iota(I32, (tm, 1), 0)) % lp
    valid = t >= PAD
    tx = (i * tm - POOL_HALO + lp + lax.broadcasted_iota(I32, (tm + POOL_HALO, 1), 0)) % lp
    uext = jnp.concatenate([uh_ref[...], u_ref[...]], axis=0).astype(F32)
    ug = jnp.where(tx >= PAD, uext, 0.0)
    n_rows = (t - PAD + 1).astype(F32)
    gw = ug.shape[1] // len(POOL_WINDOWS)
    pooled = []
    for gi, w in enumerate(POOL_WINDOWS):
        c = ug[:, gi * gw:(gi + 1) * gw]
        s, span = c, 1
        while span < w:
            s = s[span:] + s[:-span]
            span *= 2
        win = s[POOL_HALO + 1 - w:POOL_HALO + 1 - w + tm]
        n_valid = jnp.clip(n_rows, 1.0, float(w))
        mixed = jnp.where(valid, win / n_valid - c[POOL_HALO:], 0.0)
        pooled.append(jnp.dot(mixed.astype(BF16), pmix_ref[gi], preferred_element_type=F32))
    pool = jnp.concatenate(pooled, axis=1) * pscale_ref[...]

    a = jnp.dot(attn_ref[...], wa_ref[...], preferred_element_type=F32)
    p = jnp.dot(pool.astype(BF16), wp_ref[...], preferred_element_type=F32)
    merged = (jax.nn.sigmoid(ga_ref[...].astype(F32)) * a
              + jax.nn.sigmoid(gp_ref[...].astype(F32)) * p)
    h2 = (_tile_rows(lp // BLOCK, lead_ref, xa_ref, xb_ref)
          + jnp.dot(merged.astype(BF16), wo_ref[...], preferred_element_type=F32))
    h2_ref[...] = h2

    ms = jnp.mean(h2 * h2, axis=-1, keepdims=True)
    hn2 = h2 * lax.rsqrt(ms + RMS_EPS) * g2_ref[...]
    x_hi = hn2.astype(BF16)
    x_hi32 = x_hi.astype(F32)
    half = hn2.shape[1] // 2
    lo = lax.bitcast_convert_type(x_hi32[:, :half], U32)
    hi = lax.bitcast_convert_type(x_hi32[:, half:], U32)
    xp_ref[...] = (hi & jnp.uint32(0xFFFF0000)) | (lo >> 16)

    x_lo = (hn2 - x_hi32).astype(BF16)
    hi_prod = jnp.dot(x_hi, wr_ref[...], preferred_element_type=F32)
    lo_prod = jnp.dot(x_lo, wr_ref[:, :LANES], preferred_element_type=F32)
    logits = hi_prod[:, :LANES] + (hi_prod[:, LANES:] + lo_prod) + br_ref[...]
    col = lax.broadcasted_iota(I32, logits.shape, 1).astype(F32)
    neg = -jnp.inf
    gl = jnp.where(col < N_GROUPS, logits, neg)
    gmax = jnp.max(gl, axis=-1, keepdims=True)
    grp = jnp.min(jnp.where(gl == gmax, col, float(LANES)), axis=-1, keepdims=True)
    p_grp = 1.0 / jnp.sum(jnp.exp(gl - gmax), axis=-1, keepdims=True)
    e_lo = N_GROUPS + grp * EXPERTS_PER_GROUP
    el = jnp.where((col >= e_lo) & (col < e_lo + EXPERTS_PER_GROUP), logits, neg)
    m1 = jnp.max(el, axis=-1, keepdims=True)
    i1 = jnp.min(jnp.where(el == m1, col, float(LANES)), axis=-1, keepdims=True)
    el2 = jnp.where(col == i1, neg, el)
    m2 = jnp.max(el2, axis=-1, keepdims=True)
    i2 = jnp.min(jnp.where(el2 == m2, col, float(LANES)), axis=-1, keepdims=True)
    z = jnp.exp(m2 - m1)
    w1 = p_grp / (1.0 + z)
    w2 = p_grp * z / (1.0 + z)
    e1 = i1 - N_GROUPS
    e2 = i2 - N_GROUPS

    oh1 = jnp.where((col == e1) & valid, 1.0, 0.0)
    oh2 = jnp.where((col == e2) & valid, 1.0, 0.0)
    lower = (lax.broadcasted_iota(I32, (tm, tm), 0) > lax.broadcasted_iota(I32, (tm, tm), 1))
    lower = jnp.where(lower, 1.0, 0.0).astype(BF16)
    before1 = jnp.dot(lower, oh1.astype(BF16), preferred_element_type=F32)
    before2 = jnp.dot(lower, oh2.astype(BF16), preferred_element_type=F32)
    tot1 = jnp.sum(oh1, axis=0, keepdims=True)
    tot2 = jnp.sum(oh2, axis=0, keepdims=True)
    run = run_ref[...]
    r1 = jnp.sum(oh1 * (run + before1), axis=-1, keepdims=True)
    r2 = jnp.sum(oh2 * (run + tot1 + before2), axis=-1, keepdims=True)
    run = run + tot1 + tot2
    run_ref[...] = run
    cnt_ref[...] = run

    slab = jnp.zeros(logits.shape, F32)
    for cidx, val in ((COL_E0, jnp.where(valid, e1, -1.0)),
                      (COL_E1, jnp.where(valid, e2, -1.0)),
                      (COL_W0, w1), (COL_W1, w2), (COL_R0, r1), (COL_R1, r2)):
        slab = jnp.where(col == cidx, val, slab)
    route_ref[...] = slab


def _mixer(lp, attn, u, ga, gp, lead, x2, pmix, pscale, wa, wp, wo, g2, wr, br):
    rows = attn.shape[0]
    d = x2.shape[1]
    tm = ROW_TILE
    nb = lp // BLOCK
    halo_blocks = tm // POOL_HALO
    row = lambda i: (i, 0)
    return pl.pallas_call(
        functools.partial(_mixer_kernel, lp),
        grid=(rows // tm,),
        in_specs=[pl.BlockSpec((tm, attn.shape[1]), row),
                  pl.BlockSpec((tm, u.shape[1]), row),
                  pl.BlockSpec((POOL_HALO, u.shape[1]),
                               lambda i: (jnp.maximum(i * halo_blocks - 1, 0), 0)),
                  pl.BlockSpec((tm, d), row),
                  pl.BlockSpec((tm, d), row),
                  _resident(lead.shape)] + _tile_block_specs(nb, nb - 1, d) + [
                  _resident(pmix.shape), _resident(pscale.shape), _resident(wa.shape),
                  _resident(wp.shape), _resident(wo.shape), _resident(g2.shape),
                  _resident(wr.shape), _resident(br.shape)],
        out_specs=[pl.BlockSpec((tm, d), row),
                   pl.BlockSpec((tm, d // 2), row),
                   pl.BlockSpec((tm, LANES), row),
                   pl.BlockSpec((1, LANES), lambda i: (0, 0))],
        out_shape=[jax.ShapeDtypeStruct((rows, d), F32),
                   jax.ShapeDtypeStruct((rows, d // 2), U32),
                   jax.ShapeDtypeStruct((rows, LANES), F32),
                   jax.ShapeDtypeStruct((1, LANES), F32)],
        scratch_shapes=[pltpu.VMEM((1, LANES), F32)],
        compiler_params=_params(("arbitrary",)),
        name="mixer",
    )(attn, u, u, ga, gp, lead, x2, x2, pmix, pscale, wa, wp, wo, g2, wr, br)


def _zero_unused_slots(gap_lo_ref, gap_hi_ref, used_ref, xs_ref, zero_ref, sem):
    bm = zero_ref.shape[0]
    n_rows = xs_ref.shape[0]
    n_blocks = (n_rows - SPARE_ROWS) // bm
    zero_ref[...] = jnp.zeros_like(zero_ref)

    def piece(start, size):
        return pltpu.make_async_copy(zero_ref.at[pl.ds(0, size), :], xs_ref.at[pl.ds(start, size), :], sem)

    def for_each_piece(op):
        def tail(e, carry):
            lo, hi = gap_lo_ref[e], gap_hi_ref[e]
            length = hi - lo
            for bit in range(int(math.log2(bm)) - 1, 2, -1):
                size = 1 << bit
                above = (length >> (bit + 1)) << (bit + 1)

                @pl.when((length >> bit) & 1 == 1)
                def _():
                    op(piece(pl.multiple_of(hi - above - size, 8), size))
            for j in range(7):
                @pl.when(j < (length & 7))
                def _():
                    op(piece(lo + j, 1))
            return carry

        def block(b, carry):
            op(piece(pl.multiple_of(b * bm, bm), bm))
            return carry

        lax.fori_loop(0, N_EXPERTS, tail, 0)
        lax.fori_loop(used_ref[0], n_blocks + SPARE_ROWS // bm, block, 0)

    for_each_piece(lambda c: c.start())
    for_each_piece(lambda c: c.wait())


def _scatter_kernel(gap_lo_ref, gap_hi_ref, used_ref, pos_ref, x_ref, xs_ref, stage_ref, zero_ref, sem, zsem):
    i = pl.program_id(0)
    last = pl.num_programs(0) - 1
    tm = x_ref.shape[0]
    slot = i % 2

    @pl.when(i == 0)
    def _():
        _zero_unused_slots(gap_lo_ref, gap_hi_ref, used_ref, xs_ref, zero_ref, zsem)

    def retire(s):
        for k in range(2):
            pltpu.make_async_copy(stage_ref.at[s], xs_ref.at[pl.ds(0, tm), :], sem.at[s]).wait()

    @pl.when(i >= 2)
    def _():
        retire(slot)

    stage_ref[slot] = x_ref[...]

    def start(it, carry):
        r0 = pl.multiple_of(it * DMA_UNROLL, DMA_UNROLL)
        for j in range(DMA_UNROLL):
            for k in range(2):
                p = pos_ref[(i * tm + r0 + j) * 2 + k]
                pltpu.make_async_copy(stage_ref.at[slot, pl.ds(r0 + j, 1), :], xs_ref.at[pl.ds(p, 1), :],
                                      sem.at[slot]).start(priority=k)
        return carry

    lax.fori_loop(0, tm // DMA_UNROLL, start, 0)

    @pl.when(i == last)
    def _():
        @pl.when(i >= 1)
        def _():
            retire(1 - slot)
        retire(slot)


def _scatter(gap_lo, gap_hi, n_used, pos, xp, slots):
    rows, half = xp.shape
    tm = ROW_TILE
    assert SPARE_ROWS % EXPERT_ROWS == 0 and slots % EXPERT_ROWS == 0
    return pl.pallas_call(
        _scatter_kernel,
        grid_spec=pltpu.PrefetchScalarGridSpec(
            num_scalar_prefetch=4,
            grid=(rows // tm,),
            in_specs=[pl.BlockSpec((tm, half), lambda i, *_: (i, 0))],
            out_specs=pl.BlockSpec(memory_space=pl.ANY),
            scratch_shapes=[pltpu.VMEM((2, tm, half), U32), pltpu.VMEM((EXPERT_ROWS, half), U32),
                            pltpu.SemaphoreType.DMA((2,)), pltpu.SemaphoreType.DMA(())],
        ),
        out_shape=jax.ShapeDtypeStruct((slots + SPARE_ROWS, half), U32),
        compiler_params=_params(("arbitrary",)),
        name="scatter",
    )(gap_lo, gap_hi, n_used, pos, xp)


def _pack_halves(x):
    n = x.shape[1] // 2
    r = x.astype(BF16).astype(F32)
    lo = lax.bitcast_convert_type(r[:, :n], U32)
    hi = lax.bitcast_convert_type(r[:, n:], U32)
    return (hi & jnp.uint32(0xFFFF0000)) | (lo >> 16)


def _unpack_halves(p):
    return (lax.bitcast_convert_type(p << 16, F32),
            lax.bitcast_convert_type(p & jnp.uint32(0xFFFF0000), F32))


def _expert_kernel(be_ref, nrow_ref, last_ref, ord_ref, used_ref, xs_ref, wg_hbm, wu_hbm, wd_hbm, ys_ref,
                   wg_f, wu_f, wd_f, sem):
    b = pl.program_id(0)
    n_valid_rows = nrow_ref[b]
    expert = be_ref[b]
    ordinal = ord_ref[b]
    n_slots = wg_f.shape[0]
    slot = ordinal % n_slots

    def weight_copies(e, s):
        return [pltpu.make_async_copy(hbm.at[e], buf.at[s], sem.at[s, n])
                for n, (hbm, buf) in enumerate(((wg_hbm, wg_f), (wu_hbm, wu_f), (wd_hbm, wd_f)))]

    def start_fetch(o):
        e = used_ref[o]

        @pl.when(e >= 0)
        def _():
            for c in weight_copies(e, o % n_slots):
                c.start()

    @pl.when(b == 0)
    def _():
        for o in range(WEIGHT_LOOKAHEAD):
            start_fetch(o)

    @pl.when((b == 0) | (expert != be_ref[jnp.maximum(b - 1, 0)]))
    def _():
        start_fetch(ordinal + WEIGHT_LOOKAHEAD)
        for c in weight_copies(expert, slot):
            c.wait()

    @pl.when(n_valid_rows == 0)
    def _():
        ys_ref[...] = jnp.zeros_like(ys_ref)

    @pl.when(n_valid_rows > 0)
    def _():
        lo, hi = _unpack_halves(xs_ref[...])
        x = jnp.concatenate([lo, hi], axis=1).astype(BF16)
        gate = jnp.dot(x, wg_f[slot].astype(BF16), preferred_element_type=F32)
        up = jnp.dot(x, wu_f[slot].astype(BF16), preferred_element_type=F32)
        hb = (jax.nn.silu(gate) * up).astype(BF16)
        ys_ref[...] = _pack_halves(jnp.dot(hb, wd_f[slot].astype(BF16), preferred_element_type=F32))


def _experts(block_expert, block_rows, last_block, block_ord, used_experts, xs, w_gate, w_up, w_down):
    half = xs.shape[1]
    n_exp, d, de = w_gate.shape
    bm = EXPERT_ROWS
    n_blocks = block_expert.shape[0]
    slots = n_blocks * bm
    n_slots = WEIGHT_LOOKAHEAD + 1
    any_space = pl.BlockSpec(memory_space=pl.ANY)
    return pl.pallas_call(
        _expert_kernel,
        grid_spec=pltpu.PrefetchScalarGridSpec(
            num_scalar_prefetch=5,
            grid=(n_blocks,),
            in_specs=[pl.BlockSpec((bm, half), lambda b, be, nr, last, *_: (jnp.minimum(b, last[0]), 0)),
                      any_space, any_space, any_space],
            out_specs=pl.BlockSpec((bm, half), lambda b, *_: (b, 0)),
            scratch_shapes=[pltpu.VMEM((n_slots, d, de), F32), pltpu.VMEM((n_slots, d, de), F32),
                            pltpu.VMEM((n_slots, de, d), F32),
                            pltpu.SemaphoreType.DMA((n_slots, 3))],
        ),
        out_shape=jax.ShapeDtypeStruct((slots, half), U32),
        compiler_params=_params(("arbitrary",)),
        name="experts",
    )(block_expert, block_rows, last_block, block_ord, used_experts, xs, w_gate, w_up, w_down)


def _combine_kernel(nb, per_batch, pos_ref, h2_ref, route_ref, g_ref, ys_ref, o_ref, y_ref, sem):
    t = pl.program_id(0)
    tm = h2_ref.shape[0]

    def issue(tile, slot):
        base = ((tile // per_batch) * nb + 1 + tile % per_batch) * tm

        for r in range(tm):
            for k in range(2):
                p = pos_ref[(base + r) * 2 + k]
                pltpu.make_async_copy(ys_ref.at[pl.ds(p, 1), :], y_ref.at[slot, k, pl.ds(r, 1), :],
                                      sem.at[slot]).start(priority=k)

    @pl.when(t == 0)
    def _():
        issue(0, 0)

    @pl.when(t + 1 < pl.num_programs(0))
    def _():
        issue(t + 1, (t + 1) % 2)

    slot = t % 2
    for k in range(2):
        pltpu.make_async_copy(ys_ref.at[pl.ds(0, tm), :], y_ref.at[slot, k], sem.at[slot]).wait()
    route = route_ref[...]
    w0 = route[:, COL_W0:COL_W0 + 1]
    w1 = route[:, COL_W1:COL_W1 + 1]
    half = y_ref.shape[-1]
    y0 = _unpack_halves(y_ref[slot, 0])
    y1 = _unpack_halves(y_ref[slot, 1])
    hs = [h2_ref[:, s * half:(s + 1) * half] + (w0 * y0[s] + w1 * y1[s]) for s in range(2)]
    ms = sum(jnp.sum(h * h, axis=-1, keepdims=True) for h in hs) / (2 * half)
    inv = lax.rsqrt(ms + RMS_EPS)
    for s in range(2):
        o_ref[0, :, s * half:(s + 1) * half] = hs[s] * inv * g_ref[:, s * half:(s + 1) * half]


def _combine(pos, h2, route, g, ys, batch, nb, seq):
    rows, d = h2.shape
    tm = BLOCK
    per_batch = seq // tm
    tile = lambda t, pos: ((t // per_batch) * nb + 1 + t % per_batch, 0)
    return pl.pallas_call(
        functools.partial(_combine_kernel, nb, per_batch),
        grid_spec=pltpu.PrefetchScalarGridSpec(
            num_scalar_prefetch=1,
            grid=(batch * per_batch,),
            in_specs=[pl.BlockSpec((tm, d), tile),
                      pl.BlockSpec((tm, LANES), tile),
                      pl.BlockSpec((1, d), lambda t, pos: (0, 0)),
                      pl.BlockSpec(memory_space=pl.ANY)],
            out_specs=pl.BlockSpec((1, tm, d), lambda t, pos: (t // per_batch, t % per_batch, 0)),
            scratch_shapes=[pltpu.VMEM((2, 2, tm, ys.shape[1]), U32), pltpu.SemaphoreType.DMA((2,))],
        ),
        out_shape=jax.ShapeDtypeStruct((batch, seq, d), F32),
        compiler_params=_params(("arbitrary",)),
        name="combine",
    )(pos, h2, route, g, ys)


def kernel(x, meta_tokens, rel_bias, norm_mix, w_in, attn_sinks, pool_mix, pool_scale,
           w_attn_branch, w_pool_branch, w_out, norm_ffn, w_router_group, b_router_group,
           w_router_expert, b_router_expert, w_gate, w_up, w_down, norm_final):
    batch, seq, d = x.shape
    depth = w_in.shape[0]
    assert depth == 1, "single-layer stack"
    aw = w_attn_branch.shape[1]
    pw = w_pool_branch.shape[1]
    kvw = 2 * N_KV_HEADS * HEAD_DIM
    assert aw == N_Q_HEADS * HEAD_DIM and w_in.shape[2] == aw + kvw + pw + 2 * d
    assert seq % BLOCK == 0
    lp = seq + BLOCK
    nb = lp // BLOCK
    rows = batch * lp
    assert rows % ROW_TILE == 0

    assert ROW_TILE == 2 * BLOCK
    lead = jnp.concatenate([jnp.zeros((PAD, d), x.dtype), meta_tokens.astype(x.dtype)], axis=0)
    x2 = x.reshape(batch * seq, d)

    q, kv, u, ga, gp = _inproj(lead, x2, nb, norm_mix[0][None], w_in[0].astype(BF16), (aw, kvw, pw, d, d))

    attn = _attention(q, kv, _attn_bias_tables(rel_bias), attn_sinks[0].astype(F32), batch, nb)

    n_router = N_GROUPS + N_EXPERTS
    wr = jnp.concatenate([w_router_group[0], w_router_expert[0], jnp.zeros((d, LANES - n_router), F32)], axis=1)
    br = jnp.concatenate([b_router_group[0], b_router_expert[0], jnp.zeros((LANES - n_router,), F32)])[None]
    wr_hi = wr.astype(BF16)
    wr_split = jnp.concatenate([wr_hi, (wr - wr_hi.astype(F32)).astype(BF16)], axis=1)
    h2, xp, route, counts = _mixer(
        lp, attn, u, ga, gp, lead, x2, pool_mix[0].astype(BF16), pool_scale[0][None].astype(F32),
        w_attn_branch[0].astype(BF16), w_pool_branch[0].astype(BF16), w_out[0].astype(BF16),
        norm_ffn[0][None], wr_split, br)

    bm = EXPERT_ROWS
    n_tok = batch * (seq + N_META)
    n_blocks = (2 * n_tok) // bm + N_EXPERTS
    cnt = counts[0, :N_EXPERTS].astype(I32)
    blocks_e = (cnt + bm - 1) // bm
    bend = jnp.cumsum(blocks_e)
    bstart = bend - blocks_e
    ord_e = jnp.cumsum((blocks_e > 0).astype(I32)) - 1
    n_used = bend[-1]
    last_block = jnp.maximum(n_used - 1, 0)
    blk = jnp.arange(n_blocks, dtype=I32)

    def expert_of(block):
        return jnp.minimum(jnp.sum((bend[None, :] <= block[:, None]).astype(I32), axis=1), N_EXPERTS - 1)

    block_expert = expert_of(jnp.minimum(blk, last_block))
    own = block_expert[:, None] == jnp.arange(N_EXPERTS, dtype=I32)[None, :]
    pick = lambda v: jnp.sum(jnp.where(own, v[None, :], 0), axis=1)
    block_rows = jnp.clip(pick(cnt) - (blk - pick(bstart)) * bm, 0, bm)
    block_rows = jnp.where(blk < n_used, block_rows, 0).astype(I32)
    block_ord = pick(ord_e).astype(I32)
    experts = jnp.arange(N_EXPERTS, dtype=I32)
    is_kth = (ord_e[None, :] == jnp.arange(N_EXPERTS + WEIGHT_LOOKAHEAD, dtype=I32)[:, None]) & (blocks_e > 0)[None, :]
    used_experts = jnp.where(jnp.any(is_kth, axis=1), jnp.sum(jnp.where(is_kth, experts[None, :], 0), axis=1), -1)
    gap_lo = (bstart * bm + cnt).astype(I32)
    gap_hi = (bend * bm).astype(I32)
    e = route[:, COL_E0:COL_E1 + 1].astype(I32)
    rank = route[:, COL_R0:COL_R1 + 1].astype(I32)
    first_row = jnp.sum(jnp.where(e[..., None] == jnp.arange(N_EXPERTS, dtype=I32), bstart * bm, 0), axis=-1)
    spare = n_blocks * bm + jnp.arange(2 * rows, dtype=I32).reshape(rows, 2) % SPARE_ROWS
    pos = jnp.where(e >= 0, first_row + rank, spare).astype(I32).reshape(-1)

    xs = _scatter(gap_lo, gap_hi, n_used.astype(I32).reshape(1), pos, xp, n_blocks * bm)
    ys = _experts(block_expert.astype(I32), block_rows, last_block.astype(I32).reshape(1), block_ord,
                  used_experts.astype(I32), xs, w_gate[0], w_up[0], w_down[0])
    return _combine(pos, h2, route, norm_final[None].astype(F32), ys, batch, nb, seq)
```

```python
import functools
import math

import numpy as np
import jax
import jax.numpy as jnp
from jax import lax
from jax.experimental import pallas as pl
from jax.experimental.pallas import tpu as pltpu

F32 = jnp.float32
BF16 = jnp.bfloat16
I32 = jnp.int32
U32 = jnp.uint32

BLOCK = 128
N_META = 16
PAD = BLOCK - N_META
HEAD_DIM = 64
N_KV_HEADS = 2
Q_PER_KV = 8
N_Q_HEADS = N_KV_HEADS * Q_PER_KV
WINDOW = 128
POOL_WINDOWS = (2, 4, 8, 16)
POOL_HALO = 16
N_BUCKETS = 32
MAX_DISTANCE = 128
N_GROUPS = 8
EXPERTS_PER_GROUP = 8
N_EXPERTS = N_GROUPS * EXPERTS_PER_GROUP
RMS_EPS = 1e-6
LANES = 128
VMEM_LIMIT = 56 * 1024 * 1024

ROW_TILE = 256
EXPERT_ROWS = 256
WEIGHT_LOOKAHEAD = 2
SPARE_ROWS = 4 * ROW_TILE
COL_E0, COL_E1, COL_W0, COL_W1, COL_R0, COL_R1 = 0, 1, 2, 3, 4, 5


def _params(sem, vmem=VMEM_LIMIT):
    return pltpu.CompilerParams(dimension_semantics=sem, vmem_limit_bytes=vmem)


def _resident(shape):
    nd = len(shape)
    return pl.BlockSpec(shape, lambda *_: (0,) * nd, pipeline_mode=pl.Buffered(1))


def _tile_block_specs(nb, per_batch, d):
    def spec(half):
        def index(i, *_):
            g = 2 * i + half
            return ((g // nb) * per_batch + jnp.maximum(g % nb - 1, 0), 0)
        return pl.BlockSpec((BLOCK, d), index)
    return [spec(0), spec(1)]


def _tile_rows(nb, lead_ref, xa_ref, xb_ref):
    i = pl.program_id(0)
    halves = [jnp.where((2 * i + half) % nb == 0, lead_ref[...], ref[...])
              for half, ref in enumerate((xa_ref, xb_ref))]
    return jnp.concatenate(halves, axis=0)


def _inproj_kernel(nb, lead_ref, xa_ref, xb_ref, g_ref, w_ref, q_ref, kv_ref, u_ref, ga_ref, gp_ref):
    x = _tile_rows(nb, lead_ref, xa_ref, xb_ref)
    ms = jnp.mean(x * x, axis=-1, keepdims=True)
    hn = (x * lax.rsqrt(ms + RMS_EPS) * g_ref[...]).astype(BF16)
    off = 0
    for ref in (q_ref, kv_ref, u_ref, ga_ref, gp_ref):
        width = ref.shape[1]
        for c in range(0, width, 1024):
            cw = min(1024, width - c)
            ref[:, c:c + cw] = jnp.dot(hn, w_ref[:, off + c:off + c + cw],
                                       preferred_element_type=F32).astype(BF16)
        off += width


def _inproj(lead, x2, nb, g, w, widths):
    d = x2.shape[1]
    tm = ROW_TILE
    per_batch = nb - 1
    rows = x2.shape[0] // per_batch * nb
    outs = [jax.ShapeDtypeStruct((rows, wd), BF16) for wd in widths]
    return pl.pallas_call(
        functools.partial(_inproj_kernel, nb),
        grid=(rows // tm,),
        in_specs=[_resident(lead.shape)] + _tile_block_specs(nb, per_batch, d)
                 + [_resident((1, d)), _resident(w.shape)],
        out_specs=[pl.BlockSpec((tm, wd), lambda i: (i, 0)) for wd in widths],
        out_shape=outs,
        compiler_params=_params(("parallel",)),
        name="inproj",
    )(lead, x2, x2, g, w)


PAIR = 2 * HEAD_DIM
KEYS = 3 * BLOCK
N_PAIRS = N_Q_HEADS // 2


def _attn_kernel(sink_ref, q_ref, kvp_ref, kvc_ref, kvm_ref, bias_ref, o_ref):
    kw = N_KV_HEADS * HEAD_DIM
    zeros = jnp.zeros((KEYS, HEAD_DIM), BF16)
    ones = jnp.ones((KEYS, HEAD_DIM), BF16)
    lane = lax.broadcasted_iota(I32, (BLOCK, PAIR), 1)
    scale = jnp.asarray(HEAD_DIM ** -0.5, BF16)
    for hk in range(N_KV_HEADS):
        ks = slice(hk * HEAD_DIM, (hk + 1) * HEAD_DIM)
        vs = slice(kw + hk * HEAD_DIM, kw + (hk + 1) * HEAD_DIM)
        k3 = jnp.concatenate([kvp_ref[:, ks], kvc_ref[:, ks], kvm_ref[:, ks]], axis=0) * scale
        v3 = jnp.concatenate([kvp_ref[:, vs], kvc_ref[:, vs], kvm_ref[:, vs]], axis=0)
        kbd = jnp.concatenate([jnp.concatenate([k3, zeros], axis=1),
                               jnp.concatenate([zeros, k3], axis=1)], axis=0)
        vext = jnp.concatenate([jnp.concatenate([v3, zeros, ones, zeros], axis=1),
                                jnp.concatenate([zeros, v3, zeros, ones], axis=1)], axis=0)
        for jp in range(Q_PER_KV // 2):
            j = hk * (Q_PER_KV // 2) + jp
            qp = q_ref[:, j * PAIR:(j + 1) * PAIR]
            s = lax.dot_general(qp, kbd, (((1,), (1,)), ((), ())),
                                preferred_element_type=F32) + bias_ref[0, j]
            probs, sink_terms = [], []
            for side in range(2):
                ss = s[:, side * KEYS:(side + 1) * KEYS]
                sink = sink_ref[2 * j + side]
                mx = jnp.maximum(jnp.max(ss, axis=-1, keepdims=True), sink)
                probs.append(jnp.exp(ss - mx))
                sink_terms.append(jnp.exp(sink - mx))
            p = jnp.concatenate(probs, axis=1).astype(BF16)
            r = jnp.dot(p, vext, preferred_element_type=F32)
            den = r[:, PAIR:] + jnp.where(lane < HEAD_DIM, sink_terms[0], sink_terms[1])
            o_ref[:, j * PAIR:(j + 1) * PAIR] = (r[:, :PAIR] / den).astype(o_ref.dtype)


def _attention(q, kv, bias, sinks, batch, nb):
    rows, aw = q.shape
    kvw = kv.shape[1]
    return pl.pallas_call(
        _attn_kernel,
        grid=(batch, nb),
        in_specs=[pl.BlockSpec(memory_space=pltpu.SMEM),
                  pl.BlockSpec((BLOCK, aw), lambda b, n: (b * nb + n, 0)),
                  pl.BlockSpec((BLOCK, kvw), lambda b, n: (b * nb + jnp.maximum(n - 1, 0), 0)),
                  pl.BlockSpec((BLOCK, kvw), lambda b, n: (b * nb + n, 0)),
                  pl.BlockSpec((BLOCK, kvw), lambda b, n: (b * nb, 0)),
                  pl.BlockSpec((1,) + bias.shape[1:], lambda b, n: (jnp.minimum(n, 2), 0, 0, 0))],
        out_specs=pl.BlockSpec((BLOCK, aw), lambda b, n: (b * nb + n, 0)),
        out_shape=jax.ShapeDtypeStruct((rows, aw), BF16),
        compiler_params=_params(("parallel", "arbitrary")),
        name="attention",
    )(sinks, q, kv, kv, kv, bias)


def _attn_bias_tables(rel_bias):
    max_exact = N_BUCKETS // 2
    qi = np.arange(BLOCK)[:, None]
    kj = np.arange(2 * BLOCK)[None, :]
    mj = np.arange(BLOCK)[None, :]
    dist = qi + BLOCK - kj
    d = np.maximum(dist, 0)
    large = max_exact + (np.log(np.maximum(d, 1).astype(np.float32) / max_exact)
                         / math.log(MAX_DISTANCE / max_exact) * (N_BUCKETS - max_exact)).astype(np.int32)
    bucket = np.where(d < max_exact, d, np.minimum(large, N_BUCKETS - 1))
    onehot = (bucket[..., None] == np.arange(N_BUCKETS)).astype(np.float32)
    rb = rel_bias.astype(F32)
    win = jnp.einsum("qkb,bh->hqk", onehot, rb, precision=lax.Precision.HIGHEST)
    meta = jnp.broadcast_to(rb[N_BUCKETS - 1][:, None, None], (N_Q_HEADS, BLOCK, BLOCK))
    full = jnp.concatenate([win, meta], axis=2)
    masks = []
    for n in range(3):
        win_ok = (dist >= 0) & (dist < WINDOW) & ((n - 1) * BLOCK + kj >= PAD)
        meta_ok = (mj >= PAD) & (n * BLOCK + qi - mj >= WINDOW)
        masks.append(np.concatenate([win_ok, meta_ok], axis=1))
    t = jnp.where(np.stack(masks)[:, None], full[None], -jnp.inf)
    t = t.reshape(3, N_PAIRS, 2, BLOCK, KEYS).transpose(0, 1, 3, 2, 4)
    return t.reshape(3, N_PAIRS, BLOCK, 2 * KEYS)


def _mixer_kernel(lp, attn_ref, u_ref, uh_ref, ga_ref, gp_ref, lead_ref, xa_ref, xb_ref,
                  pmix_ref, pscale_ref, wa_ref, wp_ref, wo_ref, g2_ref, wr_ref, br_ref,
                  h2_ref, xp_ref, route_ref, cnt_ref, run_ref):
    i = pl.program_id(0)
    tm = h2_ref.shape[0]

    @pl.when(i == 0)
    def _():
        run_ref[...] = jnp.zeros_like(run_ref)

    t = (i * tm + lax.broadcasted_iota(I32, (tm, 1), 0)) % lp
    valid = t >= PAD
    tx = (i * tm - POOL_HALO + lp + lax.broadcasted_iota(I32, (tm + POOL_HALO, 1), 0)) % lp
    uext = jnp.concatenate([uh_ref[...], u_ref[...]], axis=0).astype(F32)
    ug = jnp.where(tx >= PAD, uext, 0.0)
    n_rows = (t - PAD + 1).astype(F32)
    gw = ug.shape[1] // len(POOL_WINDOWS)
    pooled = []
    for gi, w in enumerate(POOL_WINDOWS):
        c = ug[:, gi * gw:(gi + 1) * gw]
        s, span = c, 1
        while span < w:
            s = s[span:] + s[:-span]
            span *= 2
        win = s[POOL_HALO + 1 - w:POOL_HALO + 1 - w + tm]
        n_valid = jnp.clip(n_rows, 1.0, float(w))
        mixed = jnp.where(valid, win / n_valid - c[POOL_HALO:], 0.0)
        pooled.append(jnp.dot(mixed.astype(BF16), pmix_ref[gi], preferred_element_type=F32))
    pool = jnp.concatenate(pooled, axis=1) * pscale_ref[...]

    a = jnp.dot(attn_ref[...], wa_ref[...], preferred_element_type=F32)
    p = jnp.dot(pool.astype(BF16), wp_ref[...], preferred_element_type=F32)
    merged = (jax.nn.sigmoid(ga_ref[...].astype(F32)) * a
              + jax.nn.sigmoid(gp_ref[...].astype(F32)) * p)
    h2 = (_tile_rows(lp // BLOCK, lead_ref, xa_ref, xb_ref)
          + jnp.dot(merged.astype(BF16), wo_ref[...], preferred_element_type=F32))
    h2_ref[...] = h2

    ms = jnp.mean(h2 * h2, axis=-1, keepdims=True)
    hn2 = h2 * lax.rsqrt(ms + RMS_EPS) * g2_ref[...]
    x_hi = hn2.astype(BF16)
    x_hi32 = x_hi.astype(F32)
    half = hn2.shape[1] // 2
    lo = lax.bitcast_convert_type(x_hi32[:, :half], U32)
    hi = lax.bitcast_convert_type(x_hi32[:, half:], U32)
    xp_ref[...] = (hi & jnp.uint32(0xFFFF0000)) | (lo >> 16)

    x_lo = (hn2 - x_hi32).astype(BF16)
    hi_prod = jnp.dot(x_hi, wr_ref[...], preferred_element_type=F32)
    lo_prod = jnp.dot(x_lo, wr_ref[:, :LANES], preferred_element_type=F32)
    logits = hi_prod[:, :LANES] + (hi_prod[:, LANES:] + lo_prod) + br_ref[...]
    col = lax.broadcasted_iota(I32, logits.shape, 1).astype(F32)
    neg = -jnp.inf
    gl = jnp.where(col < N_GROUPS, logits, neg)
    gmax = jnp.max(gl, axis=-1, keepdims=True)
    grp = jnp.min(jnp.where(gl == gmax, col, float(LANES)), axis=-1, keepdims=True)
    p_grp = 1.0 / jnp.sum(jnp.exp(gl - gmax), axis=-1, keepdims=True)
    e_lo = N_GROUPS + grp * EXPERTS_PER_GROUP
    el = jnp.where((col >= e_lo) & (col < e_lo + EXPERTS_PER_GROUP), logits, neg)
    m1 = jnp.max(el, axis=-1, keepdims=True)
    i1 = jnp.min(jnp.where(el == m1, col, float(LANES)), axis=-1, keepdims=True)
    el2 = jnp.where(col == i1, neg, el)
    m2 = jnp.max(el2, axis=-1, keepdims=True)
    i2 = jnp.min(jnp.where(el2 == m2, col, float(LANES)), axis=-1, keepdims=True)
    z = jnp.exp(m2 - m1)
    w1 = p_grp / (1.0 + z)
    w2 = p_grp * z / (1.0 + z)
    e1 = i1 - N_GROUPS
    e2 = i2 - N_GROUPS

    oh1 = jnp.where((col == e1) & valid, 1.0, 0.0)
    oh2 = jnp.where((col == e2) & valid, 1.0, 0.0)
    lower = (lax.broadcasted_iota(I32, (tm, tm), 0) > lax.broadcasted_iota(I32, (tm, tm), 1))
    lower = jnp.where(lower, 1.0, 0.0).astype(BF16)
    before1 = jnp.dot(lower, oh1.astype(BF16), preferred_element_type=F32)
    before2 = jnp.dot(lower, oh2.astype(BF16), preferred_element_type=F32)
    tot1 = jnp.sum(oh1, axis=0, keepdims=True)
    tot2 = jnp.sum(oh2, axis=0, keepdims=True)
    run = run_ref[...]
    r1 = jnp.sum(oh1 * (run + before1), axis=-1, keepdims=True)
    r2 = jnp.sum(oh2 * (run + tot1 + before2), axis=-1, keepdims=True)
    run = run + tot1 + tot2
    run_ref[...] = run
    cnt_ref[...] = run

    slab = jnp.zeros(logits.shape, F32)
    for cidx, val in ((COL_E0, jnp.where(valid, e1, -1.0)),
                      (COL_E1, jnp.where(valid, e2, -1.0)),
                      (COL_W0, w1), (COL_W1, w2), (COL_R0, r1), (COL_R1, r2)):
        slab = jnp.where(col == cidx, val, slab)
    route_ref[...] = slab


def _mixer(lp, attn, u, ga, gp, lead, x2, pmix, pscale, wa, wp, wo, g2, wr, br):
    rows = attn.shape[0]
    d = x2.shape[1]
    tm = ROW_TILE
    nb = lp // BLOCK
    halo_blocks = tm // POOL_HALO
    row = lambda i: (i, 0)
    return pl.pallas_call(
        functools.partial(_mixer_kernel, lp),
        grid=(rows // tm,),
        in_specs=[pl.BlockSpec((tm, attn.shape[1]), row),
                  pl.BlockSpec((tm, u.shape[1]), row),
                  pl.BlockSpec((POOL_HALO, u.shape[1]),
                               lambda i: (jnp.maximum(i * halo_blocks - 1, 0), 0)),
                  pl.BlockSpec((tm, d), row),
                  pl.BlockSpec((tm, d), row),
                  _resident(lead.shape)] + _tile_block_specs(nb, nb - 1, d) + [
                  _resident(pmix.shape), _resident(pscale.shape), _resident(wa.shape),
                  _resident(wp.shape), _resident(wo.shape), _resident(g2.shape),
                  _resident(wr.shape), _resident(br.shape)],
        out_specs=[pl.BlockSpec((tm, d), row),
                   pl.BlockSpec((tm, d // 2), row),
                   pl.BlockSpec((tm, LANES), row),
                   pl.BlockSpec((1, LANES), lambda i: (0, 0))],
        out_shape=[jax.ShapeDtypeStruct((rows, d), F32),
                   jax.ShapeDtypeStruct((rows, d // 2), U32),
                   jax.ShapeDtypeStruct((rows, LANES), F32),
                   jax.ShapeDtypeStruct((1, LANES), F32)],
        scratch_shapes=[pltpu.VMEM((1, LANES), F32)],
        compiler_params=_params(("arbitrary",)),
        name="mixer",
    )(attn, u, u, ga, gp, lead, x2, x2, pmix, pscale, wa, wp, wo, g2, wr, br)


def _zero_unused_slots(gap_lo_ref, gap_hi_ref, used_ref, xs_ref, zero_ref, sem):
    bm = zero_ref.shape[0]
    n_rows = xs_ref.shape[0]
    n_blocks = (n_rows - SPARE_ROWS) // bm
    zero_ref[...] = jnp.zeros_like(zero_ref)

    def piece(start, size):
        return pltpu.make_async_copy(zero_ref.at[pl.ds(0, size), :], xs_ref.at[pl.ds(start, size), :], sem)

    def for_each_piece(op):
        def tail(e, carry):
            lo, hi = gap_lo_ref[e], gap_hi_ref[e]
            length = hi - lo
            for bit in range(int(math.log2(bm)) - 1, 2, -1):
                size = 1 << bit
                above = (length >> (bit + 1)) << (bit + 1)

                @pl.when((length >> bit) & 1 == 1)
                def _():
                    op(piece(pl.multiple_of(hi - above - size, 8), size))
            for j in range(7):
                @pl.when(j < (length & 7))
                def _():
                    op(piece(lo + j, 1))
            return carry

        def block(b, carry):
            op(piece(pl.multiple_of(b * bm, bm), bm))
            return carry

        lax.fori_loop(0, N_EXPERTS, tail, 0)
        lax.fori_loop(used_ref[0], n_blocks + SPARE_ROWS // bm, block, 0)

    for_each_piece(lambda c: c.start())
    for_each_piece(lambda c: c.wait())


def _scatter_kernel(gap_lo_ref, gap_hi_ref, used_ref, pos_ref, x_ref, xs_ref, stage_ref, zero_ref, sem, zsem):
    i = pl.program_id(0)
    last = pl.num_programs(0) - 1
    tm = x_ref.shape[0]
    slot = i % 2

    @pl.when(i == 0)
    def _():
        _zero_unused_slots(gap_lo_ref, gap_hi_ref, used_ref, xs_ref, zero_ref, zsem)

    def retire(s):
        for k in range(2):
            pltpu.make_async_copy(stage_ref.at[s], xs_ref.at[pl.ds(0, tm), :], sem.at[s]).wait()

    @pl.when(i >= 2)
    def _():
        retire(slot)

    stage_ref[slot] = x_ref[...]

    for r in range(tm):
        for k in range(2):
            p = pos_ref[(i * tm + r) * 2 + k]
            pltpu.make_async_copy(stage_ref.at[slot, pl.ds(r, 1), :], xs_ref.at[pl.ds(p, 1), :],
                                  sem.at[slot]).start(priority=k)

    @pl.when(i == last)
    def _():
        @pl.when(i >= 1)
        def _():
            retire(1 - slot)
        retire(slot)


def _scatter(gap_lo, gap_hi, n_used, pos, xp, slots):
    rows, half = xp.shape
    tm = ROW_TILE
    assert SPARE_ROWS % EXPERT_ROWS == 0 and slots % EXPERT_ROWS == 0
    return pl.pallas_call(
        _scatter_kernel,
        grid_spec=pltpu.PrefetchScalarGridSpec(
            num_scalar_prefetch=4,
            grid=(rows // tm,),
            in_specs=[pl.BlockSpec((tm, half), lambda i, *_: (i, 0))],
            out_specs=pl.BlockSpec(memory_space=pl.ANY),
            scratch_shapes=[pltpu.VMEM((2, tm, half), U32), pltpu.VMEM((EXPERT_ROWS, half), U32),
                            pltpu.SemaphoreType.DMA((2,)), pltpu.SemaphoreType.DMA(())],
        ),
        out_shape=jax.ShapeDtypeStruct((slots + SPARE_ROWS, half), U32),
        compiler_params=_params(("arbitrary",)),
        name="scatter",
    )(gap_lo, gap_hi, n_used, pos, xp)


def _pack_halves(x):
    n = x.shape[1] // 2
    r = x.astype(BF16).astype(F32)
    lo = lax.bitcast_convert_type(r[:, :n], U32)
    hi = lax.bitcast_convert_type(r[:, n:], U32)
    return (hi & jnp.uint32(0xFFFF0000)) | (lo >> 16)


def _unpack_halves(p):
    return (lax.bitcast_convert_type(p << 16, F32),
            lax.bitcast_convert_type(p & jnp.uint32(0xFFFF0000), F32))


def _expert_kernel(be_ref, nrow_ref, last_ref, ord_ref, used_ref, xs_ref, wg_hbm, wu_hbm, wd_hbm, ys_ref,
                   wg_f, wu_f, wd_f, sem):
    b = pl.program_id(0)
    n_valid_rows = nrow_ref[b]
    expert = be_ref[b]
    ordinal = ord_ref[b]
    n_slots = wg_f.shape[0]
    slot = ordinal % n_slots

    def weight_copies(e, s):
        return [pltpu.make_async_copy(hbm.at[e], buf.at[s], sem.at[s, n])
                for n, (hbm, buf) in enumerate(((wg_hbm, wg_f), (wu_hbm, wu_f), (wd_hbm, wd_f)))]

    def start_fetch(o):
        e = used_ref[o]

        @pl.when(e >= 0)
        def _():
            for c in weight_copies(e, o % n_slots):
                c.start()

    @pl.when(b == 0)
    def _():
        for o in range(WEIGHT_LOOKAHEAD):
            start_fetch(o)

    @pl.when((b == 0) | (expert != be_ref[jnp.maximum(b - 1, 0)]))
    def _():
        start_fetch(ordinal + WEIGHT_LOOKAHEAD)
        for c in weight_copies(expert, slot):
            c.wait()

    @pl.when(n_valid_rows == 0)
    def _():
        ys_ref[...] = jnp.zeros_like(ys_ref)

    @pl.when(n_valid_rows > 0)
    def _():
        lo, hi = _unpack_halves(xs_ref[...])
        x = jnp.concatenate([lo, hi], axis=1).astype(BF16)
        gate = jnp.dot(x, wg_f[slot].astype(BF16), preferred_element_type=F32)
        up = jnp.dot(x, wu_f[slot].astype(BF16), preferred_element_type=F32)
        hb = (jax.nn.silu(gate) * up).astype(BF16)
        ys_ref[...] = _pack_halves(jnp.dot(hb, wd_f[slot].astype(BF16), preferred_element_type=F32))


def _experts(block_expert, block_rows, last_block, block_ord, used_experts, xs, w_gate, w_up, w_down):
    half = xs.shape[1]
    n_exp, d, de = w_gate.shape
    bm = EXPERT_ROWS
    n_blocks = block_expert.shape[0]
    slots = n_blocks * bm
    n_slots = WEIGHT_LOOKAHEAD + 1
    any_space = pl.BlockSpec(memory_space=pl.ANY)
    return pl.pallas_call(
        _expert_kernel,
        grid_spec=pltpu.PrefetchScalarGridSpec(
            num_scalar_prefetch=5,
            grid=(n_blocks,),
            in_specs=[pl.BlockSpec((bm, half), lambda b, be, nr, last, *_: (jnp.minimum(b, last[0]), 0)),
                      any_space, any_space, any_space],
            out_specs=pl.BlockSpec((bm, half), lambda b, *_: (b, 0)),
            scratch_shapes=[pltpu.VMEM((n_slots, d, de), F32), pltpu.VMEM((n_slots, d, de), F32),
                            pltpu.VMEM((n_slots, de, d), F32),
                            pltpu.SemaphoreType.DMA((n_slots, 3))],
        ),
        out_shape=jax.ShapeDtypeStruct((slots, half), U32),
        compiler_params=_params(("arbitrary",)),
        name="experts",
    )(block_expert, block_rows, last_block, block_ord, used_experts, xs, w_gate, w_up, w_down)


def _combine_kernel(nb, per_batch, pos_ref, h2_ref, route_ref, g_ref, ys_ref, o_ref, y_ref, sem):
    t = pl.program_id(0)
    tm = h2_ref.shape[0]

    def issue(tile, slot):
        base = ((tile // per_batch) * nb + 1 + tile % per_batch) * tm

        for r in range(tm):
            for k in range(2):
                p = pos_ref[(base + r) * 2 + k]
                pltpu.make_async_copy(ys_ref.at[pl.ds(p, 1), :], y_ref.at[slot, k, pl.ds(r, 1), :],
                                      sem.at[slot]).start(priority=k)

    @pl.when(t == 0)
    def _():
        issue(0, 0)

    @pl.when(t + 1 < pl.num_programs(0))
    def _():
        issue(t + 1, (t + 1) % 2)

    slot = t % 2
    for k in range(2):
        pltpu.make_async_copy(ys_ref.at[pl.ds(0, tm), :], y_ref.at[slot, k], sem.at[slot]).wait()
    route = route_ref[...]
    w0 = route[:, COL_W0:COL_W0 + 1]
    w1 = route[:, COL_W1:COL_W1 + 1]
    half = y_ref.shape[-1]
    y0 = _unpack_halves(y_ref[slot, 0])
    y1 = _unpack_halves(y_ref[slot, 1])
    hs = [h2_ref[:, s * half:(s + 1) * half] + (w0 * y0[s] + w1 * y1[s]) for s in range(2)]
    ms = sum(jnp.sum(h * h, axis=-1, keepdims=True) for h in hs) / (2 * half)
    inv = lax.rsqrt(ms + RMS_EPS)
    for s in range(2):
        o_ref[0, :, s * half:(s + 1) * half] = hs[s] * inv * g_ref[:, s * half:(s + 1) * half]


def _combine(pos, h2, route, g, ys, batch, nb, seq):
    rows, d = h2.shape
    tm = BLOCK
    per_batch = seq // tm
    tile = lambda t, pos: ((t // per_batch) * nb + 1 + t % per_batch, 0)
    return pl.pallas_call(
        functools.partial(_combine_kernel, nb, per_batch),
        grid_spec=pltpu.PrefetchScalarGridSpec(
            num_scalar_prefetch=1,
            grid=(batch * per_batch,),
            in_specs=[pl.BlockSpec((tm, d), tile),
                      pl.BlockSpec((tm, LANES), tile),
                      pl.BlockSpec((1, d), lambda t, pos: (0, 0)),
                      pl.BlockSpec(memory_space=pl.ANY)],
            out_specs=pl.BlockSpec((1, tm, d), lambda t, pos: (t // per_batch, t % per_batch, 0)),
            scratch_shapes=[pltpu.VMEM((2, 2, tm, ys.shape[1]), U32), pltpu.SemaphoreType.DMA((2,))],
        ),
        out_shape=jax.ShapeDtypeStruct((batch, seq, d), F32),
        compiler_params=_params(("arbitrary",)),
        name="combine",
    )(pos, h2, route, g, ys)


def kernel(x, meta_tokens, rel_bias, norm_mix, w_in, attn_sinks, pool_mix, pool_scale,
           w_attn_branch, w_pool_branch, w_out, norm_ffn, w_router_group, b_router_group,
           w_router_expert, b_router_expert, w_gate, w_up, w_down, norm_final):
    batch, seq, d = x.shape
    depth = w_in.shape[0]
    assert depth == 1, "single-layer stack"
    aw = w_attn_branch.shape[1]
    pw = w_pool_branch.shape[1]
    kvw = 2 * N_KV_HEADS * HEAD_DIM
    assert aw == N_Q_HEADS * HEAD_DIM and w_in.shape[2] == aw + kvw + pw + 2 * d
    assert seq % BLOCK == 0
    lp = seq + BLOCK
    nb = lp // BLOCK
    rows = batch * lp
    assert rows % ROW_TILE == 0

    assert ROW_TILE == 2 * BLOCK
    lead = jnp.concatenate([jnp.zeros((PAD, d), x.dtype), meta_tokens.astype(x.dtype)], axis=0)
    x2 = x.reshape(batch * seq, d)

    q, kv, u, ga, gp = _inproj(lead, x2, nb, norm_mix[0][None], w_in[0].astype(BF16), (aw, kvw, pw, d, d))

    attn = _attention(q, kv, _attn_bias_tables(rel_bias), attn_sinks[0].astype(F32), batch, nb)

    n_router = N_GROUPS + N_EXPERTS
    wr = jnp.concatenate([w_router_group[0], w_router_expert[0], jnp.zeros((d, LANES - n_router), F32)], axis=1)
    br = jnp.concatenate([b_router_group[0], b_router_expert[0], jnp.zeros((LANES - n_router,), F32)])[None]
    wr_hi = wr.astype(BF16)
    wr_split = jnp.concatenate([wr_hi, (wr - wr_hi.astype(F32)).astype(BF16)], axis=1)
    h2, xp, route, counts = _mixer(
        lp, attn, u, ga, gp, lead, x2, pool_mix[0].astype(BF16), pool_scale[0][None].astype(F32),
        w_attn_branch[0].astype(BF16), w_pool_branch[0].astype(BF16), w_out[0].astype(BF16),
        norm_ffn[0][None], wr_split, br)

    bm = EXPERT_ROWS
    n_tok = batch * (seq + N_META)
    n_blocks = (2 * n_tok) // bm + N_EXPERTS
    cnt = counts[0, :N_EXPERTS].astype(I32)
    blocks_e = (cnt + bm - 1) // bm
    bend = jnp.cumsum(blocks_e)
    bstart = bend - blocks_e
    ord_e = jnp.cumsum((blocks_e > 0).astype(I32)) - 1
    n_used = bend[-1]
    last_block = jnp.maximum(n_used - 1, 0)
    blk = jnp.arange(n_blocks, dtype=I32)

    def expert_of(block):
        return jnp.minimum(jnp.sum((bend[None, :] <= block[:, None]).astype(I32), axis=1), N_EXPERTS - 1)

    block_expert = expert_of(jnp.minimum(blk, last_block))
    own = block_expert[:, None] == jnp.arange(N_EXPERTS, dtype=I32)[None, :]
    pick = lambda v: jnp.sum(jnp.where(own, v[None, :], 0), axis=1)
    block_rows = jnp.clip(pick(cnt) - (blk - pick(bstart)) * bm, 0, bm)
    block_rows = jnp.where(blk < n_used, block_rows, 0).astype(I32)
    block_ord = pick(ord_e).astype(I32)
    experts = jnp.arange(N_EXPERTS, dtype=I32)
    is_kth = (ord_e[None, :] == jnp.arange(N_EXPERTS + WEIGHT_LOOKAHEAD, dtype=I32)[:, None]) & (blocks_e > 0)[None, :]
    used_experts = jnp.where(jnp.any(is_kth, axis=1), jnp.sum(jnp.where(is_kth, experts[None, :], 0), axis=1), -1)
    gap_lo = (bstart * bm + cnt).astype(I32)
    gap_hi = (bend * bm).astype(I32)
    e = route[:, COL_E0:COL_E1 + 1].astype(I32)
    rank = route[:, COL_R0:COL_R1 + 1].astype(I32)
    first_row = jnp.sum(jnp.where(e[..., None] == jnp.arange(N_EXPERTS, dtype=I32), bstart * bm, 0), axis=-1)
    spare = n_blocks * bm + jnp.arange(2 * rows, dtype=I32).reshape(rows, 2) % SPARE_ROWS
    pos = jnp.where(e >= 0, first_row + rank, spare).astype(I32).reshape(-1)

    xs = _scatter(gap_lo, gap_hi, n_used.astype(I32).reshape(1), pos, xp, n_blocks * bm)
    ys = _experts(block_expert.astype(I32), block_rows, last_block.astype(I32).reshape(1), block_ord,
                  used_experts.astype(I32), xs, w_gate[0], w_up[0], w_down[0])
    return _combine(pos, h2, route, norm_final[None].astype(F32), ys, batch, nb, seq)
```

```python
import functools
import math

import numpy as np
import jax
import jax.numpy as jnp
from jax import lax
from jax.experimental import pallas as pl
from jax.experimental.pallas import tpu as pltpu

F32 = jnp.float32
BF16 = jnp.bfloat16
I32 = jnp.int32
U32 = jnp.uint32

BLOCK = 128
N_META = 16
PAD = BLOCK - N_META
HEAD_DIM = 64
N_KV_HEADS = 2
Q_PER_KV = 8
N_Q_HEADS = N_KV_HEADS * Q_PER_KV
WINDOW = 128
POOL_WINDOWS = (2, 4, 8, 16)
POOL_HALO = 16
N_BUCKETS = 32
MAX_DISTANCE = 128
N_GROUPS = 8
EXPERTS_PER_GROUP = 8
N_EXPERTS = N_GROUPS * EXPERTS_PER_GROUP
RMS_EPS = 1e-6
LANES = 128
VMEM_LIMIT = 56 * 1024 * 1024

ROW_TILE = 256
EXPERT_ROWS = 256
WEIGHT_LOOKAHEAD = 2
SPARE_ROWS = 4 * ROW_TILE
COL_E0, COL_E1, COL_W0, COL_W1, COL_R0, COL_R1 = 0, 1, 2, 3, 4, 5


def _params(sem, vmem=VMEM_LIMIT):
    return pltpu.CompilerParams(dimension_semantics=sem, vmem_limit_bytes=vmem)


def _resident(shape):
    nd = len(shape)
    return pl.BlockSpec(shape, lambda *_: (0,) * nd, pipeline_mode=pl.Buffered(1))


def _tile_block_specs(nb, per_batch, d):
    def spec(half):
        def index(i, *_):
            g = 2 * i + half
            return ((g // nb) * per_batch + jnp.maximum(g % nb - 1, 0), 0)
        return pl.BlockSpec((BLOCK, d), index)
    return [spec(0), spec(1)]


def _tile_rows(nb, lead_ref, xa_ref, xb_ref):
    i = pl.program_id(0)
    halves = [jnp.where((2 * i + half) % nb == 0, lead_ref[...], ref[...])
              for half, ref in enumerate((xa_ref, xb_ref))]
    return jnp.concatenate(halves, axis=0)


def _inproj_kernel(nb, lead_ref, xa_ref, xb_ref, g_ref, w_ref, q_ref, kv_ref, u_ref, ga_ref, gp_ref):
    x = _tile_rows(nb, lead_ref, xa_ref, xb_ref)
    ms = jnp.mean(x * x, axis=-1, keepdims=True)
    hn = (x * lax.rsqrt(ms + RMS_EPS) * g_ref[...]).astype(BF16)
    off = 0
    for ref in (q_ref, kv_ref, u_ref, ga_ref, gp_ref):
        width = ref.shape[1]
        for c in range(0, width, 1024):
            cw = min(1024, width - c)
            ref[:, c:c + cw] = jnp.dot(hn, w_ref[:, off + c:off + c + cw],
                                       preferred_element_type=F32).astype(BF16)
        off += width


def _inproj(lead, x2, nb, g, w, widths):
    d = x2.shape[1]
    tm = ROW_TILE
    per_batch = nb - 1
    rows = x2.shape[0] // per_batch * nb
    outs = [jax.ShapeDtypeStruct((rows, wd), BF16) for wd in widths]
    return pl.pallas_call(
        functools.partial(_inproj_kernel, nb),
        grid=(rows // tm,),
        in_specs=[_resident(lead.shape)] + _tile_block_specs(nb, per_batch, d)
                 + [_resident((1, d)), _resident(w.shape)],
        out_specs=[pl.BlockSpec((tm, wd), lambda i: (i, 0)) for wd in widths],
        out_shape=outs,
        compiler_params=_params(("parallel",)),
        name="inproj",
    )(lead, x2, x2, g, w)


PAIR = 2 * HEAD_DIM
KEYS = 3 * BLOCK
N_PAIRS = N_Q_HEADS // 2


def _attn_kernel(sink_ref, q_ref, kvp_ref, kvc_ref, kvm_ref, bias_ref, o_ref):
    kw = N_KV_HEADS * HEAD_DIM
    zeros = jnp.zeros((KEYS, HEAD_DIM), BF16)
    ones = jnp.ones((KEYS, HEAD_DIM), BF16)
    lane = lax.broadcasted_iota(I32, (BLOCK, PAIR), 1)
    scale = jnp.asarray(HEAD_DIM ** -0.5, BF16)
    for hk in range(N_KV_HEADS):
        ks = slice(hk * HEAD_DIM, (hk + 1) * HEAD_DIM)
        vs = slice(kw + hk * HEAD_DIM, kw + (hk + 1) * HEAD_DIM)
        k3 = jnp.concatenate([kvp_ref[:, ks], kvc_ref[:, ks], kvm_ref[:, ks]], axis=0) * scale
        v3 = jnp.concatenate([kvp_ref[:, vs], kvc_ref[:, vs], kvm_ref[:, vs]], axis=0)
        kbd = jnp.concatenate([jnp.concatenate([k3, zeros], axis=1),
                               jnp.concatenate([zeros, k3], axis=1)], axis=0)
        vext = jnp.concatenate([jnp.concatenate([v3, zeros, ones, zeros], axis=1),
                                jnp.concatenate([zeros, v3, zeros, ones], axis=1)], axis=0)
        for jp in range(Q_PER_KV // 2):
            j = hk * (Q_PER_KV // 2) + jp
            qp = q_ref[:, j * PAIR:(j + 1) * PAIR]
            s = lax.dot_general(qp, kbd, (((1,), (1,)), ((), ())),
                                preferred_element_type=F32) + bias_ref[0, j]
            probs, sink_terms = [], []
            for side in range(2):
                ss = s[:, side * KEYS:(side + 1) * KEYS]
                sink = sink_ref[2 * j + side]
                mx = jnp.maximum(jnp.max(ss, axis=-1, keepdims=True), sink)
                probs.append(jnp.exp(ss - mx))
                sink_terms.append(jnp.exp(sink - mx))
            p = jnp.concatenate(probs, axis=1).astype(BF16)
            r = jnp.dot(p, vext, preferred_element_type=F32)
            den = r[:, PAIR:] + jnp.where(lane < HEAD_DIM, sink_terms[0], sink_terms[1])
            o_ref[:, j * PAIR:(j + 1) * PAIR] = (r[:, :PAIR] / den).astype(o_ref.dtype)


def _attention(q, kv, bias, sinks, batch, nb):
    rows, aw = q.shape
    kvw = kv.shape[1]
    return pl.pallas_call(
        _attn_kernel,
        grid=(batch, nb),
        in_specs=[pl.BlockSpec(memory_space=pltpu.SMEM),
                  pl.BlockSpec((BLOCK, aw), lambda b, n: (b * nb + n, 0)),
                  pl.BlockSpec((BLOCK, kvw), lambda b, n: (b * nb + jnp.maximum(n - 1, 0), 0)),
                  pl.BlockSpec((BLOCK, kvw), lambda b, n: (b * nb + n, 0)),
                  pl.BlockSpec((BLOCK, kvw), lambda b, n: (b * nb, 0)),
                  pl.BlockSpec((1,) + bias.shape[1:], lambda b, n: (jnp.minimum(n, 2), 0, 0, 0))],
        out_specs=pl.BlockSpec((BLOCK, aw), lambda b, n: (b * nb + n, 0)),
        out_shape=jax.ShapeDtypeStruct((rows, aw), BF16),
        compiler_params=_params(("parallel", "arbitrary")),
        name="attention",
    )(sinks, q, kv, kv, kv, bias)


def _attn_bias_tables(rel_bias):
    max_exact = N_BUCKETS // 2
    qi = np.arange(BLOCK)[:, None]
    kj = np.arange(2 * BLOCK)[None, :]
    mj = np.arange(BLOCK)[None, :]
    dist = qi + BLOCK - kj
    d = np.maximum(dist, 0)
    large = max_exact + (np.log(np.maximum(d, 1).astype(np.float32) / max_exact)
                         / math.log(MAX_DISTANCE / max_exact) * (N_BUCKETS - max_exact)).astype(np.int32)
    bucket = np.where(d < max_exact, d, np.minimum(large, N_BUCKETS - 1))
    onehot = (bucket[..., None] == np.arange(N_BUCKETS)).astype(np.float32)
    rb = rel_bias.astype(F32)
    win = jnp.einsum("qkb,bh->hqk", onehot, rb, precision=lax.Precision.HIGHEST)
    meta = jnp.broadcast_to(rb[N_BUCKETS - 1][:, None, None], (N_Q_HEADS, BLOCK, BLOCK))
    full = jnp.concatenate([win, meta], axis=2)
    masks = []
    for n in range(3):
        win_ok = (dist >= 0) & (dist < WINDOW) & ((n - 1) * BLOCK + kj >= PAD)
        meta_ok = (mj >= PAD) & (n * BLOCK + qi - mj >= WINDOW)
        masks.append(np.concatenate([win_ok, meta_ok], axis=1))
    t = jnp.where(np.stack(masks)[:, None], full[None], -jnp.inf)
    t = t.reshape(3, N_PAIRS, 2, BLOCK, KEYS).transpose(0, 1, 3, 2, 4)
    return t.reshape(3, N_PAIRS, BLOCK, 2 * KEYS)


MIX_CHUNKS = 4


def _interleave(pattern, **streams):
    for key in pattern:
        next(streams[key], None)
    for s in streams.values():
        for _ in s:
            pass


def _mixer_kernel(lp, n_tiles, attn_ref, u_ref, uh_ref, ga_ref, gp_ref, lead_ref, xa_ref, xb_ref,
                  pmix_ref, pscale_ref, wa_ref, wp_ref, wo_ref, g2_ref, wr_ref, br_ref,
                  h2_ref, xp_ref, route_ref, cnt_ref, run_ref, h2_s, a_s, p_s, m_s):
    i = pl.program_id(0)
    tm = h2_ref.shape[0]
    d = h2_ref.shape[1]
    cw = d // MIX_CHUNKS

    @pl.when(i == 0)
    def _():
        run_ref[...] = jnp.zeros_like(run_ref)
        h2_s[...] = jnp.zeros_like(h2_s)

    def chain():
        ic = jnp.minimum(i, n_tiles - 1)
        t = (ic * tm + lax.broadcasted_iota(I32, (tm, 1), 0)) % lp
        valid = t >= PAD
        tx = (ic * tm - POOL_HALO + lp + lax.broadcasted_iota(I32, (tm + POOL_HALO, 1), 0)) % lp
        n_rows = (t - PAD + 1).astype(F32)
        gw = u_ref.shape[1] // len(POOL_WINDOWS)
        attn = attn_ref[...]
        aw = d // len(POOL_WINDOWS)
        for gi, w in enumerate(POOL_WINDOWS):
            a_s[:, gi * aw:(gi + 1) * aw] = jnp.dot(attn, wa_ref[:, gi * aw:(gi + 1) * aw],
                                                    preferred_element_type=F32)
            cols = slice(gi * gw, (gi + 1) * gw)
            uext = jnp.concatenate([uh_ref[:, cols], u_ref[:, cols]], axis=0).astype(F32)
            c = jnp.where(tx >= PAD, uext, 0.0)
            s, span = c, 1
            while span < w:
                s = s[span:] + s[:-span]
                span *= 2
            win = s[POOL_HALO + 1 - w:POOL_HALO + 1 - w + tm]
            n_valid = jnp.clip(n_rows, 1.0, float(w))
            mixed = jnp.where(valid, win / n_valid - c[POOL_HALO:], 0.0)
            m_s[:, cols] = (jnp.dot(mixed.astype(BF16), pmix_ref[gi], preferred_element_type=F32)
                            * pscale_ref[:, cols]).astype(BF16)
            yield
        pool = m_s[:, :u_ref.shape[1]]
        for c in range(MIX_CHUNKS):
            cols = slice(c * cw, (c + 1) * cw)
            p_s[:, cols] = jnp.dot(pool, wp_ref[:, cols], preferred_element_type=F32)
            yield
        nb = lp // BLOCK
        halves = [jnp.where((2 * ic + hf) % nb == 0, lead_ref[...], ref[...])
                  for hf, ref in enumerate((xa_ref, xb_ref))]
        h2 = jnp.concatenate(halves, axis=0)
        for c in range(MIX_CHUNKS):
            cols = slice(c * cw, (c + 1) * cw)
            merged = (jax.nn.sigmoid(ga_ref[:, cols].astype(F32)) * a_s[:, cols]
                      + jax.nn.sigmoid(gp_ref[:, cols].astype(F32)) * p_s[:, cols]).astype(BF16)
            yield
            h2 = h2 + jnp.dot(merged, wo_ref[cols, :], preferred_element_type=F32)
            yield
        h2_ref[...] = h2
        h2_s[...] = h2
        yield

    def tail():
        h2p = h2_s[...]
        tp = ((i - 1) * tm + lp + lax.broadcasted_iota(I32, (tm, 1), 0)) % lp
        valid_p = (tp >= PAD) & (i >= 1)
        ms = jnp.mean(h2p * h2p, axis=-1, keepdims=True)
        inv = lax.rsqrt(ms + RMS_EPS)
        yield
        half = d // 2
        hi_prod = jnp.zeros((tm, 2 * LANES), F32)
        lo_prod = jnp.zeros((tm, LANES), F32)
        n_pack = MIX_CHUNKS // 2
        pw = half // n_pack
        for c in range(n_pack):
            parts = []
            for base in (c * pw, half + c * pw):
                hn = h2p[:, base:base + pw] * inv * g2_ref[:, base:base + pw]
                x_hi = hn.astype(BF16)
                x_hi32 = x_hi.astype(F32)
                x_lo = (hn - x_hi32).astype(BF16)
                hi_prod = hi_prod + jnp.dot(x_hi, wr_ref[base:base + pw, :], preferred_element_type=F32)
                lo_prod = lo_prod + jnp.dot(x_lo, wr_ref[base:base + pw, :LANES], preferred_element_type=F32)
                parts.append(lax.bitcast_convert_type(x_hi32, U32))
            xp_ref[:, c * pw:(c + 1) * pw] = (parts[1] & jnp.uint32(0xFFFF0000)) | (parts[0] >> 16)
            yield
        logits = hi_prod[:, :LANES] + (hi_prod[:, LANES:] + lo_prod) + br_ref[...]
        col = lax.broadcasted_iota(I32, logits.shape, 1).astype(F32)
        neg = -jnp.inf
        gl = jnp.where(col < N_GROUPS, logits, neg)
        gmax = jnp.max(gl, axis=-1, keepdims=True)
        grp = jnp.min(jnp.where(gl == gmax, col, float(LANES)), axis=-1, keepdims=True)
        p_grp = 1.0 / jnp.sum(jnp.exp(gl - gmax), axis=-1, keepdims=True)
        yield
        e_lo = N_GROUPS + grp * EXPERTS_PER_GROUP
        el = jnp.where((col >= e_lo) & (col < e_lo + EXPERTS_PER_GROUP), logits, neg)
        m1 = jnp.max(el, axis=-1, keepdims=True)
        i1 = jnp.min(jnp.where(el == m1, col, float(LANES)), axis=-1, keepdims=True)
        el2 = jnp.where(col == i1, neg, el)
        m2 = jnp.max(el2, axis=-1, keepdims=True)
        i2 = jnp.min(jnp.where(el2 == m2, col, float(LANES)), axis=-1, keepdims=True)
        z = jnp.exp(m2 - m1)
        w1 = p_grp / (1.0 + z)
        w2 = p_grp * z / (1.0 + z)
        e1 = i1 - N_GROUPS
        e2 = i2 - N_GROUPS
        yield
        oh1 = jnp.where((col == e1) & valid_p, 1.0, 0.0)
        oh2 = jnp.where((col == e2) & valid_p, 1.0, 0.0)
        lower = (lax.broadcasted_iota(I32, (tm, tm), 0) > lax.broadcasted_iota(I32, (tm, tm), 1))
        lower = jnp.where(lower, 1.0, 0.0).astype(BF16)
        before1 = jnp.dot(lower, oh1.astype(BF16), preferred_element_type=F32)
        before2 = jnp.dot(lower, oh2.astype(BF16), preferred_element_type=F32)
        tot1 = jnp.sum(oh1, axis=0, keepdims=True)
        tot2 = jnp.sum(oh2, axis=0, keepdims=True)
        run = run_ref[...]
        r1 = jnp.sum(oh1 * (run + before1), axis=-1, keepdims=True)
        r2 = jnp.sum(oh2 * (run + tot1 + before2), axis=-1, keepdims=True)
        run = run + tot1 + tot2
        run_ref[...] = run
        cnt_ref[...] = run
        yield
        slab = jnp.zeros(logits.shape, F32)
        for cidx, val in ((COL_E0, jnp.where(valid_p, e1, -1.0)),
                          (COL_E1, jnp.where(valid_p, e2, -1.0)),
                          (COL_W0, w1), (COL_W1, w2), (COL_R0, r1), (COL_R1, r2)):
            slab = jnp.where(col == cidx, val, slab)
        route_ref[...] = slab
        yield

    _interleave("cccc" + "ctctctct" + "cctcctcctcc", c=chain(), t=tail())


def _mixer(lp, attn, u, ga, gp, lead, x2, pmix, pscale, wa, wp, wo, g2, wr, br):
    rows = attn.shape[0]
    d = x2.shape[1]
    tm = ROW_TILE
    nb = lp // BLOCK
    n_tiles = rows // tm
    halo_blocks = tm // POOL_HALO
    cur = lambda i: (jnp.minimum(i, n_tiles - 1), 0)
    prev = lambda i: (jnp.maximum(i - 1, 0), 0)

    def x_spec(half):
        def index(i):
            g = 2 * jnp.minimum(i, n_tiles - 1) + half
            return ((g // nb) * (nb - 1) + jnp.maximum(g % nb - 1, 0), 0)
        return pl.BlockSpec((BLOCK, d), index)

    return pl.pallas_call(
        functools.partial(_mixer_kernel, lp, n_tiles),
        grid=(n_tiles + 1,),
        in_specs=[pl.BlockSpec((tm, attn.shape[1]), cur),
                  pl.BlockSpec((tm, u.shape[1]), cur),
                  pl.BlockSpec((POOL_HALO, u.shape[1]),
                               lambda i: (jnp.maximum(jnp.minimum(i, n_tiles - 1) * halo_blocks - 1, 0), 0)),
                  pl.BlockSpec((tm, d), cur),
                  pl.BlockSpec((tm, d), cur),
                  _resident(lead.shape), x_spec(0), x_spec(1),
                  _resident(pmix.shape), _resident(pscale.shape), _resident(wa.shape),
                  _resident(wp.shape), _resident(wo.shape), _resident(g2.shape),
                  _resident(wr.shape), _resident(br.shape)],
        out_specs=[pl.BlockSpec((tm, d), cur),
                   pl.BlockSpec((tm, d // 2), prev),
                   pl.BlockSpec((tm, LANES), prev),
                   pl.BlockSpec((1, LANES), lambda i: (0, 0))],
        out_shape=[jax.ShapeDtypeStruct((rows, d), F32),
                   jax.ShapeDtypeStruct((rows, d // 2), U32),
                   jax.ShapeDtypeStruct((rows, LANES), F32),
                   jax.ShapeDtypeStruct((1, LANES), F32)],
        scratch_shapes=[pltpu.VMEM((1, LANES), F32), pltpu.VMEM((tm, d), F32),
                        pltpu.VMEM((tm, d), F32), pltpu.VMEM((tm, d), F32), pltpu.VMEM((tm, d), BF16)],
        compiler_params=_params(("arbitrary",)),
        name="mixer",
    )(attn, u, u, ga, gp, lead, x2, x2, pmix, pscale, wa, wp, wo, g2, wr, br)


def _zero_unused_slots(gap_lo_ref, gap_hi_ref, used_ref, xs_ref, zero_ref, sem):
    bm = zero_ref.shape[0]
    n_rows = xs_ref.shape[0]
    n_blocks = (n_rows - SPARE_ROWS) // bm
    zero_ref[...] = jnp.zeros_like(zero_ref)

    def piece(start, size):
        return pltpu.make_async_copy(zero_ref.at[pl.ds(0, size), :], xs_ref.at[pl.ds(start, size), :], sem)

    def for_each_piece(op):
        def tail(e, carry):
            lo, hi = gap_lo_ref[e], gap_hi_ref[e]
            length = hi - lo
            for bit in range(int(math.log2(bm)) - 1, 2, -1):
                size = 1 << bit
                above = (length >> (bit + 1)) << (bit + 1)

                @pl.when((length >> bit) & 1 == 1)
                def _():
                    op(piece(pl.multiple_of(hi - above - size, 8), size))
            for j in range(7):
                @pl.when(j < (length & 7))
                def _():
                    op(piece(lo + j, 1))
            return carry

        def block(b, carry):
            op(piece(pl.multiple_of(b * bm, bm), bm))
            return carry

        lax.fori_loop(0, N_EXPERTS, tail, 0)
        lax.fori_loop(used_ref[0], n_blocks + SPARE_ROWS // bm, block, 0)

    for_each_piece(lambda c: c.start())
    for_each_piece(lambda c: c.wait())


def _scatter_kernel(gap_lo_ref, gap_hi_ref, used_ref, pos_ref, x_ref, xs_ref, stage_ref, zero_ref, sem, zsem):
    i = pl.program_id(0)
    last = pl.num_programs(0) - 1
    tm = x_ref.shape[0]
    slot = i % 2

    @pl.when(i == 0)
    def _():
        _zero_unused_slots(gap_lo_ref, gap_hi_ref, used_ref, xs_ref, zero_ref, zsem)

    def retire(s):
        for k in range(2):
            pltpu.make_async_copy(stage_ref.at[s], xs_ref.at[pl.ds(0, tm), :], sem.at[s]).wait()

    @pl.when(i >= 2)
    def _():
        retire(slot)

    stage_ref[slot] = x_ref[...]

    for r in range(tm):
        for k in range(2):
            p = pos_ref[(i * tm + r) * 2 + k]
            pltpu.make_async_copy(stage_ref.at[slot, pl.ds(r, 1), :], xs_ref.at[pl.ds(p, 1), :],
                                  sem.at[slot]).start(priority=k)

    @pl.when(i == last)
    def _():
        @pl.when(i >= 1)
        def _():
            retire(1 - slot)
        retire(slot)


def _scatter(gap_lo, gap_hi, n_used, pos, xp, slots):
    rows, half = xp.shape
    tm = ROW_TILE
    assert SPARE_ROWS % EXPERT_ROWS == 0 and slots % EXPERT_ROWS == 0
    return pl.pallas_call(
        _scatter_kernel,
        grid_spec=pltpu.PrefetchScalarGridSpec(
            num_scalar_prefetch=4,
            grid=(rows // tm,),
            in_specs=[pl.BlockSpec((tm, half), lambda i, *_: (i, 0))],
            out_specs=pl.BlockSpec(memory_space=pl.ANY),
            scratch_shapes=[pltpu.VMEM((2, tm, half), U32), pltpu.VMEM((EXPERT_ROWS, half), U32),
                            pltpu.SemaphoreType.DMA((2,)), pltpu.SemaphoreType.DMA(())],
        ),
        out_shape=jax.ShapeDtypeStruct((slots + SPARE_ROWS, half), U32),
        compiler_params=_params(("arbitrary",)),
        name="scatter",
    )(gap_lo, gap_hi, n_used, pos, xp)


def _pack_halves(x):
    n = x.shape[1] // 2
    r = x.astype(BF16).astype(F32)
    lo = lax.bitcast_convert_type(r[:, :n], U32)
    hi = lax.bitcast_convert_type(r[:, n:], U32)
    return (hi & jnp.uint32(0xFFFF0000)) | (lo >> 16)


def _unpack_halves(p):
    return (lax.bitcast_convert_type(p << 16, F32),
            lax.bitcast_convert_type(p & jnp.uint32(0xFFFF0000), F32))


def _expert_kernel(be_ref, nrow_ref, last_ref, ord_ref, used_ref, xs_ref, wg_hbm, wu_hbm, wd_hbm, ys_ref,
                   wg_f, wu_f, wd_f, sem):
    b = pl.program_id(0)
    n_valid_rows = nrow_ref[b]
    expert = be_ref[b]
    ordinal = ord_ref[b]
    n_slots = wg_f.shape[0]
    slot = ordinal % n_slots

    def weight_copies(e, s):
        return [pltpu.make_async_copy(hbm.at[e], buf.at[s], sem.at[s, n])
                for n, (hbm, buf) in enumerate(((wg_hbm, wg_f), (wu_hbm, wu_f), (wd_hbm, wd_f)))]

    def start_fetch(o):
        e = used_ref[o]

        @pl.when(e >= 0)
        def _():
            for c in weight_copies(e, o % n_slots):
                c.start()

    @pl.when(b == 0)
    def _():
        for o in range(WEIGHT_LOOKAHEAD):
            start_fetch(o)

    @pl.when((b == 0) | (expert != be_ref[jnp.maximum(b - 1, 0)]))
    def _():
        start_fetch(ordinal + WEIGHT_LOOKAHEAD)
        for c in weight_copies(expert, slot):
            c.wait()

    @pl.when(n_valid_rows == 0)
    def _():
        ys_ref[...] = jnp.zeros_like(ys_ref)

    @pl.when(n_valid_rows > 0)
    def _():
        lo, hi = _unpack_halves(xs_ref[...])
        x = jnp.concatenate([lo, hi], axis=1).astype(BF16)
        gate = jnp.dot(x, wg_f[slot].astype(BF16), preferred_element_type=F32)
        up = jnp.dot(x, wu_f[slot].astype(BF16), preferred_element_type=F32)
        hb = (jax.nn.silu(gate) * up).astype(BF16)
        ys_ref[...] = _pack_halves(jnp.dot(hb, wd_f[slot].astype(BF16), preferred_element_type=F32))


def _experts(block_expert, block_rows, last_block, block_ord, used_experts, xs, w_gate, w_up, w_down):
    half = xs.shape[1]
    n_exp, d, de = w_gate.shape
    bm = EXPERT_ROWS
    n_blocks = block_expert.shape[0]
    slots = n_blocks * bm
    n_slots = WEIGHT_LOOKAHEAD + 1
    any_space = pl.BlockSpec(memory_space=pl.ANY)
    return pl.pallas_call(
        _expert_kernel,
        grid_spec=pltpu.PrefetchScalarGridSpec(
            num_scalar_prefetch=5,
            grid=(n_blocks,),
            in_specs=[pl.BlockSpec((bm, half), lambda b, be, nr, last, *_: (jnp.minimum(b, last[0]), 0)),
                      any_space, any_space, any_space],
            out_specs=pl.BlockSpec((bm, half), lambda b, *_: (b, 0)),
            scratch_shapes=[pltpu.VMEM((n_slots, d, de), F32), pltpu.VMEM((n_slots, d, de), F32),
                            pltpu.VMEM((n_slots, de, d), F32),
                            pltpu.SemaphoreType.DMA((n_slots, 3))],
        ),
        out_shape=jax.ShapeDtypeStruct((slots, half), U32),
        compiler_params=_params(("arbitrary",)),
        name="experts",
    )(block_expert, block_rows, last_block, block_ord, used_experts, xs, w_gate, w_up, w_down)


def _combine_kernel(nb, per_batch, pos_ref, h2_ref, route_ref, g_ref, ys_ref, o_ref, y_ref, sem):
    t = pl.program_id(0)
    tm = h2_ref.shape[0]

    def issue(tile, slot):
        base = ((tile // per_batch) * nb + 1 + tile % per_batch) * tm

        for r in range(tm):
            for k in range(2):
                p = pos_ref[(base + r) * 2 + k]
                pltpu.make_async_copy(ys_ref.at[pl.ds(p, 1), :], y_ref.at[slot, k, pl.ds(r, 1), :],
                                      sem.at[slot]).start(priority=k)

    @pl.when(t == 0)
    def _():
        issue(0, 0)

    @pl.when(t + 1 < pl.num_programs(0))
    def _():
        issue(t + 1, (t + 1) % 2)

    slot = t % 2
    for k in range(2):
        pltpu.make_async_copy(ys_ref.at[pl.ds(0, tm), :], y_ref.at[slot, k], sem.at[slot]).wait()
    route = route_ref[...]
    w0 = route[:, COL_W0:COL_W0 + 1]
    w1 = route[:, COL_W1:COL_W1 + 1]
    half = y_ref.shape[-1]
    y0 = _unpack_halves(y_ref[slot, 0])
    y1 = _unpack_halves(y_ref[slot, 1])
    hs = [h2_ref[:, s * half:(s + 1) * half] + (w0 * y0[s] + w1 * y1[s]) for s in range(2)]
    ms = sum(jnp.sum(h * h, axis=-1, keepdims=True) for h in hs) / (2 * half)
    inv = lax.rsqrt(ms + RMS_EPS)
    for s in range(2):
        o_ref[0, :, s * half:(s + 1) * half] = hs[s] * inv * g_ref[:, s * half:(s + 1) * half]


def _combine(pos, h2, route, g, ys, batch, nb, seq):
    rows, d = h2.shape
    tm = BLOCK
    per_batch = seq // tm
    tile = lambda t, pos: ((t // per_batch) * nb + 1 + t % per_batch, 0)
    return pl.pallas_call(
        functools.partial(_combine_kernel, nb, per_batch),
        grid_spec=pltpu.PrefetchScalarGridSpec(
            num_scalar_prefetch=1,
            grid=(batch * per_batch,),
            in_specs=[pl.BlockSpec((tm, d), tile),
                      pl.BlockSpec((tm, LANES), tile),
                      pl.BlockSpec((1, d), lambda t, pos: (0, 0)),
                      pl.BlockSpec(memory_space=pl.ANY)],
            out_specs=pl.BlockSpec((1, tm, d), lambda t, pos: (t // per_batch, t % per_batch, 0)),
            scratch_shapes=[pltpu.VMEM((2, 2, tm, ys.shape[1]), U32), pltpu.SemaphoreType.DMA((2,))],
        ),
        out_shape=jax.ShapeDtypeStruct((batch, seq, d), F32),
        compiler_params=_params(("arbitrary",)),
        name="combine",
    )(pos, h2, route, g, ys)


def kernel(x, meta_tokens, rel_bias, norm_mix, w_in, attn_sinks, pool_mix, pool_scale,
           w_attn_branch, w_pool_branch, w_out, norm_ffn, w_router_group, b_router_group,
           w_router_expert, b_router_expert, w_gate, w_up, w_down, norm_final):
    batch, seq, d = x.shape
    depth = w_in.shape[0]
    assert depth == 1, "single-layer stack"
    aw = w_attn_branch.shape[1]
    pw = w_pool_branch.shape[1]
    kvw = 2 * N_KV_HEADS * HEAD_DIM
    assert aw == N_Q_HEADS * HEAD_DIM and w_in.shape[2] == aw + kvw + pw + 2 * d
    assert seq % BLOCK == 0
    lp = seq + BLOCK
    nb = lp // BLOCK
    rows = batch * lp
    assert rows % ROW_TILE == 0

    assert ROW_TILE == 2 * BLOCK
    lead = jnp.concatenate([jnp.zeros((PAD, d), x.dtype), meta_tokens.astype(x.dtype)], axis=0)
    x2 = x.reshape(batch * seq, d)

    q, kv, u, ga, gp = _inproj(lead, x2, nb, norm_mix[0][None], w_in[0].astype(BF16), (aw, kvw, pw, d, d))

    attn = _attention(q, kv, _attn_bias_tables(rel_bias), attn_sinks[0].astype(F32), batch, nb)

    n_router = N_GROUPS + N_EXPERTS
    wr = jnp.concatenate([w_router_group[0], w_router_expert[0], jnp.zeros((d, LANES - n_router), F32)], axis=1)
    br = jnp.concatenate([b_router_group[0], b_router_expert[0], jnp.zeros((LANES - n_router,), F32)])[None]
    wr_hi = wr.astype(BF16)
    wr_split = jnp.concatenate([wr_hi, (wr - wr_hi.astype(F32)).astype(BF16)], axis=1)
    h2, xp, route, counts = _mixer(
        lp, attn, u, ga, gp, lead, x2, pool_mix[0].astype(BF16), pool_scale[0][None].astype(F32),
        w_attn_branch[0].astype(BF16), w_pool_branch[0].astype(BF16), w_out[0].astype(BF16),
        norm_ffn[0][None], wr_split, br)

    bm = EXPERT_ROWS
    n_tok = batch * (seq + N_META)
    n_blocks = (2 * n_tok) // bm + N_EXPERTS
    cnt = counts[0, :N_EXPERTS].astype(I32)
    blocks_e = (cnt + bm - 1) // bm
    bend = jnp.cumsum(blocks_e)
    bstart = bend - blocks_e
    ord_e = jnp.cumsum((blocks_e > 0).astype(I32)) - 1
    n_used = bend[-1]
    last_block = jnp.maximum(n_used - 1, 0)
    blk = jnp.arange(n_blocks, dtype=I32)

    def expert_of(block):
        return jnp.minimum(jnp.sum((bend[None, :] <= block[:, None]).astype(I32), axis=1), N_EXPERTS - 1)

    block_expert = expert_of(jnp.minimum(blk, last_block))
    own = block_expert[:, None] == jnp.arange(N_EXPERTS, dtype=I32)[None, :]
    pick = lambda v: jnp.sum(jnp.where(own, v[None, :], 0), axis=1)
    block_rows = jnp.clip(pick(cnt) - (blk - pick(bstart)) * bm, 0, bm)
    block_rows = jnp.where(blk < n_used, block_rows, 0).astype(I32)
    block_ord = pick(ord_e).astype(I32)
    experts = jnp.arange(N_EXPERTS, dtype=I32)
    is_kth = (ord_e[None, :] == jnp.arange(N_EXPERTS + WEIGHT_LOOKAHEAD, dtype=I32)[:, None]) & (blocks_e > 0)[None, :]
    used_experts = jnp.where(jnp.any(is_kth, axis=1), jnp.sum(jnp.where(is_kth, experts[None, :], 0), axis=1), -1)
    gap_lo = (bstart * bm + cnt).astype(I32)
    gap_hi = (bend * bm).astype(I32)
    e = route[:, COL_E0:COL_E1 + 1].astype(I32)
    rank = route[:, COL_R0:COL_R1 + 1].astype(I32)
    first_row = jnp.sum(jnp.where(e[..., None] == jnp.arange(N_EXPERTS, dtype=I32), bstart * bm, 0), axis=-1)
    spare = n_blocks * bm + jnp.arange(2 * rows, dtype=I32).reshape(rows, 2) % SPARE_ROWS
    pos = jnp.where(e >= 0, first_row + rank, spare).astype(I32).reshape(-1)

    xs = _scatter(gap_lo, gap_hi, n_used.astype(I32).reshape(1), pos, xp, n_blocks * bm)
    ys = _experts(block_expert.astype(I32), block_rows, last_block.astype(I32).reshape(1), block_ord,
                  used_experts.astype(I32), xs, w_gate[0], w_up[0], w_down[0])
    return _combine(pos, h2, route, norm_final[None].astype(F32), ys, batch, nb, seq)
```

```python
import functools
import math

import numpy as np
import jax
import jax.numpy as jnp
from jax import lax
from jax.experimental import pallas as pl
from jax.experimental.pallas import tpu as pltpu

F32 = jnp.float32
BF16 = jnp.bfloat16
I32 = jnp.int32
U32 = jnp.uint32

BLOCK = 128
N_META = 16
PAD = BLOCK - N_META
HEAD_DIM = 64
N_KV_HEADS = 2
Q_PER_KV = 8
N_Q_HEADS = N_KV_HEADS * Q_PER_KV
WINDOW = 128
POOL_WINDOWS = (2, 4, 8, 16)
POOL_HALO = 16
N_BUCKETS = 32
MAX_DISTANCE = 128
N_GROUPS = 8
EXPERTS_PER_GROUP = 8
N_EXPERTS = N_GROUPS * EXPERTS_PER_GROUP
RMS_EPS = 1e-6
LANES = 128
VMEM_LIMIT = 56 * 1024 * 1024

ROW_TILE = 256
EXPERT_ROWS = 256
WEIGHT_LOOKAHEAD = 2
SPARE_ROWS = 4 * ROW_TILE
COL_E0, COL_E1, COL_W0, COL_W1, COL_R0, COL_R1 = 0, 1, 2, 3, 4, 5
ROUTE_ROWS = 8


def _params(sem, vmem=VMEM_LIMIT):
    return pltpu.CompilerParams(dimension_semantics=sem, vmem_limit_bytes=vmem)


def _resident(shape):
    nd = len(shape)
    return pl.BlockSpec(shape, lambda *_: (0,) * nd, pipeline_mode=pl.Buffered(1))


def _tile_block_specs(nb, per_batch, d):
    def spec(half):
        def index(i, *_):
            g = 2 * i + half
            return ((g // nb) * per_batch + jnp.maximum(g % nb - 1, 0), 0)
        return pl.BlockSpec((BLOCK, d), index)
    return [spec(0), spec(1)]


def _tile_rows(nb, lead_ref, xa_ref, xb_ref):
    i = pl.program_id(0)
    halves = [jnp.where((2 * i + half) % nb == 0, lead_ref[...], ref[...])
              for half, ref in enumerate((xa_ref, xb_ref))]
    return jnp.concatenate(halves, axis=0)


def _inproj_kernel(nb, lead_ref, xa_ref, xb_ref, g_ref, w_ref, q_ref, kv_ref, u_ref, ga_ref, gp_ref):
    x = _tile_rows(nb, lead_ref, xa_ref, xb_ref)
    ms = jnp.mean(x * x, axis=-1, keepdims=True)
    hn = (x * lax.rsqrt(ms + RMS_EPS) * g_ref[...]).astype(BF16)
    off = 0
    for ref in (q_ref, kv_ref, u_ref, ga_ref, gp_ref):
        width = ref.shape[1]
        for c in range(0, width, 1024):
            cw = min(1024, width - c)
            ref[:, c:c + cw] = jnp.dot(hn, w_ref[:, off + c:off + c + cw],
                                       preferred_element_type=F32).astype(BF16)
        off += width


def _inproj(lead, x2, nb, g, w, widths):
    d = x2.shape[1]
    tm = ROW_TILE
    per_batch = nb - 1
    rows = x2.shape[0] // per_batch * nb
    outs = [jax.ShapeDtypeStruct((rows, wd), BF16) for wd in widths]
    return pl.pallas_call(
        functools.partial(_inproj_kernel, nb),
        grid=(rows // tm,),
        in_specs=[_resident(lead.shape)] + _tile_block_specs(nb, per_batch, d)
                 + [_resident((1, d)), _resident(w.shape)],
        out_specs=[pl.BlockSpec((tm, wd), lambda i: (i, 0)) for wd in widths],
        out_shape=outs,
        compiler_params=_params(("parallel",)),
        name="inproj",
    )(lead, x2, x2, g, w)


PAIR = 2 * HEAD_DIM
KEYS = 3 * BLOCK
N_PAIRS = N_Q_HEADS // 2


def _attn_kernel(sink_ref, q_ref, kvp_ref, kvc_ref, kvm_ref, bias_ref, o_ref):
    kw = N_KV_HEADS * HEAD_DIM
    zeros = jnp.zeros((KEYS, HEAD_DIM), BF16)
    ones = jnp.ones((KEYS, HEAD_DIM), BF16)
    lane = lax.broadcasted_iota(I32, (BLOCK, PAIR), 1)
    scale = jnp.asarray(HEAD_DIM ** -0.5, BF16)
    for hk in range(N_KV_HEADS):
        ks = slice(hk * HEAD_DIM, (hk + 1) * HEAD_DIM)
        vs = slice(kw + hk * HEAD_DIM, kw + (hk + 1) * HEAD_DIM)
        k3 = jnp.concatenate([kvp_ref[:, ks], kvc_ref[:, ks], kvm_ref[:, ks]], axis=0) * scale
        v3 = jnp.concatenate([kvp_ref[:, vs], kvc_ref[:, vs], kvm_ref[:, vs]], axis=0)
        kbd = jnp.concatenate([jnp.concatenate([k3, zeros], axis=1),
                               jnp.concatenate([zeros, k3], axis=1)], axis=0)
        vext = jnp.concatenate([jnp.concatenate([v3, zeros, ones, zeros], axis=1),
                                jnp.concatenate([zeros, v3, zeros, ones], axis=1)], axis=0)
        for jp in range(Q_PER_KV // 2):
            j = hk * (Q_PER_KV // 2) + jp
            qp = q_ref[:, j * PAIR:(j + 1) * PAIR]
            s = lax.dot_general(qp, kbd, (((1,), (1,)), ((), ())),
                                preferred_element_type=F32) + bias_ref[0, j]
            probs, sink_terms = [], []
            for side in range(2):
                ss = s[:, side * KEYS:(side + 1) * KEYS]
                sink = sink_ref[2 * j + side]
                mx = jnp.maximum(jnp.max(ss, axis=-1, keepdims=True), sink)
                probs.append(jnp.exp(ss - mx))
                sink_terms.append(jnp.exp(sink - mx))
            p = jnp.concatenate(probs, axis=1).astype(BF16)
            r = jnp.dot(p, vext, preferred_element_type=F32)
            den = r[:, PAIR:] + jnp.where(lane < HEAD_DIM, sink_terms[0], sink_terms[1])
            o_ref[:, j * PAIR:(j + 1) * PAIR] = (r[:, :PAIR] / den).astype(o_ref.dtype)


def _attention(q, kv, bias, sinks, batch, nb):
    rows, aw = q.shape
    kvw = kv.shape[1]
    return pl.pallas_call(
        _attn_kernel,
        grid=(batch, nb),
        in_specs=[pl.BlockSpec(memory_space=pltpu.SMEM),
                  pl.BlockSpec((BLOCK, aw), lambda b, n: (b * nb + n, 0)),
                  pl.BlockSpec((BLOCK, kvw), lambda b, n: (b * nb + jnp.maximum(n - 1, 0), 0)),
                  pl.BlockSpec((BLOCK, kvw), lambda b, n: (b * nb + n, 0)),
                  pl.BlockSpec((BLOCK, kvw), lambda b, n: (b * nb, 0)),
                  pl.BlockSpec((1,) + bias.shape[1:], lambda b, n: (jnp.minimum(n, 2), 0, 0, 0))],
        out_specs=pl.BlockSpec((BLOCK, aw), lambda b, n: (b * nb + n, 0)),
        out_shape=jax.ShapeDtypeStruct((rows, aw), BF16),
        compiler_params=_params(("parallel", "arbitrary")),
        name="attention",
    )(sinks, q, kv, kv, kv, bias)


def _attn_bias_tables(rel_bias):
    max_exact = N_BUCKETS // 2
    qi = np.arange(BLOCK)[:, None]
    kj = np.arange(2 * BLOCK)[None, :]
    mj = np.arange(BLOCK)[None, :]
    dist = qi + BLOCK - kj
    d = np.maximum(dist, 0)
    large = max_exact + (np.log(np.maximum(d, 1).astype(np.float32) / max_exact)
                         / math.log(MAX_DISTANCE / max_exact) * (N_BUCKETS - max_exact)).astype(np.int32)
    bucket = np.where(d < max_exact, d, np.minimum(large, N_BUCKETS - 1))
    onehot = (bucket[..., None] == np.arange(N_BUCKETS)).astype(np.float32)
    rb = rel_bias.astype(F32)
    win = jnp.einsum("qkb,bh->hqk", onehot, rb, precision=lax.Precision.HIGHEST)
    meta = jnp.broadcast_to(rb[N_BUCKETS - 1][:, None, None], (N_Q_HEADS, BLOCK, BLOCK))
    full = jnp.concatenate([win, meta], axis=2)
    masks = []
    for n in range(3):
        win_ok = (dist >= 0) & (dist < WINDOW) & ((n - 1) * BLOCK + kj >= PAD)
        meta_ok = (mj >= PAD) & (n * BLOCK + qi - mj >= WINDOW)
        masks.append(np.concatenate([win_ok, meta_ok], axis=1))
    t = jnp.where(np.stack(masks)[:, None], full[None], -jnp.inf)
    t = t.reshape(3, N_PAIRS, 2, BLOCK, KEYS).transpose(0, 1, 3, 2, 4)
    return t.reshape(3, N_PAIRS, BLOCK, 2 * KEYS)


MIX_CHUNKS = 4


def _interleave(pattern, **streams):
    for key in pattern:
        next(streams[key], None)
    for s in streams.values():
        for _ in s:
            pass


def _mixer_kernel(lp, n_tiles, attn_ref, u_ref, uh_ref, ga_ref, gp_ref, lead_ref, xa_ref, xb_ref,
                  pmix_ref, pscale_ref, wa_ref, wp_ref, wo_ref, g2_ref, wr_ref, br_ref,
                  h2_ref, xp_ref, route_ref, route_t_ref, cnt_ref, run_ref, h2_s, a_s, p_s, m_s):
    i = pl.program_id(0)
    tm = h2_ref.shape[0]
    d = h2_ref.shape[1]
    cw = d // MIX_CHUNKS

    @pl.when(i == 0)
    def _():
        run_ref[...] = jnp.zeros_like(run_ref)
        h2_s[...] = jnp.zeros_like(h2_s)

    def chain():
        ic = jnp.minimum(i, n_tiles - 1)
        t = (ic * tm + lax.broadcasted_iota(I32, (tm, 1), 0)) % lp
        valid = t >= PAD
        tx = (ic * tm - POOL_HALO + lp + lax.broadcasted_iota(I32, (tm + POOL_HALO, 1), 0)) % lp
        n_rows = (t - PAD + 1).astype(F32)
        gw = u_ref.shape[1] // len(POOL_WINDOWS)
        attn = attn_ref[...]
        aw = d // len(POOL_WINDOWS)
        for gi, w in enumerate(POOL_WINDOWS):
            a_s[:, gi * aw:(gi + 1) * aw] = jnp.dot(attn, wa_ref[:, gi * aw:(gi + 1) * aw],
                                                    preferred_element_type=F32)
            cols = slice(gi * gw, (gi + 1) * gw)
            uext = jnp.concatenate([uh_ref[:, cols], u_ref[:, cols]], axis=0).astype(F32)
            c = jnp.where(tx >= PAD, uext, 0.0)
            s, span = c, 1
            while span < w:
                s = s[span:] + s[:-span]
                span *= 2
            win = s[POOL_HALO + 1 - w:POOL_HALO + 1 - w + tm]
            n_valid = jnp.clip(n_rows, 1.0, float(w))
            mixed = jnp.where(valid, win / n_valid - c[POOL_HALO:], 0.0)
            m_s[:, cols] = (jnp.dot(mixed.astype(BF16), pmix_ref[gi], preferred_element_type=F32)
                            * pscale_ref[:, cols]).astype(BF16)
            yield
        pool = m_s[:, :u_ref.shape[1]]
        for c in range(MIX_CHUNKS):
            cols = slice(c * cw, (c + 1) * cw)
            p_s[:, cols] = jnp.dot(pool, wp_ref[:, cols], preferred_element_type=F32)
            yield
        nb = lp // BLOCK
        halves = [jnp.where((2 * ic + hf) % nb == 0, lead_ref[...], ref[...])
                  for hf, ref in enumerate((xa_ref, xb_ref))]
        h2 = jnp.concatenate(halves, axis=0)
        for c in range(MIX_CHUNKS):
            cols = slice(c * cw, (c + 1) * cw)
            merged = (jax.nn.sigmoid(ga_ref[:, cols].astype(F32)) * a_s[:, cols]
                      + jax.nn.sigmoid(gp_ref[:, cols].astype(F32)) * p_s[:, cols]).astype(BF16)
            yield
            h2 = h2 + jnp.dot(merged, wo_ref[cols, :], preferred_element_type=F32)
            yield
        h2_ref[...] = h2
        h2_s[...] = h2
        yield

    def tail():
        h2p = h2_s[...]
        tp = ((i - 1) * tm + lp + lax.broadcasted_iota(I32, (tm, 1), 0)) % lp
        valid_p = (tp >= PAD) & (i >= 1)
        ms = jnp.mean(h2p * h2p, axis=-1, keepdims=True)
        inv = lax.rsqrt(ms + RMS_EPS)
        yield
        half = d // 2
        hi_prod = jnp.zeros((tm, 2 * LANES), F32)
        lo_prod = jnp.zeros((tm, LANES), F32)
        n_pack = MIX_CHUNKS // 2
        pw = half // n_pack
        for c in range(n_pack):
            parts = []
            for base in (c * pw, half + c * pw):
                hn = h2p[:, base:base + pw] * inv * g2_ref[:, base:base + pw]
                x_hi = hn.astype(BF16)
                x_hi32 = x_hi.astype(F32)
                x_lo = (hn - x_hi32).astype(BF16)
                hi_prod = hi_prod + jnp.dot(x_hi, wr_ref[base:base + pw, :], preferred_element_type=F32)
                lo_prod = lo_prod + jnp.dot(x_lo, wr_ref[base:base + pw, :LANES], preferred_element_type=F32)
                parts.append(lax.bitcast_convert_type(x_hi32, U32))
            xp_ref[:, c * pw:(c + 1) * pw] = (parts[1] & jnp.uint32(0xFFFF0000)) | (parts[0] >> 16)
            yield
        logits = hi_prod[:, :LANES] + (hi_prod[:, LANES:] + lo_prod) + br_ref[...]
        col = lax.broadcasted_iota(I32, logits.shape, 1).astype(F32)
        neg = -jnp.inf
        gl = jnp.where(col < N_GROUPS, logits, neg)
        gmax = jnp.max(gl, axis=-1, keepdims=True)
        grp = jnp.min(jnp.where(gl == gmax, col, float(LANES)), axis=-1, keepdims=True)
        p_grp = 1.0 / jnp.sum(jnp.exp(gl - gmax), axis=-1, keepdims=True)
        yield
        e_lo = N_GROUPS + grp * EXPERTS_PER_GROUP
        el = jnp.where((col >= e_lo) & (col < e_lo + EXPERTS_PER_GROUP), logits, neg)
        m1 = jnp.max(el, axis=-1, keepdims=True)
        i1 = jnp.min(jnp.where(el == m1, col, float(LANES)), axis=-1, keepdims=True)
        el2 = jnp.where(col == i1, neg, el)
        m2 = jnp.max(el2, axis=-1, keepdims=True)
        i2 = jnp.min(jnp.where(el2 == m2, col, float(LANES)), axis=-1, keepdims=True)
        z = jnp.exp(m2 - m1)
        w1 = p_grp / (1.0 + z)
        w2 = p_grp * z / (1.0 + z)
        e1 = i1 - N_GROUPS
        e2 = i2 - N_GROUPS
        yield
        oh1 = jnp.where((col == e1) & valid_p, 1.0, 0.0)
        oh2 = jnp.where((col == e2) & valid_p, 1.0, 0.0)
        lower = (lax.broadcasted_iota(I32, (tm, tm), 0) > lax.broadcasted_iota(I32, (tm, tm), 1))
        lower = jnp.where(lower, 1.0, 0.0).astype(BF16)
        before1 = jnp.dot(lower, oh1.astype(BF16), preferred_element_type=F32)
        before2 = jnp.dot(lower, oh2.astype(BF16), preferred_element_type=F32)
        tot1 = jnp.sum(oh1, axis=0, keepdims=True)
        tot2 = jnp.sum(oh2, axis=0, keepdims=True)
        run = run_ref[...]
        r1 = jnp.sum(oh1 * (run + before1), axis=-1, keepdims=True)
        r2 = jnp.sum(oh2 * (run + tot1 + before2), axis=-1, keepdims=True)
        run = run + tot1 + tot2
        run_ref[...] = run
        cnt_ref[...] = run
        yield
        slab = jnp.zeros(logits.shape, F32)
        for cidx, val in ((COL_E0, jnp.where(valid_p, e1, -1.0)),
                          (COL_E1, jnp.where(valid_p, e2, -1.0)),
                          (COL_W0, w1), (COL_W1, w2), (COL_R0, r1), (COL_R1, r2)):
            slab = jnp.where(col == cidx, val, slab)
        route_ref[...] = slab
        route_t_ref[...] = slab.T[:ROUTE_ROWS, :]
        yield

    _interleave("cccc" + "ctctctct" + "cctcctcctcc", c=chain(), t=tail())


def _mixer(lp, attn, u, ga, gp, lead, x2, pmix, pscale, wa, wp, wo, g2, wr, br):
    rows = attn.shape[0]
    d = x2.shape[1]
    tm = ROW_TILE
    nb = lp // BLOCK
    n_tiles = rows // tm
    halo_blocks = tm // POOL_HALO
    cur = lambda i: (jnp.minimum(i, n_tiles - 1), 0)
    prev = lambda i: (jnp.maximum(i - 1, 0), 0)

    def x_spec(half):
        def index(i):
            g = 2 * jnp.minimum(i, n_tiles - 1) + half
            return ((g // nb) * (nb - 1) + jnp.maximum(g % nb - 1, 0), 0)
        return pl.BlockSpec((BLOCK, d), index)

    return pl.pallas_call(
        functools.partial(_mixer_kernel, lp, n_tiles),
        grid=(n_tiles + 1,),
        in_specs=[pl.BlockSpec((tm, attn.shape[1]), cur),
                  pl.BlockSpec((tm, u.shape[1]), cur),
                  pl.BlockSpec((POOL_HALO, u.shape[1]),
                               lambda i: (jnp.maximum(jnp.minimum(i, n_tiles - 1) * halo_blocks - 1, 0), 0)),
                  pl.BlockSpec((tm, d), cur),
                  pl.BlockSpec((tm, d), cur),
                  _resident(lead.shape), x_spec(0), x_spec(1),
                  _resident(pmix.shape), _resident(pscale.shape), _resident(wa.shape),
                  _resident(wp.shape), _resident(wo.shape), _resident(g2.shape),
                  _resident(wr.shape), _resident(br.shape)],
        out_specs=[pl.BlockSpec((tm, d), cur),
                   pl.BlockSpec((tm, d // 2), prev),
                   pl.BlockSpec((tm, LANES), prev),
                   pl.BlockSpec((ROUTE_ROWS, tm), lambda i: (0, jnp.maximum(i - 1, 0))),
                   pl.BlockSpec((1, LANES), lambda i: (0, 0))],
        out_shape=[jax.ShapeDtypeStruct((rows, d), F32),
                   jax.ShapeDtypeStruct((rows, d // 2), U32),
                   jax.ShapeDtypeStruct((rows, LANES), F32),
                   jax.ShapeDtypeStruct((ROUTE_ROWS, rows), F32),
                   jax.ShapeDtypeStruct((1, LANES), F32)],
        scratch_shapes=[pltpu.VMEM((1, LANES), F32), pltpu.VMEM((tm, d), F32),
                        pltpu.VMEM((tm, d), F32), pltpu.VMEM((tm, d), F32), pltpu.VMEM((tm, d), BF16)],
        compiler_params=_params(("arbitrary",)),
        name="mixer",
    )(attn, u, u, ga, gp, lead, x2, x2, pmix, pscale, wa, wp, wo, g2, wr, br)


def _zero_unused_slots(gap_lo_ref, gap_hi_ref, used_ref, xs_ref, zero_ref, sem):
    bm = zero_ref.shape[0]
    n_rows = xs_ref.shape[0]
    n_blocks = (n_rows - SPARE_ROWS) // bm
    zero_ref[...] = jnp.zeros_like(zero_ref)

    def piece(start, size):
        return pltpu.make_async_copy(zero_ref.at[pl.ds(0, size), :], xs_ref.at[pl.ds(start, size), :], sem)

    def for_each_piece(op):
        def tail(e, carry):
            lo, hi = gap_lo_ref[e], gap_hi_ref[e]
            length = hi - lo
            for bit in range(int(math.log2(bm)) - 1, 2, -1):
                size = 1 << bit
                above = (length >> (bit + 1)) << (bit + 1)

                @pl.when((length >> bit) & 1 == 1)
                def _():
                    op(piece(pl.multiple_of(hi - above - size, 8), size))
            for j in range(7):
                @pl.when(j < (length & 7))
                def _():
                    op(piece(lo + j, 1))
            return carry

        def block(b, carry):
            op(piece(pl.multiple_of(b * bm, bm), bm))
            return carry

        lax.fori_loop(0, N_EXPERTS, tail, 0)
        lax.fori_loop(used_ref[0], n_blocks + SPARE_ROWS // bm, block, 0)

    for_each_piece(lambda c: c.start())
    for_each_piece(lambda c: c.wait())


def _scatter_kernel(gap_lo_ref, gap_hi_ref, used_ref, pos0_ref, pos1_ref, x_ref, xs_ref, stage_ref, zero_ref,
                    sem, zsem):
    i = pl.program_id(0)
    last = pl.num_programs(0) - 1
    tm = x_ref.shape[0]
    slot = i % 2

    @pl.when(i == 0)
    def _():
        _zero_unused_slots(gap_lo_ref, gap_hi_ref, used_ref, xs_ref, zero_ref, zsem)

    def retire(s):
        for k in range(2):
            pltpu.make_async_copy(stage_ref.at[s], xs_ref.at[pl.ds(0, tm), :], sem.at[s]).wait()

    @pl.when(i >= 2)
    def _():
        retire(slot)

    stage_ref[slot] = x_ref[...]

    for r in range(tm):
        for k in range(2):
            p = (pos0_ref, pos1_ref)[k][i * tm + r]
            pltpu.make_async_copy(stage_ref.at[slot, pl.ds(r, 1), :], xs_ref.at[pl.ds(p, 1), :],
                                  sem.at[slot]).start(priority=k)

    @pl.when(i == last)
    def _():
        @pl.when(i >= 1)
        def _():
            retire(1 - slot)
        retire(slot)


def _scatter(gap_lo, gap_hi, n_used, pos, xp, slots):
    rows, half = xp.shape
    tm = ROW_TILE
    assert SPARE_ROWS % EXPERT_ROWS == 0 and slots % EXPERT_ROWS == 0
    return pl.pallas_call(
        _scatter_kernel,
        grid_spec=pltpu.PrefetchScalarGridSpec(
            num_scalar_prefetch=5,
            grid=(rows // tm,),
            in_specs=[pl.BlockSpec((tm, half), lambda i, *_: (i, 0))],
            out_specs=pl.BlockSpec(memory_space=pl.ANY),
            scratch_shapes=[pltpu.VMEM((2, tm, half), U32), pltpu.VMEM((EXPERT_ROWS, half), U32),
                            pltpu.SemaphoreType.DMA((2,)), pltpu.SemaphoreType.DMA(())],
        ),
        out_shape=jax.ShapeDtypeStruct((slots + SPARE_ROWS, half), U32),
        compiler_params=_params(("arbitrary",)),
        name="scatter",
    )(gap_lo, gap_hi, n_used, pos[0], pos[1], xp)


def _pack_halves(x):
    n = x.shape[1] // 2
    r = x.astype(BF16).astype(F32)
    lo = lax.bitcast_convert_type(r[:, :n], U32)
    hi = lax.bitcast_convert_type(r[:, n:], U32)
    return (hi & jnp.uint32(0xFFFF0000)) | (lo >> 16)


def _unpack_halves(p):
    return (lax.bitcast_convert_type(p << 16, F32),
            lax.bitcast_convert_type(p & jnp.uint32(0xFFFF0000), F32))


def _expert_kernel(be_ref, nrow_ref, last_ref, ord_ref, used_ref, xs_ref, wg_hbm, wu_hbm, wd_hbm, ys_ref,
                   wg_f, wu_f, wd_f, sem):
    b = pl.program_id(0)
    n_valid_rows = nrow_ref[b]
    expert = be_ref[b]
    ordinal = ord_ref[b]
    n_slots = wg_f.shape[0]
    slot = ordinal % n_slots

    def weight_copies(e, s):
        return [pltpu.make_async_copy(hbm.at[e], buf.at[s], sem.at[s, n])
                for n, (hbm, buf) in enumerate(((wg_hbm, wg_f), (wu_hbm, wu_f), (wd_hbm, wd_f)))]

    def start_fetch(o):
        e = used_ref[o]

        @pl.when(e >= 0)
        def _():
            for c in weight_copies(e, o % n_slots):
                c.start()

    @pl.when(b == 0)
    def _():
        for o in range(WEIGHT_LOOKAHEAD):
            start_fetch(o)

    @pl.when((b == 0) | (expert != be_ref[jnp.maximum(b - 1, 0)]))
    def _():
        start_fetch(ordinal + WEIGHT_LOOKAHEAD)
        for c in weight_copies(expert, slot):
            c.wait()

    @pl.when(n_valid_rows == 0)
    def _():
        ys_ref[...] = jnp.zeros_like(ys_ref)

    @pl.when(n_valid_rows > 0)
    def _():
        lo, hi = _unpack_halves(xs_ref[...])
        x = jnp.concatenate([lo, hi], axis=1).astype(BF16)
        gate = jnp.dot(x, wg_f[slot].astype(BF16), preferred_element_type=F32)
        up = jnp.dot(x, wu_f[slot].astype(BF16), preferred_element_type=F32)
        hb = (jax.nn.silu(gate) * up).astype(BF16)
        ys_ref[...] = _pack_halves(jnp.dot(hb, wd_f[slot].astype(BF16), preferred_element_type=F32))


def _experts(block_expert, block_rows, last_block, block_ord, used_experts, xs, w_gate, w_up, w_down):
    half = xs.shape[1]
    n_exp, d, de = w_gate.shape
    bm = EXPERT_ROWS
    n_blocks = block_expert.shape[0]
    slots = n_blocks * bm
    n_slots = WEIGHT_LOOKAHEAD + 1
    any_space = pl.BlockSpec(memory_space=pl.ANY)
    return pl.pallas_call(
        _expert_kernel,
        grid_spec=pltpu.PrefetchScalarGridSpec(
            num_scalar_prefetch=5,
            grid=(n_blocks,),
            in_specs=[pl.BlockSpec((bm, half), lambda b, be, nr, last, *_: (jnp.minimum(b, last[0]), 0)),
                      any_space, any_space, any_space],
            out_specs=pl.BlockSpec((bm, half), lambda b, *_: (b, 0)),
            scratch_shapes=[pltpu.VMEM((n_slots, d, de), F32), pltpu.VMEM((n_slots, d, de), F32),
                            pltpu.VMEM((n_slots, de, d), F32),
                            pltpu.SemaphoreType.DMA((n_slots, 3))],
        ),
        out_shape=jax.ShapeDtypeStruct((slots, half), U32),
        compiler_params=_params(("arbitrary",)),
        name="experts",
    )(block_expert, block_rows, last_block, block_ord, used_experts, xs, w_gate, w_up, w_down)


def _combine_kernel(nb, per_batch, pos0_ref, pos1_ref, h2_ref, route_ref, g_ref, ys_ref, o_ref, y_ref, sem):
    t = pl.program_id(0)
    tm = h2_ref.shape[0]

    def issue(tile, slot):
        base = ((tile // per_batch) * nb + 1 + tile % per_batch) * tm

        for r in range(tm):
            for k in range(2):
                p = (pos0_ref, pos1_ref)[k][base + r]
                pltpu.make_async_copy(ys_ref.at[pl.ds(p, 1), :], y_ref.at[slot, k, pl.ds(r, 1), :],
                                      sem.at[slot]).start(priority=k)

    @pl.when(t == 0)
    def _():
        issue(0, 0)

    @pl.when(t + 1 < pl.num_programs(0))
    def _():
        issue(t + 1, (t + 1) % 2)

    slot = t % 2
    for k in range(2):
        pltpu.make_async_copy(ys_ref.at[pl.ds(0, tm), :], y_ref.at[slot, k], sem.at[slot]).wait()
    route = route_ref[...]
    w0 = route[:, COL_W0:COL_W0 + 1]
    w1 = route[:, COL_W1:COL_W1 + 1]
    half = y_ref.shape[-1]
    y0 = _unpack_halves(y_ref[slot, 0])
    y1 = _unpack_halves(y_ref[slot, 1])
    hs = [h2_ref[:, s * half:(s + 1) * half] + (w0 * y0[s] + w1 * y1[s]) for s in range(2)]
    ms = sum(jnp.sum(h * h, axis=-1, keepdims=True) for h in hs) / (2 * half)
    inv = lax.rsqrt(ms + RMS_EPS)
    for s in range(2):
        o_ref[0, :, s * half:(s + 1) * half] = hs[s] * inv * g_ref[:, s * half:(s + 1) * half]


def _combine(pos, h2, route, g, ys, batch, nb, seq):
    rows, d = h2.shape
    tm = BLOCK
    per_batch = seq // tm
    tile = lambda t, *_: ((t // per_batch) * nb + 1 + t % per_batch, 0)
    return pl.pallas_call(
        functools.partial(_combine_kernel, nb, per_batch),
        grid_spec=pltpu.PrefetchScalarGridSpec(
            num_scalar_prefetch=2,
            grid=(batch * per_batch,),
            in_specs=[pl.BlockSpec((tm, d), tile),
                      pl.BlockSpec((tm, LANES), tile),
                      pl.BlockSpec((1, d), lambda t, *_: (0, 0)),
                      pl.BlockSpec(memory_space=pl.ANY)],
            out_specs=pl.BlockSpec((1, tm, d), lambda t, *_: (t // per_batch, t % per_batch, 0)),
            scratch_shapes=[pltpu.VMEM((2, 2, tm, ys.shape[1]), U32), pltpu.SemaphoreType.DMA((2,))],
        ),
        out_shape=jax.ShapeDtypeStruct((batch, seq, d), F32),
        compiler_params=_params(("arbitrary",)),
        name="combine",
    )(pos[0], pos[1], h2, route, g, ys)


def kernel(x, meta_tokens, rel_bias, norm_mix, w_in, attn_sinks, pool_mix, pool_scale,
           w_attn_branch, w_pool_branch, w_out, norm_ffn, w_router_group, b_router_group,
           w_router_expert, b_router_expert, w_gate, w_up, w_down, norm_final):
    batch, seq, d = x.shape
    depth = w_in.shape[0]
    assert depth == 1, "single-layer stack"
    aw = w_attn_branch.shape[1]
    pw = w_pool_branch.shape[1]
    kvw = 2 * N_KV_HEADS * HEAD_DIM
    assert aw == N_Q_HEADS * HEAD_DIM and w_in.shape[2] == aw + kvw + pw + 2 * d
    assert seq % BLOCK == 0
    lp = seq + BLOCK
    nb = lp // BLOCK
    rows = batch * lp
    assert rows % ROW_TILE == 0

    assert ROW_TILE == 2 * BLOCK
    lead = jnp.concatenate([jnp.zeros((PAD, d), x.dtype), meta_tokens.astype(x.dtype)], axis=0)
    x2 = x.reshape(batch * seq, d)

    q, kv, u, ga, gp = _inproj(lead, x2, nb, norm_mix[0][None], w_in[0].astype(BF16), (aw, kvw, pw, d, d))

    attn = _attention(q, kv, _attn_bias_tables(rel_bias), attn_sinks[0].astype(F32), batch, nb)

    n_router = N_GROUPS + N_EXPERTS
    wr = jnp.concatenate([w_router_group[0], w_router_expert[0], jnp.zeros((d, LANES - n_router), F32)], axis=1)
    br = jnp.concatenate([b_router_group[0], b_router_expert[0], jnp.zeros((LANES - n_router,), F32)])[None]
    wr_hi = wr.astype(BF16)
    wr_split = jnp.concatenate([wr_hi, (wr - wr_hi.astype(F32)).astype(BF16)], axis=1)
    h2, xp, route, route_t, counts = _mixer(
        lp, attn, u, ga, gp, lead, x2, pool_mix[0].astype(BF16), pool_scale[0][None].astype(F32),
        w_attn_branch[0].astype(BF16), w_pool_branch[0].astype(BF16), w_out[0].astype(BF16),
        norm_ffn[0][None], wr_split, br)

    bm = EXPERT_ROWS
    n_tok = batch * (seq + N_META)
    n_blocks = (2 * n_tok) // bm + N_EXPERTS
    cnt = counts[0, :N_EXPERTS].astype(I32)
    blocks_e = (cnt + bm - 1) // bm
    bend = jnp.cumsum(blocks_e)
    bstart = bend - blocks_e
    ord_e = jnp.cumsum((blocks_e > 0).astype(I32)) - 1
    n_used = bend[-1]
    last_block = jnp.maximum(n_used - 1, 0)
    blk = jnp.arange(n_blocks, dtype=I32)

    def expert_of(block):
        return jnp.minimum(jnp.sum((bend[None, :] <= block[:, None]).astype(I32), axis=1), N_EXPERTS - 1)

    block_expert = expert_of(jnp.minimum(blk, last_block))
    own = block_expert[:, None] == jnp.arange(N_EXPERTS, dtype=I32)[None, :]
    pick = lambda v: jnp.sum(jnp.where(own, v[None, :], 0), axis=1)
    block_rows = jnp.clip(pick(cnt) - (blk - pick(bstart)) * bm, 0, bm)
    block_rows = jnp.where(blk < n_used, block_rows, 0).astype(I32)
    block_ord = pick(ord_e).astype(I32)
    experts = jnp.arange(N_EXPERTS, dtype=I32)
    is_kth = (ord_e[None, :] == jnp.arange(N_EXPERTS + WEIGHT_LOOKAHEAD, dtype=I32)[:, None]) & (blocks_e > 0)[None, :]
    used_experts = jnp.where(jnp.any(is_kth, axis=1), jnp.sum(jnp.where(is_kth, experts[None, :], 0), axis=1), -1)
    gap_lo = (bstart * bm + cnt).astype(I32)
    gap_hi = (bend * bm).astype(I32)
    e = route_t[COL_E0:COL_E1 + 1].astype(I32)
    rank = route_t[COL_R0:COL_R1 + 1].astype(I32)
    first_row = jnp.sum(jnp.where(e[:, None, :] == experts[None, :, None], (bstart * bm)[None, :, None], 0), axis=1)
    spare = n_blocks * bm + (2 * jnp.arange(rows, dtype=I32)[None, :] + jnp.arange(2, dtype=I32)[:, None]) % SPARE_ROWS
    pos = jnp.where(e >= 0, first_row + rank, spare).astype(I32)

    xs = _scatter(gap_lo, gap_hi, n_used.astype(I32).reshape(1), pos, xp, n_blocks * bm)
    ys = _experts(block_expert.astype(I32), block_rows, last_block.astype(I32).reshape(1), block_ord,
                  used_experts.astype(I32), xs, w_gate[0], w_up[0], w_down[0])
    return _combine(pos, h2, route, norm_final[None].astype(F32), ys, batch, nb, seq)
```

```python
import functools
import math

import numpy as np
import jax
import jax.numpy as jnp
from jax import lax
from jax.experimental import pallas as pl
from jax.experimental.pallas import tpu as pltpu

F32 = jnp.float32
BF16 = jnp.bfloat16
I32 = jnp.int32
U32 = jnp.uint32

BLOCK = 128
N_META = 16
PAD = BLOCK - N_META
HEAD_DIM = 64
N_KV_HEADS = 2
Q_PER_KV = 8
N_Q_HEADS = N_KV_HEADS * Q_PER_KV
WINDOW = 128
POOL_WINDOWS = (2, 4, 8, 16)
POOL_HALO = 16
N_BUCKETS = 32
MAX_DISTANCE = 128
N_GROUPS = 8
EXPERTS_PER_GROUP = 8
N_EXPERTS = N_GROUPS * EXPERTS_PER_GROUP
RMS_EPS = 1e-6
LANES = 128
VMEM_LIMIT = 56 * 1024 * 1024

ROW_TILE = 256
EXPERT_ROWS = 256
WEIGHT_LOOKAHEAD = 2
SPARE_ROWS = 4 * ROW_TILE
COL_E0, COL_E1, COL_W0, COL_W1, COL_R0, COL_R1 = 0, 1, 2, 3, 4, 5
ROUTE_ROWS = 8


def _params(sem, vmem=VMEM_LIMIT):
    return pltpu.CompilerParams(dimension_semantics=sem, vmem_limit_bytes=vmem)


def _resident(shape):
    nd = len(shape)
    return pl.BlockSpec(shape, lambda *_: (0,) * nd, pipeline_mode=pl.Buffered(1))


def _tile_block_specs(nb, per_batch, d):
    def spec(half):
        def index(i, *_):
            g = 2 * i + half
            return ((g // nb) * per_batch + jnp.maximum(g % nb - 1, 0), 0)
        return pl.BlockSpec((BLOCK, d), index)
    return [spec(0), spec(1)]


def _tile_rows(nb, lead_ref, xa_ref, xb_ref):
    i = pl.program_id(0)
    halves = [jnp.where((2 * i + half) % nb == 0, lead_ref[...], ref[...])
              for half, ref in enumerate((xa_ref, xb_ref))]
    return jnp.concatenate(halves, axis=0)


def _inproj_kernel(nb, lead_ref, xa_ref, xb_ref, g_ref, w_ref, q_ref, kv_ref, u_ref, ga_ref, gp_ref):
    x = _tile_rows(nb, lead_ref, xa_ref, xb_ref)
    ms = jnp.mean(x * x, axis=-1, keepdims=True)
    hn = (x * lax.rsqrt(ms + RMS_EPS) * g_ref[...]).astype(BF16)
    off = 0
    for ref in (q_ref, kv_ref, u_ref, ga_ref, gp_ref):
        width = ref.shape[1]
        for c in range(0, width, 1024):
            cw = min(1024, width - c)
            ref[:, c:c + cw] = jnp.dot(hn, w_ref[:, off + c:off + c + cw],
                                       preferred_element_type=F32).astype(BF16)
        off += width


def _inproj(lead, x2, nb, g, w, widths):
    d = x2.shape[1]
    tm = ROW_TILE
    per_batch = nb - 1
    rows = x2.shape[0] // per_batch * nb
    outs = [jax.ShapeDtypeStruct((rows, wd), BF16) for wd in widths]
    return pl.pallas_call(
        functools.partial(_inproj_kernel, nb),
        grid=(rows // tm,),
        in_specs=[_resident(lead.shape)] + _tile_block_specs(nb, per_batch, d)
                 + [_resident((1, d)), _resident(w.shape)],
        out_specs=[pl.BlockSpec((tm, wd), lambda i: (i, 0)) for wd in widths],
        out_shape=outs,
        compiler_params=_params(("parallel",)),
        name="inproj",
    )(lead, x2, x2, g, w)


PAIR = 2 * HEAD_DIM
KEYS = 3 * BLOCK
N_PAIRS = N_Q_HEADS // 2


def _attn_kernel(sink_ref, q_ref, kvp_ref, kvc_ref, kvm_ref, bias_ref, o_ref):
    kw = N_KV_HEADS * HEAD_DIM
    zeros = jnp.zeros((KEYS, HEAD_DIM), BF16)
    ones = jnp.ones((KEYS, HEAD_DIM), BF16)
    lane = lax.broadcasted_iota(I32, (BLOCK, PAIR), 1)
    scale = jnp.asarray(HEAD_DIM ** -0.5, BF16)
    for hk in range(N_KV_HEADS):
        ks = slice(hk * HEAD_DIM, (hk + 1) * HEAD_DIM)
        vs = slice(kw + hk * HEAD_DIM, kw + (hk + 1) * HEAD_DIM)
        k3 = jnp.concatenate([kvp_ref[:, ks], kvc_ref[:, ks], kvm_ref[:, ks]], axis=0) * scale
        v3 = jnp.concatenate([kvp_ref[:, vs], kvc_ref[:, vs], kvm_ref[:, vs]], axis=0)
        kbd = jnp.concatenate([jnp.concatenate([k3, zeros], axis=1),
                               jnp.concatenate([zeros, k3], axis=1)], axis=0)
        vext = jnp.concatenate([jnp.concatenate([v3, zeros, ones, zeros], axis=1),
                                jnp.concatenate([zeros, v3, zeros, ones], axis=1)], axis=0)
        for jp in range(Q_PER_KV // 2):
            j = hk * (Q_PER_KV // 2) + jp
            qp = q_ref[:, j * PAIR:(j + 1) * PAIR]
            s = lax.dot_general(qp, kbd, (((1,), (1,)), ((), ())),
                                preferred_element_type=F32) + bias_ref[0, j]
            probs, sink_terms = [], []
            for side in range(2):
                ss = s[:, side * KEYS:(side + 1) * KEYS]
                sink = sink_ref[2 * j + side]
                mx = jnp.maximum(jnp.max(ss, axis=-1, keepdims=True), sink)
                probs.append(jnp.exp(ss - mx))
                sink_terms.append(jnp.exp(sink - mx))
            p = jnp.concatenate(probs, axis=1).astype(BF16)
            r = jnp.dot(p, vext, preferred_element_type=F32)
            den = r[:, PAIR:] + jnp.where(lane < HEAD_DIM, sink_terms[0], sink_terms[1])
            o_ref[:, j * PAIR:(j + 1) * PAIR] = (r[:, :PAIR] / den).astype(o_ref.dtype)


def _attention(q, kv, bias, sinks, batch, nb):
    rows, aw = q.shape
    kvw = kv.shape[1]
    return pl.pallas_call(
        _attn_kernel,
        grid=(batch, nb),
        in_specs=[pl.BlockSpec(memory_space=pltpu.SMEM),
                  pl.BlockSpec((BLOCK, aw), lambda b, n: (b * nb + n, 0)),
                  pl.BlockSpec((BLOCK, kvw), lambda b, n: (b * nb + jnp.maximum(n - 1, 0), 0)),
                  pl.BlockSpec((BLOCK, kvw), lambda b, n: (b * nb + n, 0)),
                  pl.BlockSpec((BLOCK, kvw), lambda b, n: (b * nb, 0)),
                  pl.BlockSpec((1,) + bias.shape[1:], lambda b, n: (jnp.minimum(n, 2), 0, 0, 0))],
        out_specs=pl.BlockSpec((BLOCK, aw), lambda b, n: (b * nb + n, 0)),
        out_shape=jax.ShapeDtypeStruct((rows, aw), BF16),
        compiler_params=_params(("parallel", "arbitrary")),
        name="attention",
    )(sinks, q, kv, kv, kv, bias)


def _attn_bias_tables(rel_bias):
    max_exact = N_BUCKETS // 2
    qi = np.arange(BLOCK)[:, None]
    kj = np.arange(2 * BLOCK)[None, :]
    mj = np.arange(BLOCK)[None, :]
    dist = qi + BLOCK - kj
    d = np.maximum(dist, 0)
    large = max_exact + (np.log(np.maximum(d, 1).astype(np.float32) / max_exact)
                         / math.log(MAX_DISTANCE / max_exact) * (N_BUCKETS - max_exact)).astype(np.int32)
    bucket = np.where(d < max_exact, d, np.minimum(large, N_BUCKETS - 1))
    onehot = (bucket[..., None] == np.arange(N_BUCKETS)).astype(np.float32)
    rb = rel_bias.astype(F32)
    win = jnp.einsum("qkb,bh->hqk", onehot, rb, precision=lax.Precision.HIGHEST)
    meta = jnp.broadcast_to(rb[N_BUCKETS - 1][:, None, None], (N_Q_HEADS, BLOCK, BLOCK))
    full = jnp.concatenate([win, meta], axis=2)
    masks = []
    for n in range(3):
        win_ok = (dist >= 0) & (dist < WINDOW) & ((n - 1) * BLOCK + kj >= PAD)
        meta_ok = (mj >= PAD) & (n * BLOCK + qi - mj >= WINDOW)
        masks.append(np.concatenate([win_ok, meta_ok], axis=1))
    t = jnp.where(np.stack(masks)[:, None], full[None], -jnp.inf)
    t = t.reshape(3, N_PAIRS, 2, BLOCK, KEYS).transpose(0, 1, 3, 2, 4)
    return t.reshape(3, N_PAIRS, BLOCK, 2 * KEYS)


MIX_CHUNKS = 4


def _interleave(pattern, **streams):
    for key in pattern:
        next(streams[key], None)
    for s in streams.values():
        for _ in s:
            pass


def _mixer_kernel(lp, n_tiles, attn_ref, u_ref, uh_ref, ga_ref, gp_ref, lead_ref, xa_ref, xb_ref,
                  pmix_ref, pscale_ref, wa_ref, wp_ref, wo_ref, g2_ref, wr_ref, br_ref,
                  h2_ref, xp_ref, route_ref, route_t_ref, cnt_ref, run_ref, h2_s, a_s, p_s, m_s):
    i = pl.program_id(0)
    tm = h2_ref.shape[0]
    d = h2_ref.shape[1]
    cw = d // MIX_CHUNKS

    @pl.when(i == 0)
    def _():
        run_ref[...] = jnp.zeros_like(run_ref)
        h2_s[...] = jnp.zeros_like(h2_s)

    def chain():
        ic = jnp.minimum(i, n_tiles - 1)
        t = (ic * tm + lax.broadcasted_iota(I32, (tm, 1), 0)) % lp
        valid = t >= PAD
        tx = (ic * tm - POOL_HALO + lp + lax.broadcasted_iota(I32, (tm + POOL_HALO, 1), 0)) % lp
        n_rows = (t - PAD + 1).astype(F32)
        gw = u_ref.shape[1] // len(POOL_WINDOWS)
        attn = attn_ref[...]
        aw = d // len(POOL_WINDOWS)
        for gi, w in enumerate(POOL_WINDOWS):
            a_s[:, gi * aw:(gi + 1) * aw] = jnp.dot(attn, wa_ref[:, gi * aw:(gi + 1) * aw],
                                                    preferred_element_type=F32)
            cols = slice(gi * gw, (gi + 1) * gw)
            uext = jnp.concatenate([uh_ref[:, cols], u_ref[:, cols]], axis=0).astype(F32)
            c = jnp.where(tx >= PAD, uext, 0.0)
            s, span = c, 1
            while span < w:
                s = s[span:] + s[:-span]
                span *= 2
            win = s[POOL_HALO + 1 - w:POOL_HALO + 1 - w + tm]
            n_valid = jnp.clip(n_rows, 1.0, float(w))
            mixed = jnp.where(valid, win / n_valid - c[POOL_HALO:], 0.0)
            m_s[:, cols] = (jnp.dot(mixed.astype(BF16), pmix_ref[gi], preferred_element_type=F32)
                            * pscale_ref[:, cols]).astype(BF16)
            yield
        pool = m_s[:, :u_ref.shape[1]]
        for c in range(MIX_CHUNKS):
            cols = slice(c * cw, (c + 1) * cw)
            p_s[:, cols] = jnp.dot(pool, wp_ref[:, cols], preferred_element_type=F32)
            yield
        nb = lp // BLOCK
        halves = [jnp.where((2 * ic + hf) % nb == 0, lead_ref[...], ref[...])
                  for hf, ref in enumerate((xa_ref, xb_ref))]
        h2 = jnp.concatenate(halves, axis=0)
        for c in range(MIX_CHUNKS):
            cols = slice(c * cw, (c + 1) * cw)
            merged = (jax.nn.sigmoid(ga_ref[:, cols].astype(F32)) * a_s[:, cols]
                      + jax.nn.sigmoid(gp_ref[:, cols].astype(F32)) * p_s[:, cols]).astype(BF16)
            yield
            h2 = h2 + jnp.dot(merged, wo_ref[cols, :], preferred_element_type=F32)
            yield
        h2_ref[...] = h2
        h2_s[...] = h2
        yield

    def tail():
        h2p = h2_s[...]
        tp = ((i - 1) * tm + lp + lax.broadcasted_iota(I32, (tm, 1), 0)) % lp
        valid_p = (tp >= PAD) & (i >= 1)
        ms = jnp.mean(h2p * h2p, axis=-1, keepdims=True)
        inv = lax.rsqrt(ms + RMS_EPS)
        yield
        half = d // 2
        hi_prod = jnp.zeros((tm, 2 * LANES), F32)
        lo_prod = jnp.zeros((tm, LANES), F32)
        n_pack = MIX_CHUNKS // 2
        pw = half // n_pack
        for c in range(n_pack):
            parts = []
            for base in (c * pw, half + c * pw):
                hn = h2p[:, base:base + pw] * inv * g2_ref[:, base:base + pw]
                x_hi = hn.astype(BF16)
                x_hi32 = x_hi.astype(F32)
                x_lo = (hn - x_hi32).astype(BF16)
                hi_prod = hi_prod + jnp.dot(x_hi, wr_ref[base:base + pw, :], preferred_element_type=F32)
                lo_prod = lo_prod + jnp.dot(x_lo, wr_ref[base:base + pw, :LANES], preferred_element_type=F32)
                parts.append(lax.bitcast_convert_type(x_hi32, U32))
            xp_ref[:, c * pw:(c + 1) * pw] = (parts[1] & jnp.uint32(0xFFFF0000)) | (parts[0] >> 16)
            yield
        logits = hi_prod[:, :LANES] + (hi_prod[:, LANES:] + lo_prod) + br_ref[...]
        col = lax.broadcasted_iota(I32, logits.shape, 1).astype(F32)
        neg = -jnp.inf
        gl = jnp.where(col < N_GROUPS, logits, neg)
        gmax = jnp.max(gl, axis=-1, keepdims=True)
        grp = jnp.min(jnp.where(gl == gmax, col, float(LANES)), axis=-1, keepdims=True)
        p_grp = 1.0 / jnp.sum(jnp.exp(gl - gmax), axis=-1, keepdims=True)
        yield
        e_lo = N_GROUPS + grp * EXPERTS_PER_GROUP
        el = jnp.where((col >= e_lo) & (col < e_lo + EXPERTS_PER_GROUP), logits, neg)
        m1 = jnp.max(el, axis=-1, keepdims=True)
        i1 = jnp.min(jnp.where(el == m1, col, float(LANES)), axis=-1, keepdims=True)
        el2 = jnp.where(col == i1, neg, el)
        m2 = jnp.max(el2, axis=-1, keepdims=True)
        i2 = jnp.min(jnp.where(el2 == m2, col, float(LANES)), axis=-1, keepdims=True)
        z = jnp.exp(m2 - m1)
        w1 = p_grp / (1.0 + z)
        w2 = p_grp * z / (1.0 + z)
        e1 = i1 - N_GROUPS
        e2 = i2 - N_GROUPS
        yield
        oh1 = jnp.where((col == e1) & valid_p, 1.0, 0.0)
        oh2 = jnp.where((col == e2) & valid_p, 1.0, 0.0)
        lower = (lax.broadcasted_iota(I32, (tm, tm), 0) > lax.broadcasted_iota(I32, (tm, tm), 1))
        lower = jnp.where(lower, 1.0, 0.0).astype(BF16)
        before1 = jnp.dot(lower, oh1.astype(BF16), preferred_element_type=F32)
        before2 = jnp.dot(lower, oh2.astype(BF16), preferred_element_type=F32)
        tot1 = jnp.sum(oh1, axis=0, keepdims=True)
        tot2 = jnp.sum(oh2, axis=0, keepdims=True)
        run = run_ref[...]
        r1 = jnp.sum(oh1 * (run + before1), axis=-1, keepdims=True)
        r2 = jnp.sum(oh2 * (run + tot1 + before2), axis=-1, keepdims=True)
        run = run + tot1 + tot2
        run_ref[...] = run
        cnt_ref[...] = run
        yield
        slab = jnp.zeros(logits.shape, F32)
        for cidx, val in ((COL_E0, jnp.where(valid_p, e1, -1.0)),
                          (COL_E1, jnp.where(valid_p, e2, -1.0)),
                          (COL_W0, w1), (COL_W1, w2), (COL_R0, r1), (COL_R1, r2)):
            slab = jnp.where(col == cidx, val, slab)
        route_ref[...] = slab
        route_t_ref[...] = slab.T[:ROUTE_ROWS, :]
        yield

    _interleave("cccc" + "ctctctct" + "cctcctcctcc", c=chain(), t=tail())


def _mixer(lp, attn, u, ga, gp, lead, x2, pmix, pscale, wa, wp, wo, g2, wr, br):
    rows = attn.shape[0]
    d = x2.shape[1]
    tm = ROW_TILE
    nb = lp // BLOCK
    n_tiles = rows // tm
    halo_blocks = tm // POOL_HALO
    cur = lambda i: (jnp.minimum(i, n_tiles - 1), 0)
    prev = lambda i: (jnp.maximum(i - 1, 0), 0)

    def x_spec(half):
        def index(i):
            g = 2 * jnp.minimum(i, n_tiles - 1) + half
            return ((g // nb) * (nb - 1) + jnp.maximum(g % nb - 1, 0), 0)
        return pl.BlockSpec((BLOCK, d), index)

    return pl.pallas_call(
        functools.partial(_mixer_kernel, lp, n_tiles),
        grid=(n_tiles + 1,),
        in_specs=[pl.BlockSpec((tm, attn.shape[1]), cur),
                  pl.BlockSpec((tm, u.shape[1]), cur),
                  pl.BlockSpec((POOL_HALO, u.shape[1]),
                               lambda i: (jnp.maximum(jnp.minimum(i, n_tiles - 1) * halo_blocks - 1, 0), 0)),
                  pl.BlockSpec((tm, d), cur),
                  pl.BlockSpec((tm, d), cur),
                  _resident(lead.shape), x_spec(0), x_spec(1),
                  _resident(pmix.shape), _resident(pscale.shape), _resident(wa.shape),
                  _resident(wp.shape), _resident(wo.shape), _resident(g2.shape),
                  _resident(wr.shape), _resident(br.shape)],
        out_specs=[pl.BlockSpec((tm, d), cur),
                   pl.BlockSpec((tm, d // 2), prev),
                   pl.BlockSpec((tm, LANES), prev),
                   pl.BlockSpec((ROUTE_ROWS, tm), lambda i: (0, jnp.maximum(i - 1, 0))),
                   pl.BlockSpec((1, LANES), lambda i: (0, 0))],
        out_shape=[jax.ShapeDtypeStruct((rows, d), F32),
                   jax.ShapeDtypeStruct((rows, d // 2), U32),
                   jax.ShapeDtypeStruct((rows, LANES), F32),
                   jax.ShapeDtypeStruct((ROUTE_ROWS, rows), F32),
                   jax.ShapeDtypeStruct((1, LANES), F32)],
        scratch_shapes=[pltpu.VMEM((1, LANES), F32), pltpu.VMEM((tm, d), F32),
                        pltpu.VMEM((tm, d), F32), pltpu.VMEM((tm, d), F32), pltpu.VMEM((tm, d), BF16)],
        compiler_params=_params(("arbitrary",)),
        name="mixer",
    )(attn, u, u, ga, gp, lead, x2, x2, pmix, pscale, wa, wp, wo, g2, wr, br)


def _zero_copies(gap_lo_ref, gap_hi_ref, used_ref, xs_ref, zero_ref, sem, op, spare):
    bm = zero_ref.shape[0]
    n_rows = xs_ref.shape[0]
    n_blocks = (n_rows - SPARE_ROWS) // bm

    def piece(start, size):
        return pltpu.make_async_copy(zero_ref.at[pl.ds(0, size), :], xs_ref.at[pl.ds(start, size), :], sem)

    if spare:
        for b in range(n_blocks, n_blocks + SPARE_ROWS // bm):
            op(piece(b * bm, bm))
    else:
        def tail(e, carry):
            lo, hi = gap_lo_ref[e], gap_hi_ref[e]
            length = hi - lo
            for bit in range(int(math.log2(bm)) - 1, 2, -1):
                size = 1 << bit
                above = (length >> (bit + 1)) << (bit + 1)

                @pl.when((length >> bit) & 1 == 1)
                def _():
                    op(piece(pl.multiple_of(hi - above - size, 8), size))
            for j in range(7):
                @pl.when(j < (length & 7))
                def _():
                    op(piece(lo + j, 1))
            return carry

        def block(b, carry):
            op(piece(pl.multiple_of(b * bm, bm), bm))
            return carry

        lax.fori_loop(0, N_EXPERTS, tail, 0)
        lax.fori_loop(used_ref[0], n_blocks, block, 0)


def _scatter_kernel(gap_lo_ref, gap_hi_ref, used_ref, pos0_ref, pos1_ref, x_ref, xs_ref, stage_ref, zero_ref,
                    sem, zsem):
    i = pl.program_id(0)
    last = pl.num_programs(0) - 1
    tm = x_ref.shape[0]
    slot = i % 2

    def zero_copies(op, spare):
        _zero_copies(gap_lo_ref, gap_hi_ref, used_ref, xs_ref, zero_ref, zsem.at[int(spare)], op, spare)

    @pl.when(i == 0)
    def _():
        zero_ref[...] = jnp.zeros_like(zero_ref)
        zero_copies(lambda c: c.start(), True)
        zero_copies(lambda c: c.start(), False)
        zero_copies(lambda c: c.wait(), True)

    def retire(s):
        for k in range(2):
            pltpu.make_async_copy(stage_ref.at[s], xs_ref.at[pl.ds(0, tm), :], sem.at[s]).wait()

    @pl.when(i >= 2)
    def _():
        retire(slot)

    stage_ref[slot] = x_ref[...]

    for r in range(tm):
        for k in range(2):
            p = (pos0_ref, pos1_ref)[k][i * tm + r]
            pltpu.make_async_copy(stage_ref.at[slot, pl.ds(r, 1), :], xs_ref.at[pl.ds(p, 1), :],
                                  sem.at[slot]).start(priority=k)

    @pl.when(i == last)
    def _():
        @pl.when(i >= 1)
        def _():
            retire(1 - slot)
        retire(slot)
        zero_copies(lambda c: c.wait(), False)


def _scatter(gap_lo, gap_hi, n_used, pos, xp, slots):
    rows, half = xp.shape
    tm = ROW_TILE
    assert SPARE_ROWS % EXPERT_ROWS == 0 and slots % EXPERT_ROWS == 0
    return pl.pallas_call(
        _scatter_kernel,
        grid_spec=pltpu.PrefetchScalarGridSpec(
            num_scalar_prefetch=5,
            grid=(rows // tm,),
            in_specs=[pl.BlockSpec((tm, half), lambda i, *_: (i, 0))],
            out_specs=pl.BlockSpec(memory_space=pl.ANY),
            scratch_shapes=[pltpu.VMEM((2, tm, half), U32), pltpu.VMEM((EXPERT_ROWS, half), U32),
                            pltpu.SemaphoreType.DMA((2,)), pltpu.SemaphoreType.DMA((2,))],
        ),
        out_shape=jax.ShapeDtypeStruct((slots + SPARE_ROWS, half), U32),
        compiler_params=_params(("arbitrary",)),
        name="scatter",
    )(gap_lo, gap_hi, n_used, pos[0], pos[1], xp)


def _pack_halves(x):
    n = x.shape[1] // 2
    r = x.astype(BF16).astype(F32)
    lo = lax.bitcast_convert_type(r[:, :n], U32)
    hi = lax.bitcast_convert_type(r[:, n:], U32)
    return (hi & jnp.uint32(0xFFFF0000)) | (lo >> 16)


def _unpack_halves(p):
    return (lax.bitcast_convert_type(p << 16, F32),
            lax.bitcast_convert_type(p & jnp.uint32(0xFFFF0000), F32))


def _expert_kernel(be_ref, nrow_ref, last_ref, ord_ref, used_ref, xs_ref, wg_hbm, wu_hbm, wd_hbm, ys_ref,
                   wg_f, wu_f, wd_f, sem):
    b = pl.program_id(0)
    n_valid_rows = nrow_ref[b]
    expert = be_ref[b]
    ordinal = ord_ref[b]
    n_slots = wg_f.shape[0]
    slot = ordinal % n_slots

    def weight_copies(e, s):
        return [pltpu.make_async_copy(hbm.at[e], buf.at[s], sem.at[s, n])
                for n, (hbm, buf) in enumerate(((wg_hbm, wg_f), (wu_hbm, wu_f), (wd_hbm, wd_f)))]

    def start_fetch(o):
        e = used_ref[o]

        @pl.when(e >= 0)
        def _():
            for c in weight_copies(e, o % n_slots):
                c.start()

    @pl.when(b == 0)
    def _():
        for o in range(WEIGHT_LOOKAHEAD):
            start_fetch(o)

    @pl.when((b == 0) | (expert != be_ref[jnp.maximum(b - 1, 0)]))
    def _():
        start_fetch(ordinal + WEIGHT_LOOKAHEAD)
        for c in weight_copies(expert, slot):
            c.wait()

    @pl.when(n_valid_rows == 0)
    def _():
        ys_ref[...] = jnp.zeros_like(ys_ref)

    @pl.when(n_valid_rows > 0)
    def _():
        lo, hi = _unpack_halves(xs_ref[...])
        x = jnp.concatenate([lo, hi], axis=1).astype(BF16)
        gate = jnp.dot(x, wg_f[slot].astype(BF16), preferred_element_type=F32)
        up = jnp.dot(x, wu_f[slot].astype(BF16), preferred_element_type=F32)
        hb = (jax.nn.silu(gate) * up).astype(BF16)
        ys_ref[...] = _pack_halves(jnp.dot(hb, wd_f[slot].astype(BF16), preferred_element_type=F32))


def _experts(block_expert, block_rows, last_block, block_ord, used_experts, xs, w_gate, w_up, w_down):
    half = xs.shape[1]
    n_exp, d, de = w_gate.shape
    bm = EXPERT_ROWS
    n_blocks = block_expert.shape[0]
    slots = n_blocks * bm
    n_slots = WEIGHT_LOOKAHEAD + 1
    any_space = pl.BlockSpec(memory_space=pl.ANY)
    return pl.pallas_call(
        _expert_kernel,
        grid_spec=pltpu.PrefetchScalarGridSpec(
            num_scalar_prefetch=5,
            grid=(n_blocks,),
            in_specs=[pl.BlockSpec((bm, half), lambda b, be, nr, last, *_: (jnp.minimum(b, last[0]), 0)),
                      any_space, any_space, any_space],
            out_specs=pl.BlockSpec((bm, half), lambda b, *_: (b, 0)),
            scratch_shapes=[pltpu.VMEM((n_slots, d, de), F32), pltpu.VMEM((n_slots, d, de), F32),
                            pltpu.VMEM((n_slots, de, d), F32),
                            pltpu.SemaphoreType.DMA((n_slots, 3))],
        ),
        out_shape=jax.ShapeDtypeStruct((slots, half), U32),
        compiler_params=_params(("arbitrary",)),
        name="experts",
    )(block_expert, block_rows, last_block, block_ord, used_experts, xs, w_gate, w_up, w_down)


def _combine_kernel(nb, per_batch, pos0_ref, pos1_ref, h2_ref, route_ref, g_ref, ys_ref, o_ref, y_ref, sem):
    t = pl.program_id(0)
    tm = h2_ref.shape[0]

    def issue(tile, slot):
        base = ((tile // per_batch) * nb + 1 + tile % per_batch) * tm

        for r in range(tm):
            for k in range(2):
                p = (pos0_ref, pos1_ref)[k][base + r]
                pltpu.make_async_copy(ys_ref.at[pl.ds(p, 1), :], y_ref.at[slot, k, pl.ds(r, 1), :],
                                      sem.at[slot]).start(priority=k)

    @pl.when(t == 0)
    def _():
        issue(0, 0)

    @pl.when(t + 1 < pl.num_programs(0))
    def _():
        issue(t + 1, (t + 1) % 2)

    slot = t % 2
    for k in range(2):
        pltpu.make_async_copy(ys_ref.at[pl.ds(0, tm), :], y_ref.at[slot, k], sem.at[slot]).wait()
    route = route_ref[...]
    w0 = route[:, COL_W0:COL_W0 + 1]
    w1 = route[:, COL_W1:COL_W1 + 1]
    half = y_ref.shape[-1]
    y0 = _unpack_halves(y_ref[slot, 0])
    y1 = _unpack_halves(y_ref[slot, 1])
    hs = [h2_ref[:, s * half:(s + 1) * half] + (w0 * y0[s] + w1 * y1[s]) for s in range(2)]
    ms = sum(jnp.sum(h * h, axis=-1, keepdims=True) for h in hs) / (2 * half)
    inv = lax.rsqrt(ms + RMS_EPS)
    for s in range(2):
        o_ref[0, :, s * half:(s + 1) * half] = hs[s] * inv * g_ref[:, s * half:(s + 1) * half]


def _combine(pos, h2, route, g, ys, batch, nb, seq):
    rows, d = h2.shape
    tm = BLOCK
    per_batch = seq // tm
    tile = lambda t, *_: ((t // per_batch) * nb + 1 + t % per_batch, 0)
    return pl.pallas_call(
        functools.partial(_combine_kernel, nb, per_batch),
        grid_spec=pltpu.PrefetchScalarGridSpec(
            num_scalar_prefetch=2,
            grid=(batch * per_batch,),
            in_specs=[pl.BlockSpec((tm, d), tile),
                      pl.BlockSpec((tm, LANES), tile),
                      pl.BlockSpec((1, d), lambda t, *_: (0, 0)),
                      pl.BlockSpec(memory_space=pl.ANY)],
            out_specs=pl.BlockSpec((1, tm, d), lambda t, *_: (t // per_batch, t % per_batch, 0)),
            scratch_shapes=[pltpu.VMEM((2, 2, tm, ys.shape[1]), U32), pltpu.SemaphoreType.DMA((2,))],
        ),
        out_shape=jax.ShapeDtypeStruct((batch, seq, d), F32),
        compiler_params=_params(("arbitrary",)),
        name="combine",
    )(pos[0], pos[1], h2, route, g, ys)


def kernel(x, meta_tokens, rel_bias, norm_mix, w_in, attn_sinks, pool_mix, pool_scale,
           w_attn_branch, w_pool_branch, w_out, norm_ffn, w_router_group, b_router_group,
           w_router_expert, b_router_expert, w_gate, w_up, w_down, norm_final):
    batch, seq, d = x.shape
    depth = w_in.shape[0]
    assert depth == 1, "single-layer stack"
    aw = w_attn_branch.shape[1]
    pw = w_pool_branch.shape[1]
    kvw = 2 * N_KV_HEADS * HEAD_DIM
    assert aw == N_Q_HEADS * HEAD_DIM and w_in.shape[2] == aw + kvw + pw + 2 * d
    assert seq % BLOCK == 0
    lp = seq + BLOCK
    nb = lp // BLOCK
    rows = batch * lp
    assert rows % ROW_TILE == 0

    assert ROW_TILE == 2 * BLOCK
    lead = jnp.concatenate([jnp.zeros((PAD, d), x.dtype), meta_tokens.astype(x.dtype)], axis=0)
    x2 = x.reshape(batch * seq, d)

    q, kv, u, ga, gp = _inproj(lead, x2, nb, norm_mix[0][None], w_in[0].astype(BF16), (aw, kvw, pw, d, d))

    attn = _attention(q, kv, _attn_bias_tables(rel_bias), attn_sinks[0].astype(F32), batch, nb)

    n_router = N_GROUPS + N_EXPERTS
    wr = jnp.concatenate([w_router_group[0], w_router_expert[0], jnp.zeros((d, LANES - n_router), F32)], axis=1)
    br = jnp.concatenate([b_router_group[0], b_router_expert[0], jnp.zeros((LANES - n_router,), F32)])[None]
    wr_hi = wr.astype(BF16)
    wr_split = jnp.concatenate([wr_hi, (wr - wr_hi.astype(F32)).astype(BF16)], axis=1)
    h2, xp, route, route_t, counts = _mixer(
        lp, attn, u, ga, gp, lead, x2, pool_mix[0].astype(BF16), pool_scale[0][None].astype(F32),
        w_attn_branch[0].astype(BF16), w_pool_branch[0].astype(BF16), w_out[0].astype(BF16),
        norm_ffn[0][None], wr_split, br)

    bm = EXPERT_ROWS
    n_tok = batch * (seq + N_META)
    n_blocks = (2 * n_tok) // bm + N_EXPERTS
    cnt = counts[0, :N_EXPERTS].astype(I32)
    blocks_e = (cnt + bm - 1) // bm
    bend = jnp.cumsum(blocks_e)
    bstart = bend - blocks_e
    ord_e = jnp.cumsum((blocks_e > 0).astype(I32)) - 1
    n_used = bend[-1]
    last_block = jnp.maximum(n_used - 1, 0)
    blk = jnp.arange(n_blocks, dtype=I32)

    def expert_of(block):
        return jnp.minimum(jnp.sum((bend[None, :] <= block[:, None]).astype(I32), axis=1), N_EXPERTS - 1)

    block_expert = expert_of(jnp.minimum(blk, last_block))
    own = block_expert[:, None] == jnp.arange(N_EXPERTS, dtype=I32)[None, :]
    pick = lambda v: jnp.sum(jnp.where(own, v[None, :], 0), axis=1)
    block_rows = jnp.clip(pick(cnt) - (blk - pick(bstart)) * bm, 0, bm)
    block_rows = jnp.where(blk < n_used, block_rows, 0).astype(I32)
    block_ord = pick(ord_e).astype(I32)
    experts = jnp.arange(N_EXPERTS, dtype=I32)
    is_kth = (ord_e[None, :] == jnp.arange(N_EXPERTS + WEIGHT_LOOKAHEAD, dtype=I32)[:, None]) & (blocks_e > 0)[None, :]
    used_experts = jnp.where(jnp.any(is_kth, axis=1), jnp.sum(jnp.where(is_kth, experts[None, :], 0), axis=1), -1)
    gap_lo = (bstart * bm + cnt).astype(I32)
    gap_hi = (bend * bm).astype(I32)
    e = route_t[COL_E0:COL_E1 + 1].astype(I32)
    rank = route_t[COL_R0:COL_R1 + 1].astype(I32)
    first_row = jnp.sum(jnp.where(e[:, None, :] == experts[None, :, None], (bstart * bm)[None, :, None], 0), axis=1)
    spare = n_blocks * bm + (2 * jnp.arange(rows, dtype=I32)[None, :] + jnp.arange(2, dtype=I32)[:, None]) % SPARE_ROWS
    pos = jnp.where(e >= 0, first_row + rank, spare).astype(I32)

    xs = _scatter(gap_lo, gap_hi, n_used.astype(I32).reshape(1), pos, xp, n_blocks * bm)
    ys = _experts(block_expert.astype(I32), block_rows, last_block.astype(I32).reshape(1), block_ord,
                  used_experts.astype(I32), xs, w_gate[0], w_up[0], w_down[0])
    return _combine(pos, h2, route, norm_final[None].astype(F32), ys, batch, nb, seq)
```

```python
import functools
import math

import numpy as np
import jax
import jax.numpy as jnp
from jax import lax
from jax.experimental import pallas as pl
from jax.experimental.pallas import tpu as pltpu

F32 = jnp.float32
BF16 = jnp.bfloat16
I32 = jnp.int32
U32 = jnp.uint32

BLOCK = 128
N_META = 16
PAD = BLOCK - N_META
HEAD_DIM = 64
N_KV_HEADS = 2
Q_PER_KV = 8
N_Q_HEADS = N_KV_HEADS * Q_PER_KV
WINDOW = 128
POOL_WINDOWS = (2, 4, 8, 16)
POOL_HALO = 16
N_BUCKETS = 32
MAX_DISTANCE = 128
N_GROUPS = 8
EXPERTS_PER_GROUP = 8
N_EXPERTS = N_GROUPS * EXPERTS_PER_GROUP
RMS_EPS = 1e-6
LANES = 128
VMEM_LIMIT = 56 * 1024 * 1024

ROW_TILE = 256
EXPERT_ROWS = 256
STAGE_COLS = 640
WEIGHT_LOOKAHEAD = 2
SPARE_ROWS = 4 * ROW_TILE
COL_E0, COL_E1, COL_W0, COL_W1, COL_R0, COL_R1 = 0, 1, 2, 3, 4, 5
ROUTE_ROWS = 8


def _params(sem, vmem=VMEM_LIMIT):
    return pltpu.CompilerParams(dimension_semantics=sem, vmem_limit_bytes=vmem)


def _resident(shape):
    nd = len(shape)
    return pl.BlockSpec(shape, lambda *_: (0,) * nd, pipeline_mode=pl.Buffered(1))


def _tile_block_specs(nb, per_batch, d):
    def spec(half):
        def index(i, *_):
            g = 2 * i + half
            return ((g // nb) * per_batch + jnp.maximum(g % nb - 1, 0), 0)
        return pl.BlockSpec((BLOCK, d), index)
    return [spec(0), spec(1)]


def _tile_rows(nb, lead_ref, xa_ref, xb_ref):
    i = pl.program_id(0)
    halves = [jnp.where((2 * i + half) % nb == 0, lead_ref[...], ref[...])
              for half, ref in enumerate((xa_ref, xb_ref))]
    return jnp.concatenate(halves, axis=0)


def _stage_bf16(w_hbm, w_ref, stage_ref, sem):
    rows, n = w_hbm.shape
    chunk = stage_ref.shape[1]
    for c0 in range(0, n, chunk):
        cw = min(chunk, n - c0)
        copy = pltpu.make_async_copy(w_hbm.at[:, pl.ds(c0, cw)], stage_ref.at[pl.ds(0, rows), pl.ds(0, cw)], sem)
        copy.start()
        copy.wait()
        w_ref[:, c0:c0 + cw] = stage_ref[0:rows, 0:cw].astype(BF16)


def _inproj_kernel(nb, lead_ref, xa_ref, xb_ref, g_ref, w_hbm, q_ref, kv_ref, u_ref, ga_ref, gp_ref,
                   w_ref, stage_ref, sem):
    @pl.when(pl.program_id(0) == 0)
    def _():
        _stage_bf16(w_hbm, w_ref, stage_ref, sem)

    x = _tile_rows(nb, lead_ref, xa_ref, xb_ref)
    ms = jnp.mean(x * x, axis=-1, keepdims=True)
    hn = (x * lax.rsqrt(ms + RMS_EPS) * g_ref[...]).astype(BF16)
    off = 0
    for ref in (q_ref, kv_ref, u_ref, ga_ref, gp_ref):
        width = ref.shape[1]
        for c in range(0, width, 1024):
            cw = min(1024, width - c)
            ref[:, c:c + cw] = jnp.dot(hn, w_ref[:, off + c:off + c + cw],
                                       preferred_element_type=F32).astype(BF16)
        off += width


def _inproj(lead, x2, nb, g, w, widths):
    d = x2.shape[1]
    tm = ROW_TILE
    per_batch = nb - 1
    rows = x2.shape[0] // per_batch * nb
    outs = [jax.ShapeDtypeStruct((rows, wd), BF16) for wd in widths]
    return pl.pallas_call(
        functools.partial(_inproj_kernel, nb),
        grid=(rows // tm,),
        in_specs=[_resident(lead.shape)] + _tile_block_specs(nb, per_batch, d)
                 + [_resident((1, d)), pl.BlockSpec(memory_space=pl.ANY)],
        out_specs=[pl.BlockSpec((tm, wd), lambda i: (i, 0)) for wd in widths],
        out_shape=outs,
        scratch_shapes=[pltpu.VMEM(w.shape, BF16), pltpu.VMEM((w.shape[0], STAGE_COLS), F32),
                        pltpu.SemaphoreType.DMA(())],
        compiler_params=_params(("arbitrary",)),
        name="inproj",
    )(lead, x2, x2, g, w)


PAIR = 2 * HEAD_DIM
KEYS = 3 * BLOCK
N_PAIRS = N_Q_HEADS // 2


def _attn_kernel(sink_ref, q_ref, kvp_ref, kvc_ref, kvm_ref, bias_ref, o_ref):
    kw = N_KV_HEADS * HEAD_DIM
    zeros = jnp.zeros((KEYS, HEAD_DIM), BF16)
    ones = jnp.ones((KEYS, HEAD_DIM), BF16)
    lane = lax.broadcasted_iota(I32, (BLOCK, PAIR), 1)
    scale = jnp.asarray(HEAD_DIM ** -0.5, BF16)
    for hk in range(N_KV_HEADS):
        ks = slice(hk * HEAD_DIM, (hk + 1) * HEAD_DIM)
        vs = slice(kw + hk * HEAD_DIM, kw + (hk + 1) * HEAD_DIM)
        k3 = jnp.concatenate([kvp_ref[:, ks], kvc_ref[:, ks], kvm_ref[:, ks]], axis=0) * scale
        v3 = jnp.concatenate([kvp_ref[:, vs], kvc_ref[:, vs], kvm_ref[:, vs]], axis=0)
        kbd = jnp.concatenate([jnp.concatenate([k3, zeros], axis=1),
                               jnp.concatenate([zeros, k3], axis=1)], axis=0)
        vext = jnp.concatenate([jnp.concatenate([v3, zeros, ones, zeros], axis=1),
                                jnp.concatenate([zeros, v3, zeros, ones], axis=1)], axis=0)
        for jp in range(Q_PER_KV // 2):
            j = hk * (Q_PER_KV // 2) + jp
            qp = q_ref[:, j * PAIR:(j + 1) * PAIR]
            s = lax.dot_general(qp, kbd, (((1,), (1,)), ((), ())),
                                preferred_element_type=F32) + bias_ref[0, j]
            probs, sink_terms = [], []
            for side in range(2):
                ss = s[:, side * KEYS:(side + 1) * KEYS]
                sink = sink_ref[2 * j + side]
                mx = jnp.maximum(jnp.max(ss, axis=-1, keepdims=True), sink)
                probs.append(jnp.exp(ss - mx))
                sink_terms.append(jnp.exp(sink - mx))
            p = jnp.concatenate(probs, axis=1).astype(BF16)
            r = jnp.dot(p, vext, preferred_element_type=F32)
            den = r[:, PAIR:] + jnp.where(lane < HEAD_DIM, sink_terms[0], sink_terms[1])
            o_ref[:, j * PAIR:(j + 1) * PAIR] = (r[:, :PAIR] / den).astype(o_ref.dtype)


def _attention(q, kv, bias, sinks, batch, nb):
    rows, aw = q.shape
    kvw = kv.shape[1]
    return pl.pallas_call(
        _attn_kernel,
        grid=(batch, nb),
        in_specs=[pl.BlockSpec(memory_space=pltpu.SMEM),
                  pl.BlockSpec((BLOCK, aw), lambda b, n: (b * nb + n, 0)),
                  pl.BlockSpec((BLOCK, kvw), lambda b, n: (b * nb + jnp.maximum(n - 1, 0), 0)),
                  pl.BlockSpec((BLOCK, kvw), lambda b, n: (b * nb + n, 0)),
                  pl.BlockSpec((BLOCK, kvw), lambda b, n: (b * nb, 0)),
                  pl.BlockSpec((1,) + bias.shape[1:], lambda b, n: (jnp.minimum(n, 2), 0, 0, 0))],
        out_specs=pl.BlockSpec((BLOCK, aw), lambda b, n: (b * nb + n, 0)),
        out_shape=jax.ShapeDtypeStruct((rows, aw), BF16),
        compiler_params=_params(("parallel", "arbitrary")),
        name="attention",
    )(sinks, q, kv, kv, kv, bias)


def _attn_bias_tables(rel_bias):
    max_exact = N_BUCKETS // 2
    qi = np.arange(BLOCK)[:, None]
    kj = np.arange(2 * BLOCK)[None, :]
    mj = np.arange(BLOCK)[None, :]
    dist = qi + BLOCK - kj
    d = np.maximum(dist, 0)
    large = max_exact + (np.log(np.maximum(d, 1).astype(np.float32) / max_exact)
                         / math.log(MAX_DISTANCE / max_exact) * (N_BUCKETS - max_exact)).astype(np.int32)
    bucket = np.where(d < max_exact, d, np.minimum(large, N_BUCKETS - 1))
    onehot = (bucket[..., None] == np.arange(N_BUCKETS)).astype(np.float32)
    rb = rel_bias.astype(F32)
    win = jnp.einsum("qkb,bh->hqk", onehot, rb, precision=lax.Precision.HIGHEST)
    meta = jnp.broadcast_to(rb[N_BUCKETS - 1][:, None, None], (N_Q_HEADS, BLOCK, BLOCK))
    full = jnp.concatenate([win, meta], axis=2)
    masks = []
    for n in range(3):
        win_ok = (dist >= 0) & (dist < WINDOW) & ((n - 1) * BLOCK + kj >= PAD)
        meta_ok = (mj >= PAD) & (n * BLOCK + qi - mj >= WINDOW)
        masks.append(np.concatenate([win_ok, meta_ok], axis=1))
    t = jnp.where(np.stack(masks)[:, None], full[None], -jnp.inf)
    t = t.reshape(3, N_PAIRS, 2, BLOCK, KEYS).transpose(0, 1, 3, 2, 4)
    return t.reshape(3, N_PAIRS, BLOCK, 2 * KEYS)


MIX_CHUNKS = 4


def _interleave(pattern, **streams):
    for key in pattern:
        next(streams[key], None)
    for s in streams.values():
        for _ in s:
            pass


def _mixer_kernel(lp, n_tiles, attn_ref, u_ref, uh_ref, ga_ref, gp_ref, lead_ref, xa_ref, xb_ref,
                  pmix_ref, pscale_ref, wa_hbm, wp_hbm, wo_hbm, g2_ref, wr_ref, br_ref,
                  h2_ref, xp_ref, route_ref, route_t_ref, cnt_ref, run_ref, h2_s, a_s, p_s, m_s,
                  wa_ref, wp_ref, wo_ref, stage_ref, stage_sem):
    i = pl.program_id(0)
    tm = h2_ref.shape[0]
    d = h2_ref.shape[1]
    cw = d // MIX_CHUNKS

    @pl.when(i == 0)
    def _():
        run_ref[...] = jnp.zeros_like(run_ref)
        h2_s[...] = jnp.zeros_like(h2_s)
        for w_hbm, w_ref in ((wa_hbm, wa_ref), (wp_hbm, wp_ref), (wo_hbm, wo_ref)):
            _stage_bf16(w_hbm, w_ref, stage_ref, stage_sem)

    def chain():
        ic = jnp.minimum(i, n_tiles - 1)
        t = (ic * tm + lax.broadcasted_iota(I32, (tm, 1), 0)) % lp
        valid = t >= PAD
        tx = (ic * tm - POOL_HALO + lp + lax.broadcasted_iota(I32, (tm + POOL_HALO, 1), 0)) % lp
        n_rows = (t - PAD + 1).astype(F32)
        gw = u_ref.shape[1] // len(POOL_WINDOWS)
        attn = attn_ref[...]
        aw = d // len(POOL_WINDOWS)
        for gi, w in enumerate(POOL_WINDOWS):
            a_s[:, gi * aw:(gi + 1) * aw] = jnp.dot(attn, wa_ref[:, gi * aw:(gi + 1) * aw],
                                                    preferred_element_type=F32)
            cols = slice(gi * gw, (gi + 1) * gw)
            uext = jnp.concatenate([uh_ref[:, cols], u_ref[:, cols]], axis=0).astype(F32)
            c = jnp.where(tx >= PAD, uext, 0.0)
            s, span = c, 1
            while span < w:
                s = s[span:] + s[:-span]
                span *= 2
            win = s[POOL_HALO + 1 - w:POOL_HALO + 1 - w + tm]
            n_valid = jnp.clip(n_rows, 1.0, float(w))
            mixed = jnp.where(valid, win / n_valid - c[POOL_HALO:], 0.0)
            m_s[:, cols] = (jnp.dot(mixed.astype(BF16), pmix_ref[gi], preferred_element_type=F32)
                            * pscale_ref[:, cols]).astype(BF16)
            yield
        pool = m_s[:, :u_ref.shape[1]]
        for c in range(MIX_CHUNKS):
            cols = slice(c * cw, (c + 1) * cw)
            p_s[:, cols] = jnp.dot(pool, wp_ref[:, cols], preferred_element_type=F32)
            yield
        nb = lp // BLOCK
        halves = [jnp.where((2 * ic + hf) % nb == 0, lead_ref[...], ref[...])
                  for hf, ref in enumerate((xa_ref, xb_ref))]
        h2 = jnp.concatenate(halves, axis=0)
        for c in range(MIX_CHUNKS):
            cols = slice(c * cw, (c + 1) * cw)
            merged = (jax.nn.sigmoid(ga_ref[:, cols].astype(F32)) * a_s[:, cols]
                      + jax.nn.sigmoid(gp_ref[:, cols].astype(F32)) * p_s[:, cols]).astype(BF16)
            yield
            h2 = h2 + jnp.dot(merged, wo_ref[cols, :], preferred_element_type=F32)
            yield
        h2_ref[...] = h2
        h2_s[...] = h2
        yield

    def tail():
        h2p = h2_s[...]
        tp = ((i - 1) * tm + lp + lax.broadcasted_iota(I32, (tm, 1), 0)) % lp
        valid_p = (tp >= PAD) & (i >= 1)
        ms = jnp.mean(h2p * h2p, axis=-1, keepdims=True)
        inv = lax.rsqrt(ms + RMS_EPS)
        yield
        half = d // 2
        hi_prod = jnp.zeros((tm, 2 * LANES), F32)
        lo_prod = jnp.zeros((tm, LANES), F32)
        n_pack = MIX_CHUNKS // 2
        pw = half // n_pack
        for c in range(n_pack):
            parts = []
            for base in (c * pw, half + c * pw):
                hn = h2p[:, base:base + pw] * inv * g2_ref[:, base:base + pw]
                x_hi = hn.astype(BF16)
                x_hi32 = x_hi.astype(F32)
                x_lo = (hn - x_hi32).astype(BF16)
                hi_prod = hi_prod + jnp.dot(x_hi, wr_ref[base:base + pw, :], preferred_element_type=F32)
                lo_prod = lo_prod + jnp.dot(x_lo, wr_ref[base:base + pw, :LANES], preferred_element_type=F32)
                parts.append(lax.bitcast_convert_type(x_hi32, U32))
            xp_ref[:, c * pw:(c + 1) * pw] = (parts[1] & jnp.uint32(0xFFFF0000)) | (parts[0] >> 16)
            yield
        logits = hi_prod[:, :LANES] + (hi_prod[:, LANES:] + lo_prod) + br_ref[...]
        col = lax.broadcasted_iota(I32, logits.shape, 1).astype(F32)
        neg = -jnp.inf
        gl = jnp.where(col < N_GROUPS, logits, neg)
        gmax = jnp.max(gl, axis=-1, keepdims=True)
        grp = jnp.min(jnp.where(gl == gmax, col, float(LANES)), axis=-1, keepdims=True)
        p_grp = 1.0 / jnp.sum(jnp.exp(gl - gmax), axis=-1, keepdims=True)
        yield
        e_lo = N_GROUPS + grp * EXPERTS_PER_GROUP
        el = jnp.where((col >= e_lo) & (col < e_lo + EXPERTS_PER_GROUP), logits, neg)
        m1 = jnp.max(el, axis=-1, keepdims=True)
        i1 = jnp.min(jnp.where(el == m1, col, float(LANES)), axis=-1, keepdims=True)
        el2 = jnp.where(col == i1, neg, el)
        m2 = jnp.max(el2, axis=-1, keepdims=True)
        i2 = jnp.min(jnp.where(el2 == m2, col, float(LANES)), axis=-1, keepdims=True)
        z = jnp.exp(m2 - m1)
        w1 = p_grp / (1.0 + z)
        w2 = p_grp * z / (1.0 + z)
        e1 = i1 - N_GROUPS
        e2 = i2 - N_GROUPS
        yield
        oh1 = jnp.where((col == e1) & valid_p, 1.0, 0.0)
        oh2 = jnp.where((col == e2) & valid_p, 1.0, 0.0)
        lower = (lax.broadcasted_iota(I32, (tm, tm), 0) > lax.broadcasted_iota(I32, (tm, tm), 1))
        lower = jnp.where(lower, 1.0, 0.0).astype(BF16)
        before1 = jnp.dot(lower, oh1.astype(BF16), preferred_element_type=F32)
        before2 = jnp.dot(lower, oh2.astype(BF16), preferred_element_type=F32)
        tot1 = jnp.sum(oh1, axis=0, keepdims=True)
        tot2 = jnp.sum(oh2, axis=0, keepdims=True)
        run = run_ref[...]
        r1 = jnp.sum(oh1 * (run + before1), axis=-1, keepdims=True)
        r2 = jnp.sum(oh2 * (run + tot1 + before2), axis=-1, keepdims=True)
        run = run + tot1 + tot2
        run_ref[...] = run
        cnt_ref[...] = run
        yield
        slab = jnp.zeros(logits.shape, F32)
        for cidx, val in ((COL_E0, jnp.where(valid_p, e1, -1.0)),
                          (COL_E1, jnp.where(valid_p, e2, -1.0)),
                          (COL_W0, w1), (COL_W1, w2), (COL_R0, r1), (COL_R1, r2)):
            slab = jnp.where(col == cidx, val, slab)
        route_ref[...] = slab
        route_t_ref[...] = slab.T[:ROUTE_ROWS, :]
        yield

    _interleave("cccc" + "ctctctct" + "cctcctcctcc", c=chain(), t=tail())


def _mixer(lp, attn, u, ga, gp, lead, x2, pmix, pscale, wa, wp, wo, g2, wr, br):
    rows = attn.shape[0]
    d = x2.shape[1]
    tm = ROW_TILE
    nb = lp // BLOCK
    n_tiles = rows // tm
    halo_blocks = tm // POOL_HALO
    any_space = pl.BlockSpec(memory_space=pl.ANY)
    cur = lambda i: (jnp.minimum(i, n_tiles - 1), 0)
    prev = lambda i: (jnp.maximum(i - 1, 0), 0)

    def x_spec(half):
        def index(i):
            g = 2 * jnp.minimum(i, n_tiles - 1) + half
            return ((g // nb) * (nb - 1) + jnp.maximum(g % nb - 1, 0), 0)
        return pl.BlockSpec((BLOCK, d), index)

    return pl.pallas_call(
        functools.partial(_mixer_kernel, lp, n_tiles),
        grid=(n_tiles + 1,),
        in_specs=[pl.BlockSpec((tm, attn.shape[1]), cur),
                  pl.BlockSpec((tm, u.shape[1]), cur),
                  pl.BlockSpec((POOL_HALO, u.shape[1]),
                               lambda i: (jnp.maximum(jnp.minimum(i, n_tiles - 1) * halo_blocks - 1, 0), 0)),
                  pl.BlockSpec((tm, d), cur),
                  pl.BlockSpec((tm, d), cur),
                  _resident(lead.shape), x_spec(0), x_spec(1),
                  _resident(pmix.shape), _resident(pscale.shape), any_space, any_space, any_space,
                  _resident(g2.shape), _resident(wr.shape), _resident(br.shape)],
        out_specs=[pl.BlockSpec((tm, d), cur),
                   pl.BlockSpec((tm, d // 2), prev),
                   pl.BlockSpec((tm, LANES), prev),
                   pl.BlockSpec((ROUTE_ROWS, tm), lambda i: (0, jnp.maximum(i - 1, 0))),
                   pl.BlockSpec((1, LANES), lambda i: (0, 0))],
        out_shape=[jax.ShapeDtypeStruct((rows, d), F32),
                   jax.ShapeDtypeStruct((rows, d // 2), U32),
                   jax.ShapeDtypeStruct((rows, LANES), F32),
                   jax.ShapeDtypeStruct((ROUTE_ROWS, rows), F32),
                   jax.ShapeDtypeStruct((1, LANES), F32)],
        scratch_shapes=[pltpu.VMEM((1, LANES), F32), pltpu.VMEM((tm, d), F32),
                        pltpu.VMEM((tm, d), F32), pltpu.VMEM((tm, d), F32), pltpu.VMEM((tm, d), BF16),
                        pltpu.VMEM(wa.shape, BF16), pltpu.VMEM(wp.shape, BF16), pltpu.VMEM(wo.shape, BF16),
                        pltpu.VMEM((d, STAGE_COLS), F32), pltpu.SemaphoreType.DMA(())],
        compiler_params=_params(("arbitrary",)),
        name="mixer",
    )(attn, u, u, ga, gp, lead, x2, x2, pmix, pscale, wa, wp, wo, g2, wr, br)


def _zero_unused_slots(gap_lo_ref, gap_hi_ref, used_ref, xs_ref, zero_ref, sem):
    bm = zero_ref.shape[0]
    n_rows = xs_ref.shape[0]
    n_blocks = (n_rows - SPARE_ROWS) // bm
    zero_ref[...] = jnp.zeros_like(zero_ref)

    def piece(start, size):
        return pltpu.make_async_copy(zero_ref.at[pl.ds(0, size), :], xs_ref.at[pl.ds(start, size), :], sem)

    def for_each_piece(op):
        def tail(e, carry):
            lo, hi = gap_lo_ref[e], gap_hi_ref[e]
            length = hi - lo
            for bit in range(int(math.log2(bm)) - 1, 2, -1):
                size = 1 << bit
                above = (length >> (bit + 1)) << (bit + 1)

                @pl.when((length >> bit) & 1 == 1)
                def _():
                    op(piece(pl.multiple_of(hi - above - size, 8), size))
            for j in range(7):
                @pl.when(j < (length & 7))
                def _():
                    op(piece(lo + j, 1))
            return carry

        def block(b, carry):
            op(piece(pl.multiple_of(b * bm, bm), bm))
            return carry

        lax.fori_loop(0, N_EXPERTS, tail, 0)
        lax.fori_loop(used_ref[0], n_blocks + SPARE_ROWS // bm, block, 0)

    for_each_piece(lambda c: c.start())
    for_each_piece(lambda c: c.wait())


def _scatter_kernel(gap_lo_ref, gap_hi_ref, used_ref, pos0_ref, pos1_ref, x_ref, xs_ref, stage_ref, zero_ref,
                    sem, zsem):
    i = pl.program_id(0)
    last = pl.num_programs(0) - 1
    tm = x_ref.shape[0]
    slot = i % 2

    @pl.when(i == 0)
    def _():
        _zero_unused_slots(gap_lo_ref, gap_hi_ref, used_ref, xs_ref, zero_ref, zsem)

    def retire(s):
        for k in range(2):
            pltpu.make_async_copy(stage_ref.at[s], xs_ref.at[pl.ds(0, tm), :], sem.at[s]).wait()

    @pl.when(i >= 2)
    def _():
        retire(slot)

    stage_ref[slot] = x_ref[...]

    for r in range(tm):
        for k in range(2):
            p = (pos0_ref, pos1_ref)[k][i * tm + r]
            pltpu.make_async_copy(stage_ref.at[slot, pl.ds(r, 1), :], xs_ref.at[pl.ds(p, 1), :],
                                  sem.at[slot]).start(priority=k)

    @pl.when(i == last)
    def _():
        @pl.when(i >= 1)
        def _():
            retire(1 - slot)
        retire(slot)


def _scatter(gap_lo, gap_hi, n_used, pos, xp, slots):
    rows, half = xp.shape
    tm = ROW_TILE
    assert SPARE_ROWS % EXPERT_ROWS == 0 and slots % EXPERT_ROWS == 0
    return pl.pallas_call(
        _scatter_kernel,
        grid_spec=pltpu.PrefetchScalarGridSpec(
            num_scalar_prefetch=5,
            grid=(rows // tm,),
            in_specs=[pl.BlockSpec((tm, half), lambda i, *_: (i, 0))],
            out_specs=pl.BlockSpec(memory_space=pl.ANY),
            scratch_shapes=[pltpu.VMEM((2, tm, half), U32), pltpu.VMEM((EXPERT_ROWS, half), U32),
                            pltpu.SemaphoreType.DMA((2,)), pltpu.SemaphoreType.DMA(())],
        ),
        out_shape=jax.ShapeDtypeStruct((slots + SPARE_ROWS, half), U32),
        compiler_params=_params(("arbitrary",)),
        name="scatter",
    )(gap_lo, gap_hi, n_used, pos[0], pos[1], xp)


def _pack_halves(x):
    n = x.shape[1] // 2
    r = x.astype(BF16).astype(F32)
    lo = lax.bitcast_convert_type(r[:, :n], U32)
    hi = lax.bitcast_convert_type(r[:, n:], U32)
    return (hi & jnp.uint32(0xFFFF0000)) | (lo >> 16)


def _unpack_halves(p):
    return (lax.bitcast_convert_type(p << 16, F32),
            lax.bitcast_convert_type(p & jnp.uint32(0xFFFF0000), F32))


def _expert_kernel(be_ref, nrow_ref, last_ref, ord_ref, used_ref, xs_ref, wg_hbm, wu_hbm, wd_hbm, ys_ref,
                   wg_f, wu_f, wd_f, sem):
    b = pl.program_id(0)
    n_valid_rows = nrow_ref[b]
    expert = be_ref[b]
    ordinal = ord_ref[b]
    n_slots = wg_f.shape[0]
    slot = ordinal % n_slots

    def weight_copies(e, s):
        return [pltpu.make_async_copy(hbm.at[e], buf.at[s], sem.at[s, n])
                for n, (hbm, buf) in enumerate(((wg_hbm, wg_f), (wu_hbm, wu_f), (wd_hbm, wd_f)))]

    def start_fetch(o):
        e = used_ref[o]

        @pl.when(e >= 0)
        def _():
            for c in weight_copies(e, o % n_slots):
                c.start()

    @pl.when(b == 0)
    def _():
        for o in range(WEIGHT_LOOKAHEAD):
            start_fetch(o)

    @pl.when((b == 0) | (expert != be_ref[jnp.maximum(b - 1, 0)]))
    def _():
        start_fetch(ordinal + WEIGHT_LOOKAHEAD)
        for c in weight_copies(expert, slot):
            c.wait()

    @pl.when(n_valid_rows == 0)
    def _():
        ys_ref[...] = jnp.zeros_like(ys_ref)

    @pl.when(n_valid_rows > 0)
    def _():
        lo, hi = _unpack_halves(xs_ref[...])
        x = jnp.concatenate([lo, hi], axis=1).astype(BF16)
        gate = jnp.dot(x, wg_f[slot].astype(BF16), preferred_element_type=F32)
        up = jnp.dot(x, wu_f[slot].astype(BF16), preferred_element_type=F32)
        hb = (jax.nn.silu(gate) * up).astype(BF16)
        ys_ref[...] = _pack_halves(jnp.dot(hb, wd_f[slot].astype(BF16), preferred_element_type=F32))


def _experts(block_expert, block_rows, last_block, block_ord, used_experts, xs, w_gate, w_up, w_down):
    half = xs.shape[1]
    n_exp, d, de = w_gate.shape
    bm = EXPERT_ROWS
    n_blocks = block_expert.shape[0]
    slots = n_blocks * bm
    n_slots = WEIGHT_LOOKAHEAD + 1
    any_space = pl.BlockSpec(memory_space=pl.ANY)
    return pl.pallas_call(
        _expert_kernel,
        grid_spec=pltpu.PrefetchScalarGridSpec(
            num_scalar_prefetch=5,
            grid=(n_blocks,),
            in_specs=[pl.BlockSpec((bm, half), lambda b, be, nr, last, *_: (jnp.minimum(b, last[0]), 0)),
                      any_space, any_space, any_space],
            out_specs=pl.BlockSpec((bm, half), lambda b, *_: (b, 0)),
            scratch_shapes=[pltpu.VMEM((n_slots, d, de), F32), pltpu.VMEM((n_slots, d, de), F32),
                            pltpu.VMEM((n_slots, de, d), F32),
                            pltpu.SemaphoreType.DMA((n_slots, 3))],
        ),
        out_shape=jax.ShapeDtypeStruct((slots, half), U32),
        compiler_params=_params(("arbitrary",)),
        name="experts",
    )(block_expert, block_rows, last_block, block_ord, used_experts, xs, w_gate, w_up, w_down)


def _combine_kernel(nb, per_batch, pos0_ref, pos1_ref, h2_ref, route_ref, g_ref, ys_ref, o_ref, y_ref, sem):
    t = pl.program_id(0)
    tm = h2_ref.shape[0]

    def issue(tile, slot):
        base = ((tile // per_batch) * nb + 1 + tile % per_batch) * tm

        for r in range(tm):
            for k in range(2):
                p = (pos0_ref, pos1_ref)[k][base + r]
                pltpu.make_async_copy(ys_ref.at[pl.ds(p, 1), :], y_ref.at[slot, k, pl.ds(r, 1), :],
                                      sem.at[slot]).start(priority=k)

    @pl.when(t == 0)
    def _():
        issue(0, 0)

    @pl.when(t + 1 < pl.num_programs(0))
    def _():
        issue(t + 1, (t + 1) % 2)

    slot = t % 2
    for k in range(2):
        pltpu.make_async_copy(ys_ref.at[pl.ds(0, tm), :], y_ref.at[slot, k], sem.at[slot]).wait()
    route = route_ref[...]
    w0 = route[:, COL_W0:COL_W0 + 1]
    w1 = route[:, COL_W1:COL_W1 + 1]
    half = y_ref.shape[-1]
    y0 = _unpack_halves(y_ref[slot, 0])
    y1 = _unpack_halves(y_ref[slot, 1])
    hs = [h2_ref[:, s * half:(s + 1) * half] + (w0 * y0[s] + w1 * y1[s]) for s in range(2)]
    ms = sum(jnp.sum(h * h, axis=-1, keepdims=True) for h in hs) / (2 * half)
    inv = lax.rsqrt(ms + RMS_EPS)
    for s in range(2):
        o_ref[0, :, s * half:(s + 1) * half] = hs[s] * inv * g_ref[:, s * half:(s + 1) * half]


def _combine(pos, h2, route, g, ys, batch, nb, seq):
    rows, d = h2.shape
    tm = BLOCK
    per_batch = seq // tm
    tile = lambda t, *_: ((t // per_batch) * nb + 1 + t % per_batch, 0)
    return pl.pallas_call(
        functools.partial(_combine_kernel, nb, per_batch),
        grid_spec=pltpu.PrefetchScalarGridSpec(
            num_scalar_prefetch=2,
            grid=(batch * per_batch,),
            in_specs=[pl.BlockSpec((tm, d), tile),
                      pl.BlockSpec((tm, LANES), tile),
                      pl.BlockSpec((1, d), lambda t, *_: (0, 0)),
                      pl.BlockSpec(memory_space=pl.ANY)],
            out_specs=pl.BlockSpec((1, tm, d), lambda t, *_: (t // per_batch, t % per_batch, 0)),
            scratch_shapes=[pltpu.VMEM((2, 2, tm, ys.shape[1]), U32), pltpu.SemaphoreType.DMA((2,))],
        ),
        out_shape=jax.ShapeDtypeStruct((batch, seq, d), F32),
        compiler_params=_params(("arbitrary",)),
        name="combine",
    )(pos[0], pos[1], h2, route, g, ys)


def kernel(x, meta_tokens, rel_bias, norm_mix, w_in, attn_sinks, pool_mix, pool_scale,
           w_attn_branch, w_pool_branch, w_out, norm_ffn, w_router_group, b_router_group,
           w_router_expert, b_router_expert, w_gate, w_up, w_down, norm_final):
    batch, seq, d = x.shape
    depth = w_in.shape[0]
    assert depth == 1, "single-layer stack"
    aw = w_attn_branch.shape[1]
    pw = w_pool_branch.shape[1]
    kvw = 2 * N_KV_HEADS * HEAD_DIM
    assert aw == N_Q_HEADS * HEAD_DIM and w_in.shape[2] == aw + kvw + pw + 2 * d
    assert seq % BLOCK == 0
    lp = seq + BLOCK
    nb = lp // BLOCK
    rows = batch * lp
    assert rows % ROW_TILE == 0

    assert ROW_TILE == 2 * BLOCK
    lead = jnp.concatenate([jnp.zeros((PAD, d), x.dtype), meta_tokens.astype(x.dtype)], axis=0)
    x2 = x.reshape(batch * seq, d)

    q, kv, u, ga, gp = _inproj(lead, x2, nb, norm_mix[0][None], w_in[0], (aw, kvw, pw, d, d))

    attn = _attention(q, kv, _attn_bias_tables(rel_bias), attn_sinks[0].astype(F32), batch, nb)

    n_router = N_GROUPS + N_EXPERTS
    wr = jnp.concatenate([w_router_group[0], w_router_expert[0], jnp.zeros((d, LANES - n_router), F32)], axis=1)
    br = jnp.concatenate([b_router_group[0], b_router_expert[0], jnp.zeros((LANES - n_router,), F32)])[None]
    wr_hi = wr.astype(BF16)
    wr_split = jnp.concatenate([wr_hi, (wr - wr_hi.astype(F32)).astype(BF16)], axis=1)
    h2, xp, route, route_t, counts = _mixer(
        lp, attn, u, ga, gp, lead, x2, pool_mix[0].astype(BF16), pool_scale[0][None].astype(F32),
        w_attn_branch[0], w_pool_branch[0], w_out[0], norm_ffn[0][None], wr_split, br)

    bm = EXPERT_ROWS
    n_tok = batch * (seq + N_META)
    n_blocks = (2 * n_tok) // bm + N_EXPERTS
    cnt = counts[0, :N_EXPERTS].astype(I32)
    blocks_e = (cnt + bm - 1) // bm
    bend = jnp.cumsum(blocks_e)
    bstart = bend - blocks_e
    ord_e = jnp.cumsum((blocks_e > 0).astype(I32)) - 1
    n_used = bend[-1]
    last_block = jnp.maximum(n_used - 1, 0)
    blk = jnp.arange(n_blocks, dtype=I32)

    def expert_of(block):
        return jnp.minimum(jnp.sum((bend[None, :] <= block[:, None]).astype(I32), axis=1), N_EXPERTS - 1)

    block_expert = expert_of(jnp.minimum(blk, last_block))
    own = block_expert[:, None] == jnp.arange(N_EXPERTS, dtype=I32)[None, :]
    pick = lambda v: jnp.sum(jnp.where(own, v[None, :], 0), axis=1)
    block_rows = jnp.clip(pick(cnt) - (blk - pick(bstart)) * bm, 0, bm)
    block_rows = jnp.where(blk < n_used, block_rows, 0).astype(I32)
    block_ord = pick(ord_e).astype(I32)
    experts = jnp.arange(N_EXPERTS, dtype=I32)
    is_kth = (ord_e[None, :] == jnp.arange(N_EXPERTS + WEIGHT_LOOKAHEAD, dtype=I32)[:, None]) & (blocks_e > 0)[None, :]
    used_experts = jnp.where(jnp.any(is_kth, axis=1), jnp.sum(jnp.where(is_kth, experts[None, :], 0), axis=1), -1)
    gap_lo = (bstart * bm + cnt).astype(I32)
    gap_hi = (bend * bm).astype(I32)
    e = route_t[COL_E0:COL_E1 + 1].astype(I32)
    rank = route_t[COL_R0:COL_R1 + 1].astype(I32)
    first_row = jnp.sum(jnp.where(e[:, None, :] == experts[None, :, None], (bstart * bm)[None, :, None], 0), axis=1)
    spare = n_blocks * bm + (2 * jnp.arange(rows, dtype=I32)[None, :] + jnp.arange(2, dtype=I32)[:, None]) % SPARE_ROWS
    pos = jnp.where(e >= 0, first_row + rank, spare).astype(I32)

    xs = _scatter(gap_lo, gap_hi, n_used.astype(I32).reshape(1), pos, xp, n_blocks * bm)
    ys = _experts(block_expert.astype(I32), block_rows, last_block.astype(I32).reshape(1), block_ord,
                  used_experts.astype(I32), xs, w_gate[0], w_up[0], w_down[0])
    return _combine(pos, h2, route, norm_final[None].astype(F32), ys, batch, nb, seq)
```

```python
import functools
import math

import numpy as np
import jax
import jax.numpy as jnp
from jax import lax
from jax.experimental import pallas as pl
from jax.experimental.pallas import tpu as pltpu

F32 = jnp.float32
BF16 = jnp.bfloat16
I32 = jnp.int32
U32 = jnp.uint32

BLOCK = 128
N_META = 16
PAD = BLOCK - N_META
HEAD_DIM = 64
N_KV_HEADS = 2
Q_PER_KV = 8
N_Q_HEADS = N_KV_HEADS * Q_PER_KV
WINDOW = 128
POOL_WINDOWS = (2, 4, 8, 16)
POOL_HALO = 16
N_BUCKETS = 32
MAX_DISTANCE = 128
N_GROUPS = 8
EXPERTS_PER_GROUP = 8
N_EXPERTS = N_GROUPS * EXPERTS_PER_GROUP
RMS_EPS = 1e-6
LANES = 128
VMEM_LIMIT = 56 * 1024 * 1024

ROW_TILE = 256
EXPERT_ROWS = 256
STAGE_COLS = 640
WEIGHT_LOOKAHEAD = 2
SPARE_ROWS = 4 * ROW_TILE
COL_E0, COL_E1, COL_W0, COL_W1, COL_R0, COL_R1 = 0, 1, 2, 3, 4, 5
ROUTE_ROWS = 8


def _params(sem, vmem=VMEM_LIMIT):
    return pltpu.CompilerParams(dimension_semantics=sem, vmem_limit_bytes=vmem)


def _resident(shape):
    nd = len(shape)
    return pl.BlockSpec(shape, lambda *_: (0,) * nd, pipeline_mode=pl.Buffered(1))


def _tile_block_specs(nb, per_batch, d):
    def spec(half):
        def index(i, *_):
            g = 2 * i + half
            return ((g // nb) * per_batch + jnp.maximum(g % nb - 1, 0), 0)
        return pl.BlockSpec((BLOCK, d), index)
    return [spec(0), spec(1)]


def _tile_rows(nb, lead_ref, xa_ref, xb_ref):
    i = pl.program_id(0)
    halves = [jnp.where((2 * i + half) % nb == 0, lead_ref[...], ref[...])
              for half, ref in enumerate((xa_ref, xb_ref))]
    return jnp.concatenate(halves, axis=0)


def _stage_bf16(w_hbm, w_ref, stage_ref, sem):
    rows, n = w_hbm.shape
    chunk = stage_ref.shape[2]
    starts = list(range(0, n, chunk))
    widths = [min(chunk, n - c0) for c0 in starts]
    copies = [pltpu.make_async_copy(w_hbm.at[:, pl.ds(c0, cw)],
                                    stage_ref.at[k % 2, pl.ds(0, rows), pl.ds(0, cw)], sem.at[k % 2])
              for k, (c0, cw) in enumerate(zip(starts, widths))]
    copies[0].start()
    for k, (c0, cw) in enumerate(zip(starts, widths)):
        if k + 1 < len(copies):
            copies[k + 1].start()
        copies[k].wait()
        w_ref[:, c0:c0 + cw] = stage_ref[k % 2, 0:rows, 0:cw].astype(BF16)


def _inproj_kernel(nb, lead_ref, xa_ref, xb_ref, g_ref, w_hbm, q_ref, kv_ref, u_ref, ga_ref, gp_ref,
                   w_ref, stage_ref, sem):
    @pl.when(pl.program_id(0) == 0)
    def _():
        _stage_bf16(w_hbm, w_ref, stage_ref, sem)

    x = _tile_rows(nb, lead_ref, xa_ref, xb_ref)
    ms = jnp.mean(x * x, axis=-1, keepdims=True)
    hn = (x * lax.rsqrt(ms + RMS_EPS) * g_ref[...]).astype(BF16)
    off = 0
    for ref in (q_ref, kv_ref, u_ref, ga_ref, gp_ref):
        width = ref.shape[1]
        for c in range(0, width, 1024):
            cw = min(1024, width - c)
            ref[:, c:c + cw] = jnp.dot(hn, w_ref[:, off + c:off + c + cw],
                                       preferred_element_type=F32).astype(BF16)
        off += width


def _inproj(lead, x2, nb, g, w, widths):
    d = x2.shape[1]
    tm = ROW_TILE
    per_batch = nb - 1
    rows = x2.shape[0] // per_batch * nb
    outs = [jax.ShapeDtypeStruct((rows, wd), BF16) for wd in widths]
    return pl.pallas_call(
        functools.partial(_inproj_kernel, nb),
        grid=(rows // tm,),
        in_specs=[_resident(lead.shape)] + _tile_block_specs(nb, per_batch, d)
                 + [_resident((1, d)), pl.BlockSpec(memory_space=pl.ANY)],
        out_specs=[pl.BlockSpec((tm, wd), lambda i: (i, 0)) for wd in widths],
        out_shape=outs,
        scratch_shapes=[pltpu.VMEM(w.shape, BF16), pltpu.VMEM((2, w.shape[0], STAGE_COLS), F32),
                        pltpu.SemaphoreType.DMA((2,))],
        compiler_params=_params(("arbitrary",)),
        name="inproj",
    )(lead, x2, x2, g, w)


PAIR = 2 * HEAD_DIM
KEYS = 3 * BLOCK
N_PAIRS = N_Q_HEADS // 2


def _attn_kernel(sink_ref, q_ref, kvp_ref, kvc_ref, kvm_ref, bias_ref, o_ref):
    kw = N_KV_HEADS * HEAD_DIM
    zeros = jnp.zeros((KEYS, HEAD_DIM), BF16)
    ones = jnp.ones((KEYS, HEAD_DIM), BF16)
    lane = lax.broadcasted_iota(I32, (BLOCK, PAIR), 1)
    scale = jnp.asarray(HEAD_DIM ** -0.5, BF16)
    for hk in range(N_KV_HEADS):
        ks = slice(hk * HEAD_DIM, (hk + 1) * HEAD_DIM)
        vs = slice(kw + hk * HEAD_DIM, kw + (hk + 1) * HEAD_DIM)
        k3 = jnp.concatenate([kvp_ref[:, ks], kvc_ref[:, ks], kvm_ref[:, ks]], axis=0) * scale
        v3 = jnp.concatenate([kvp_ref[:, vs], kvc_ref[:, vs], kvm_ref[:, vs]], axis=0)
        kbd = jnp.concatenate([jnp.concatenate([k3, zeros], axis=1),
                               jnp.concatenate([zeros, k3], axis=1)], axis=0)
        vext = jnp.concatenate([jnp.concatenate([v3, zeros, ones, zeros], axis=1),
                                jnp.concatenate([zeros, v3, zeros, ones], axis=1)], axis=0)
        for jp in range(Q_PER_KV // 2):
            j = hk * (Q_PER_KV // 2) + jp
            qp = q_ref[:, j * PAIR:(j + 1) * PAIR]
            s = lax.dot_general(qp, kbd, (((1,), (1,)), ((), ())),
                                preferred_element_type=F32) + bias_ref[0, j]
            probs, sink_terms = [], []
            for side in range(2):
                ss = s[:, side * KEYS:(side + 1) * KEYS]
                sink = sink_ref[2 * j + side]
                mx = jnp.maximum(jnp.max(ss, axis=-1, keepdims=True), sink)
                probs.append(jnp.exp(ss - mx))
                sink_terms.append(jnp.exp(sink - mx))
            p = jnp.concatenate(probs, axis=1).astype(BF16)
            r = jnp.dot(p, vext, preferred_element_type=F32)
            den = r[:, PAIR:] + jnp.where(lane < HEAD_DIM, sink_terms[0], sink_terms[1])
            o_ref[:, j * PAIR:(j + 1) * PAIR] = (r[:, :PAIR] / den).astype(o_ref.dtype)


def _attention(q, kv, bias, sinks, batch, nb):
    rows, aw = q.shape
    kvw = kv.shape[1]
    return pl.pallas_call(
        _attn_kernel,
        grid=(batch, nb),
        in_specs=[pl.BlockSpec(memory_space=pltpu.SMEM),
                  pl.BlockSpec((BLOCK, aw), lambda b, n: (b * nb + n, 0)),
                  pl.BlockSpec((BLOCK, kvw), lambda b, n: (b * nb + jnp.maximum(n - 1, 0), 0)),
                  pl.BlockSpec((BLOCK, kvw), lambda b, n: (b * nb + n, 0)),
                  pl.BlockSpec((BLOCK, kvw), lambda b, n: (b * nb, 0)),
                  pl.BlockSpec((1,) + bias.shape[1:], lambda b, n: (jnp.minimum(n, 2), 0, 0, 0))],
        out_specs=pl.BlockSpec((BLOCK, aw), lambda b, n: (b * nb + n, 0)),
        out_shape=jax.ShapeDtypeStruct((rows, aw), BF16),
        compiler_params=_params(("parallel", "arbitrary")),
        name="attention",
    )(sinks, q, kv, kv, kv, bias)


def _attn_bias_tables(rel_bias):
    max_exact = N_BUCKETS // 2
    qi = np.arange(BLOCK)[:, None]
    kj = np.arange(2 * BLOCK)[None, :]
    mj = np.arange(BLOCK)[None, :]
    dist = qi + BLOCK - kj
    d = np.maximum(dist, 0)
    large = max_exact + (np.log(np.maximum(d, 1).astype(np.float32) / max_exact)
                         / math.log(MAX_DISTANCE / max_exact) * (N_BUCKETS - max_exact)).astype(np.int32)
    bucket = np.where(d < max_exact, d, np.minimum(large, N_BUCKETS - 1))
    onehot = (bucket[..., None] == np.arange(N_BUCKETS)).astype(np.float32)
    rb = rel_bias.astype(F32)
    win = jnp.einsum("qkb,bh->hqk", onehot, rb, precision=lax.Precision.HIGHEST)
    meta = jnp.broadcast_to(rb[N_BUCKETS - 1][:, None, None], (N_Q_HEADS, BLOCK, BLOCK))
    full = jnp.concatenate([win, meta], axis=2)
    masks = []
    for n in range(3):
        win_ok = (dist >= 0) & (dist < WINDOW) & ((n - 1) * BLOCK + kj >= PAD)
        meta_ok = (mj >= PAD) & (n * BLOCK + qi - mj >= WINDOW)
        masks.append(np.concatenate([win_ok, meta_ok], axis=1))
    t = jnp.where(np.stack(masks)[:, None], full[None], -jnp.inf)
    t = t.reshape(3, N_PAIRS, 2, BLOCK, KEYS).transpose(0, 1, 3, 2, 4)
    return t.reshape(3, N_PAIRS, BLOCK, 2 * KEYS)


MIX_CHUNKS = 4


def _interleave(pattern, **streams):
    for key in pattern:
        next(streams[key], None)
    for s in streams.values():
        for _ in s:
            pass


def _mixer_kernel(lp, n_tiles, attn_ref, u_ref, uh_ref, ga_ref, gp_ref, lead_ref, xa_ref, xb_ref,
                  pmix_ref, pscale_ref, wa_hbm, wp_hbm, wo_hbm, g2_ref, wr_ref, br_ref,
                  h2_ref, xp_ref, route_ref, route_t_ref, cnt_ref, run_ref, h2_s, a_s, p_s, m_s,
                  wa_ref, wp_ref, wo_ref, stage_ref, stage_sem):
    i = pl.program_id(0)
    tm = h2_ref.shape[0]
    d = h2_ref.shape[1]
    cw = d // MIX_CHUNKS

    @pl.when(i == 0)
    def _():
        run_ref[...] = jnp.zeros_like(run_ref)
        h2_s[...] = jnp.zeros_like(h2_s)
        for w_hbm, w_ref in ((wa_hbm, wa_ref), (wp_hbm, wp_ref), (wo_hbm, wo_ref)):
            _stage_bf16(w_hbm, w_ref, stage_ref, stage_sem)

    def chain():
        ic = jnp.minimum(i, n_tiles - 1)
        t = (ic * tm + lax.broadcasted_iota(I32, (tm, 1), 0)) % lp
        valid = t >= PAD
        tx = (ic * tm - POOL_HALO + lp + lax.broadcasted_iota(I32, (tm + POOL_HALO, 1), 0)) % lp
        n_rows = (t - PAD + 1).astype(F32)
        gw = u_ref.shape[1] // len(POOL_WINDOWS)
        attn = attn_ref[...]
        aw = d // len(POOL_WINDOWS)
        for gi, w in enumerate(POOL_WINDOWS):
            a_s[:, gi * aw:(gi + 1) * aw] = jnp.dot(attn, wa_ref[:, gi * aw:(gi + 1) * aw],
                                                    preferred_element_type=F32)
            cols = slice(gi * gw, (gi + 1) * gw)
            uext = jnp.concatenate([uh_ref[:, cols], u_ref[:, cols]], axis=0).astype(F32)
            c = jnp.where(tx >= PAD, uext, 0.0)
            s, span = c, 1
            while span < w:
                s = s[span:] + s[:-span]
                span *= 2
            win = s[POOL_HALO + 1 - w:POOL_HALO + 1 - w + tm]
            n_valid = jnp.clip(n_rows, 1.0, float(w))
            mixed = jnp.where(valid, win / n_valid - c[POOL_HALO:], 0.0)
            m_s[:, cols] = (jnp.dot(mixed.astype(BF16), pmix_ref[gi], preferred_element_type=F32)
                            * pscale_ref[:, cols]).astype(BF16)
            yield
        pool = m_s[:, :u_ref.shape[1]]
        for c in range(MIX_CHUNKS):
            cols = slice(c * cw, (c + 1) * cw)
            p_s[:, cols] = jnp.dot(pool, wp_ref[:, cols], preferred_element_type=F32)
            yield
        nb = lp // BLOCK
        halves = [jnp.where((2 * ic + hf) % nb == 0, lead_ref[...], ref[...])
                  for hf, ref in enumerate((xa_ref, xb_ref))]
        h2 = jnp.concatenate(halves, axis=0)
        for c in range(MIX_CHUNKS):
            cols = slice(c * cw, (c + 1) * cw)
            merged = (jax.nn.sigmoid(ga_ref[:, cols].astype(F32)) * a_s[:, cols]
                      + jax.nn.sigmoid(gp_ref[:, cols].astype(F32)) * p_s[:, cols]).astype(BF16)
            yield
            h2 = h2 + jnp.dot(merged, wo_ref[cols, :], preferred_element_type=F32)
            yield
        h2_ref[...] = h2
        h2_s[...] = h2
        yield

    def tail():
        h2p = h2_s[...]
        tp = ((i - 1) * tm + lp + lax.broadcasted_iota(I32, (tm, 1), 0)) % lp
        valid_p = (tp >= PAD) & (i >= 1)
        ms = jnp.mean(h2p * h2p, axis=-1, keepdims=True)
        inv = lax.rsqrt(ms + RMS_EPS)
        yield
        half = d // 2
        hi_prod = jnp.zeros((tm, 2 * LANES), F32)
        lo_prod = jnp.zeros((tm, LANES), F32)
        n_pack = MIX_CHUNKS // 2
        pw = half // n_pack
        for c in range(n_pack):
            parts = []
            for base in (c * pw, half + c * pw):
                hn = h2p[:, base:base + pw] * inv * g2_ref[:, base:base + pw]
                x_hi = hn.astype(BF16)
                x_hi32 = x_hi.astype(F32)
                x_lo = (hn - x_hi32).astype(BF16)
                hi_prod = hi_prod + jnp.dot(x_hi, wr_ref[base:base + pw, :], preferred_element_type=F32)
                lo_prod = lo_prod + jnp.dot(x_lo, wr_ref[base:base + pw, :LANES], preferred_element_type=F32)
                parts.append(lax.bitcast_convert_type(x_hi32, U32))
            xp_ref[:, c * pw:(c + 1) * pw] = (parts[1] & jnp.uint32(0xFFFF0000)) | (parts[0] >> 16)
            yield
        logits = hi_prod[:, :LANES] + (hi_prod[:, LANES:] + lo_prod) + br_ref[...]
        col = lax.broadcasted_iota(I32, logits.shape, 1).astype(F32)
        neg = -jnp.inf
        gl = jnp.where(col < N_GROUPS, logits, neg)
        gmax = jnp.max(gl, axis=-1, keepdims=True)
        grp = jnp.min(jnp.where(gl == gmax, col, float(LANES)), axis=-1, keepdims=True)
        p_grp = 1.0 / jnp.sum(jnp.exp(gl - gmax), axis=-1, keepdims=True)
        yield
        e_lo = N_GROUPS + grp * EXPERTS_PER_GROUP
        el = jnp.where((col >= e_lo) & (col < e_lo + EXPERTS_PER_GROUP), logits, neg)
        m1 = jnp.max(el, axis=-1, keepdims=True)
        i1 = jnp.min(jnp.where(el == m1, col, float(LANES)), axis=-1, keepdims=True)
        el2 = jnp.where(col == i1, neg, el)
        m2 = jnp.max(el2, axis=-1, keepdims=True)
        i2 = jnp.min(jnp.where(el2 == m2, col, float(LANES)), axis=-1, keepdims=True)
        z = jnp.exp(m2 - m1)
        w1 = p_grp / (1.0 + z)
        w2 = p_grp * z / (1.0 + z)
        e1 = i1 - N_GROUPS
        e2 = i2 - N_GROUPS
        yield
        oh1 = jnp.where((col == e1) & valid_p, 1.0, 0.0)
        oh2 = jnp.where((col == e2) & valid_p, 1.0, 0.0)
        lower = (lax.broadcasted_iota(I32, (tm, tm), 0) > lax.broadcasted_iota(I32, (tm, tm), 1))
        lower = jnp.where(lower, 1.0, 0.0).astype(BF16)
        before1 = jnp.dot(lower, oh1.astype(BF16), preferred_element_type=F32)
        before2 = jnp.dot(lower, oh2.astype(BF16), preferred_element_type=F32)
        tot1 = jnp.sum(oh1, axis=0, keepdims=True)
        tot2 = jnp.sum(oh2, axis=0, keepdims=True)
        run = run_ref[...]
        r1 = jnp.sum(oh1 * (run + before1), axis=-1, keepdims=True)
        r2 = jnp.sum(oh2 * (run + tot1 + before2), axis=-1, keepdims=True)
        run = run + tot1 + tot2
        run_ref[...] = run
        cnt_ref[...] = run
        yield
        slab = jnp.zeros(logits.shape, F32)
        for cidx, val in ((COL_E0, jnp.where(valid_p, e1, -1.0)),
                          (COL_E1, jnp.where(valid_p, e2, -1.0)),
                          (COL_W0, w1), (COL_W1, w2), (COL_R0, r1), (COL_R1, r2)):
            slab = jnp.where(col == cidx, val, slab)
        route_ref[...] = slab
        route_t_ref[...] = slab.T[:ROUTE_ROWS, :]
        yield

    _interleave("cccc" + "ctctctct" + "cctcctcctcc", c=chain(), t=tail())


def _mixer(lp, attn, u, ga, gp, lead, x2, pmix, pscale, wa, wp, wo, g2, wr, br):
    rows = attn.shape[0]
    d = x2.shape[1]
    tm = ROW_TILE
    nb = lp // BLOCK
    n_tiles = rows // tm
    halo_blocks = tm // POOL_HALO
    any_space = pl.BlockSpec(memory_space=pl.ANY)
    cur = lambda i: (jnp.minimum(i, n_tiles - 1), 0)
    prev = lambda i: (jnp.maximum(i - 1, 0), 0)

    def x_spec(half):
        def index(i):
            g = 2 * jnp.minimum(i, n_tiles - 1) + half
            return ((g // nb) * (nb - 1) + jnp.maximum(g % nb - 1, 0), 0)
        return pl.BlockSpec((BLOCK, d), index)

    return pl.pallas_call(
        functools.partial(_mixer_kernel, lp, n_tiles),
        grid=(n_tiles + 1,),
        in_specs=[pl.BlockSpec((tm, attn.shape[1]), cur),
                  pl.BlockSpec((tm, u.shape[1]), cur),
                  pl.BlockSpec((POOL_HALO, u.shape[1]),
                               lambda i: (jnp.maximum(jnp.minimum(i, n_tiles - 1) * halo_blocks - 1, 0), 0)),
                  pl.BlockSpec((tm, d), cur),
                  pl.BlockSpec((tm, d), cur),
                  _resident(lead.shape), x_spec(0), x_spec(1),
                  _resident(pmix.shape), _resident(pscale.shape), any_space, any_space, any_space,
                  _resident(g2.shape), _resident(wr.shape), _resident(br.shape)],
        out_specs=[pl.BlockSpec((tm, d), cur),
                   pl.BlockSpec((tm, d // 2), prev),
                   pl.BlockSpec((tm, LANES), prev),
                   pl.BlockSpec((ROUTE_ROWS, tm), lambda i: (0, jnp.maximum(i - 1, 0))),
                   pl.BlockSpec((1, LANES), lambda i: (0, 0))],
        out_shape=[jax.ShapeDtypeStruct((rows, d), F32),
                   jax.ShapeDtypeStruct((rows, d // 2), U32),
                   jax.ShapeDtypeStruct((rows, LANES), F32),
                   jax.ShapeDtypeStruct((ROUTE_ROWS, rows), F32),
                   jax.ShapeDtypeStruct((1, LANES), F32)],
        scratch_shapes=[pltpu.VMEM((1, LANES), F32), pltpu.VMEM((tm, d), F32),
                        pltpu.VMEM((tm, d), F32), pltpu.VMEM((tm, d), F32), pltpu.VMEM((tm, d), BF16),
                        pltpu.VMEM(wa.shape, BF16), pltpu.VMEM(wp.shape, BF16), pltpu.VMEM(wo.shape, BF16),
                        pltpu.VMEM((2, d, STAGE_COLS), F32), pltpu.SemaphoreType.DMA((2,))],
        compiler_params=_params(("arbitrary",)),
        name="mixer",
    )(attn, u, u, ga, gp, lead, x2, x2, pmix, pscale, wa, wp, wo, g2, wr, br)


def _zero_unused_slots(gap_lo_ref, gap_hi_ref, used_ref, xs_ref, zero_ref, sem):
    bm = zero_ref.shape[0]
    n_rows = xs_ref.shape[0]
    n_blocks = (n_rows - SPARE_ROWS) // bm
    zero_ref[...] = jnp.zeros_like(zero_ref)

    def piece(start, size):
        return pltpu.make_async_copy(zero_ref.at[pl.ds(0, size), :], xs_ref.at[pl.ds(start, size), :], sem)

    def for_each_piece(op):
        def tail(e, carry):
            lo, hi = gap_lo_ref[e], gap_hi_ref[e]
            length = hi - lo
            for bit in range(int(math.log2(bm)) - 1, 2, -1):
                size = 1 << bit
                above = (length >> (bit + 1)) << (bit + 1)

                @pl.when((length >> bit) & 1 == 1)
                def _():
                    op(piece(pl.multiple_of(hi - above - size, 8), size))
            for j in range(7):
                @pl.when(j < (length & 7))
                def _():
                    op(piece(lo + j, 1))
            return carry

        def block(b, carry):
            op(piece(pl.multiple_of(b * bm, bm), bm))
            return carry

        lax.fori_loop(0, N_EXPERTS, tail, 0)
        lax.fori_loop(used_ref[0], n_blocks + SPARE_ROWS // bm, block, 0)

    for_each_piece(lambda c: c.start())
    for_each_piece(lambda c: c.wait())


def _scatter_kernel(gap_lo_ref, gap_hi_ref, used_ref, pos0_ref, pos1_ref, x_ref, xs_ref, stage_ref, zero_ref,
                    sem, zsem):
    i = pl.program_id(0)
    last = pl.num_programs(0) - 1
    tm = x_ref.shape[0]
    slot = i % 2

    @pl.when(i == 0)
    def _():
        _zero_unused_slots(gap_lo_ref, gap_hi_ref, used_ref, xs_ref, zero_ref, zsem)

    def retire(s):
        for k in range(2):
            pltpu.make_async_copy(stage_ref.at[s], xs_ref.at[pl.ds(0, tm), :], sem.at[s]).wait()

    @pl.when(i >= 2)
    def _():
        retire(slot)

    stage_ref[slot] = x_ref[...]

    for r in range(tm):
        for k in range(2):
            p = (pos0_ref, pos1_ref)[k][i * tm + r]
            pltpu.make_async_copy(stage_ref.at[slot, pl.ds(r, 1), :], xs_ref.at[pl.ds(p, 1), :],
                                  sem.at[slot]).start(priority=k)

    @pl.when(i == last)
    def _():
        @pl.when(i >= 1)
        def _():
            retire(1 - slot)
        retire(slot)


def _scatter(gap_lo, gap_hi, n_used, pos, xp, slots):
    rows, half = xp.shape
    tm = ROW_TILE
    assert SPARE_ROWS % EXPERT_ROWS == 0 and slots % EXPERT_ROWS == 0
    return pl.pallas_call(
        _scatter_kernel,
        grid_spec=pltpu.PrefetchScalarGridSpec(
            num_scalar_prefetch=5,
            grid=(rows // tm,),
            in_specs=[pl.BlockSpec((tm, half), lambda i, *_: (i, 0))],
            out_specs=pl.BlockSpec(memory_space=pl.ANY),
            scratch_shapes=[pltpu.VMEM((2, tm, half), U32), pltpu.VMEM((EXPERT_ROWS, half), U32),
                            pltpu.SemaphoreType.DMA((2,)), pltpu.SemaphoreType.DMA(())],
        ),
        out_shape=jax.ShapeDtypeStruct((slots + SPARE_ROWS, half), U32),
        compiler_params=_params(("arbitrary",)),
        name="scatter",
    )(gap_lo, gap_hi, n_used, pos[0], pos[1], xp)


def _pack_halves(x):
    n = x.shape[1] // 2
    r = x.astype(BF16).astype(F32)
    lo = lax.bitcast_convert_type(r[:, :n], U32)
    hi = lax.bitcast_convert_type(r[:, n:], U32)
    return (hi & jnp.uint32(0xFFFF0000)) | (lo >> 16)


def _unpack_halves(p):
    return (lax.bitcast_convert_type(p << 16, F32),
            lax.bitcast_convert_type(p & jnp.uint32(0xFFFF0000), F32))


def _expert_kernel(be_ref, nrow_ref, last_ref, ord_ref, used_ref, xs_ref, wg_hbm, wu_hbm, wd_hbm, ys_ref,
                   wg_f, wu_f, wd_f, sem):
    b = pl.program_id(0)
    n_valid_rows = nrow_ref[b]
    expert = be_ref[b]
    ordinal = ord_ref[b]
    n_slots = wg_f.shape[0]
    slot = ordinal % n_slots

    def weight_copies(e, s):
        return [pltpu.make_async_copy(hbm.at[e], buf.at[s], sem.at[s, n])
                for n, (hbm, buf) in enumerate(((wg_hbm, wg_f), (wu_hbm, wu_f), (wd_hbm, wd_f)))]

    def start_fetch(o):
        e = used_ref[o]

        @pl.when(e >= 0)
        def _():
            for c in weight_copies(e, o % n_slots):
                c.start()

    @pl.when(b == 0)
    def _():
        for o in range(WEIGHT_LOOKAHEAD):
            start_fetch(o)

    @pl.when((b == 0) | (expert != be_ref[jnp.maximum(b - 1, 0)]))
    def _():
        start_fetch(ordinal + WEIGHT_LOOKAHEAD)
        for c in weight_copies(expert, slot):
            c.wait()

    @pl.when(n_valid_rows == 0)
    def _():
        ys_ref[...] = jnp.zeros_like(ys_ref)

    @pl.when(n_valid_rows > 0)
    def _():
        lo, hi = _unpack_halves(xs_ref[...])
        x = jnp.concatenate([lo, hi], axis=1).astype(BF16)
        gate = jnp.dot(x, wg_f[slot].astype(BF16), preferred_element_type=F32)
        up = jnp.dot(x, wu_f[slot].astype(BF16), preferred_element_type=F32)
        hb = (jax.nn.silu(gate) * up).astype(BF16)
        ys_ref[...] = _pack_halves(jnp.dot(hb, wd_f[slot].astype(BF16), preferred_element_type=F32))


def _experts(block_expert, block_rows, last_block, block_ord, used_experts, xs, w_gate, w_up, w_down):
    half = xs.shape[1]
    n_exp, d, de = w_gate.shape
    bm = EXPERT_ROWS
    n_blocks = block_expert.shape[0]
    slots = n_blocks * bm
    n_slots = WEIGHT_LOOKAHEAD + 1
    any_space = pl.BlockSpec(memory_space=pl.ANY)
    return pl.pallas_call(
        _expert_kernel,
        grid_spec=pltpu.PrefetchScalarGridSpec(
            num_scalar_prefetch=5,
            grid=(n_blocks,),
            in_specs=[pl.BlockSpec((bm, half), lambda b, be, nr, last, *_: (jnp.minimum(b, last[0]), 0)),
                      any_space, any_space, any_space],
            out_specs=pl.BlockSpec((bm, half), lambda b, *_: (b, 0)),
            scratch_shapes=[pltpu.VMEM((n_slots, d, de), F32), pltpu.VMEM((n_slots, d, de), F32),
                            pltpu.VMEM((n_slots, de, d), F32),
                            pltpu.SemaphoreType.DMA((n_slots, 3))],
        ),
        out_shape=jax.ShapeDtypeStruct((slots, half), U32),
        compiler_params=_params(("arbitrary",)),
        name="experts",
    )(block_expert, block_rows, last_block, block_ord, used_experts, xs, w_gate, w_up, w_down)


def _combine_kernel(nb, per_batch, pos0_ref, pos1_ref, h2_ref, route_ref, g_ref, ys_ref, o_ref, y_ref, sem):
    t = pl.program_id(0)
    tm = h2_ref.shape[0]

    def issue(tile, slot):
        base = ((tile // per_batch) * nb + 1 + tile % per_batch) * tm

        for r in range(tm):
            for k in range(2):
                p = (pos0_ref, pos1_ref)[k][base + r]
                pltpu.make_async_copy(ys_ref.at[pl.ds(p, 1), :], y_ref.at[slot, k, pl.ds(r, 1), :],
                                      sem.at[slot]).start(priority=k)

    @pl.when(t == 0)
    def _():
        issue(0, 0)

    @pl.when(t + 1 < pl.num_programs(0))
    def _():
        issue(t + 1, (t + 1) % 2)

    slot = t % 2
    for k in range(2):
        pltpu.make_async_copy(ys_ref.at[pl.ds(0, tm), :], y_ref.at[slot, k], sem.at[slot]).wait()
    route = route_ref[...]
    w0 = route[:, COL_W0:COL_W0 + 1]
    w1 = route[:, COL_W1:COL_W1 + 1]
    half = y_ref.shape[-1]
    y0 = _unpack_halves(y_ref[slot, 0])
    y1 = _unpack_halves(y_ref[slot, 1])
    hs = [h2_ref[:, s * half:(s + 1) * half] + (w0 * y0[s] + w1 * y1[s]) for s in range(2)]
    ms = sum(jnp.sum(h * h, axis=-1, keepdims=True) for h in hs) / (2 * half)
    inv = lax.rsqrt(ms + RMS_EPS)
    for s in range(2):
        o_ref[0, :, s * half:(s + 1) * half] = hs[s] * inv * g_ref[:, s * half:(s + 1) * half]


def _combine(pos, h2, route, g, ys, batch, nb, seq):
    rows, d = h2.shape
    tm = BLOCK
    per_batch = seq // tm
    tile = lambda t, *_: ((t // per_batch) * nb + 1 + t % per_batch, 0)
    return pl.pallas_call(
        functools.partial(_combine_kernel, nb, per_batch),
        grid_spec=pltpu.PrefetchScalarGridSpec(
            num_scalar_prefetch=2,
            grid=(batch * per_batch,),
            in_specs=[pl.BlockSpec((tm, d), tile),
                      pl.BlockSpec((tm, LANES), tile),
                      pl.BlockSpec((1, d), lambda t, *_: (0, 0)),
                      pl.BlockSpec(memory_space=pl.ANY)],
            out_specs=pl.BlockSpec((1, tm, d), lambda t, *_: (t // per_batch, t % per_batch, 0)),
            scratch_shapes=[pltpu.VMEM((2, 2, tm, ys.shape[1]), U32), pltpu.SemaphoreType.DMA((2,))],
        ),
        out_shape=jax.ShapeDtypeStruct((batch, seq, d), F32),
        compiler_params=_params(("arbitrary",)),
        name="combine",
    )(pos[0], pos[1], h2, route, g, ys)


def kernel(x, meta_tokens, rel_bias, norm_mix, w_in, attn_sinks, pool_mix, pool_scale,
           w_attn_branch, w_pool_branch, w_out, norm_ffn, w_router_group, b_router_group,
           w_router_expert, b_router_expert, w_gate, w_up, w_down, norm_final):
    batch, seq, d = x.shape
    depth = w_in.shape[0]
    assert depth == 1, "single-layer stack"
    aw = w_attn_branch.shape[1]
    pw = w_pool_branch.shape[1]
    kvw = 2 * N_KV_HEADS * HEAD_DIM
    assert aw == N_Q_HEADS * HEAD_DIM and w_in.shape[2] == aw + kvw + pw + 2 * d
    assert seq % BLOCK == 0
    lp = seq + BLOCK
    nb = lp // BLOCK
    rows = batch * lp
    assert rows % ROW_TILE == 0

    assert ROW_TILE == 2 * BLOCK
    lead = jnp.concatenate([jnp.zeros((PAD, d), x.dtype), meta_tokens.astype(x.dtype)], axis=0)
    x2 = x.reshape(batch * seq, d)

    q, kv, u, ga, gp = _inproj(lead, x2, nb, norm_mix[0][None], w_in[0], (aw, kvw, pw, d, d))

    attn = _attention(q, kv, _attn_bias_tables(rel_bias), attn_sinks[0].astype(F32), batch, nb)

    n_router = N_GROUPS + N_EXPERTS
    wr = jnp.concatenate([w_router_group[0], w_router_expert[0], jnp.zeros((d, LANES - n_router), F32)], axis=1)
    br = jnp.concatenate([b_router_group[0], b_router_expert[0], jnp.zeros((LANES - n_router,), F32)])[None]
    wr_hi = wr.astype(BF16)
    wr_split = jnp.concatenate([wr_hi, (wr - wr_hi.astype(F32)).astype(BF16)], axis=1)
    h2, xp, route, route_t, counts = _mixer(
        lp, attn, u, ga, gp, lead, x2, pool_mix[0].astype(BF16), pool_scale[0][None].astype(F32),
        w_attn_branch[0], w_pool_branch[0], w_out[0], norm_ffn[0][None], wr_split, br)

    bm = EXPERT_ROWS
    n_tok = batch * (seq + N_META)
    n_blocks = (2 * n_tok) // bm + N_EXPERTS
    cnt = counts[0, :N_EXPERTS].astype(I32)
    blocks_e = (cnt + bm - 1) // bm
    bend = jnp.cumsum(blocks_e)
    bstart = bend - blocks_e
    ord_e = jnp.cumsum((blocks_e > 0).astype(I32)) - 1
    n_used = bend[-1]
    last_block = jnp.maximum(n_used - 1, 0)
    blk = jnp.arange(n_blocks, dtype=I32)

    def expert_of(block):
        return jnp.minimum(jnp.sum((bend[None, :] <= block[:, None]).astype(I32), axis=1), N_EXPERTS - 1)

    block_expert = expert_of(jnp.minimum(blk, last_block))
    own = block_expert[:, None] == jnp.arange(N_EXPERTS, dtype=I32)[None, :]
    pick = lambda v: jnp.sum(jnp.where(own, v[None, :], 0), axis=1)
    block_rows = jnp.clip(pick(cnt) - (blk - pick(bstart)) * bm, 0, bm)
    block_rows = jnp.where(blk < n_used, block_rows, 0).astype(I32)
    block_ord = pick(ord_e).astype(I32)
    experts = jnp.arange(N_EXPERTS, dtype=I32)
    is_kth = (ord_e[None, :] == jnp.arange(N_EXPERTS + WEIGHT_LOOKAHEAD, dtype=I32)[:, None]) & (blocks_e > 0)[None, :]
    used_experts = jnp.where(jnp.any(is_kth, axis=1), jnp.sum(jnp.where(is_kth, experts[None, :], 0), axis=1), -1)
    gap_lo = (bstart * bm + cnt).astype(I32)
    gap_hi = (bend * bm).astype(I32)
    e = route_t[COL_E0:COL_E1 + 1].astype(I32)
    rank = route_t[COL_R0:COL_R1 + 1].astype(I32)
    first_row = jnp.sum(jnp.where(e[:, None, :] == experts[None, :, None], (bstart * bm)[None, :, None], 0), axis=1)
    spare = n_blocks * bm + (2 * jnp.arange(rows, dtype=I32)[None, :] + jnp.arange(2, dtype=I32)[:, None]) % SPARE_ROWS
    pos = jnp.where(e >= 0, first_row + rank, spare).astype(I32)

    xs = _scatter(gap_lo, gap_hi, n_used.astype(I32).reshape(1), pos, xp, n_blocks * bm)
    ys = _experts(block_expert.astype(I32), block_rows, last_block.astype(I32).reshape(1), block_ord,
                  used_experts.astype(I32), xs, w_gate[0], w_up[0], w_down[0])
    return _combine(pos, h2, route, norm_final[None].astype(F32), ys, batch, nb, seq)
```

```python
import functools
import math

import numpy as np
import jax
import jax.numpy as jnp
from jax import lax
from jax.experimental import pallas as pl
from jax.experimental.pallas import tpu as pltpu

F32 = jnp.float32
BF16 = jnp.bfloat16
I32 = jnp.int32
U32 = jnp.uint32

BLOCK = 128
N_META = 16
PAD = BLOCK - N_META
HEAD_DIM = 64
N_KV_HEADS = 2
Q_PER_KV = 8
N_Q_HEADS = N_KV_HEADS * Q_PER_KV
WINDOW = 128
POOL_WINDOWS = (2, 4, 8, 16)
POOL_HALO = 16
N_BUCKETS = 32
MAX_DISTANCE = 128
N_GROUPS = 8
EXPERTS_PER_GROUP = 8
N_EXPERTS = N_GROUPS * EXPERTS_PER_GROUP
RMS_EPS = 1e-6
LANES = 128
VMEM_LIMIT = 56 * 1024 * 1024

ROW_TILE = 256
EXPERT_ROWS = 256
STAGE_COLS = 640
WEIGHT_LOOKAHEAD = 2
SPARE_ROWS = 4 * ROW_TILE
COL_E0, COL_E1, COL_W0, COL_W1, COL_R0, COL_R1 = 0, 1, 2, 3, 4, 5
ROUTE_ROWS = 8


def _params(sem, vmem=VMEM_LIMIT):
    return pltpu.CompilerParams(dimension_semantics=sem, vmem_limit_bytes=vmem)


def _resident(shape):
    nd = len(shape)
    return pl.BlockSpec(shape, lambda *_: (0,) * nd, pipeline_mode=pl.Buffered(1))


def _tile_block_specs(nb, per_batch, d):
    def spec(half):
        def index(i, *_):
            g = 2 * i + half
            return ((g // nb) * per_batch + jnp.maximum(g % nb - 1, 0), 0)
        return pl.BlockSpec((BLOCK, d), index)
    return [spec(0), spec(1)]


def _tile_rows(nb, lead_ref, xa_ref, xb_ref):
    i = pl.program_id(0)
    halves = [jnp.where((2 * i + half) % nb == 0, lead_ref[...], ref[...])
              for half, ref in enumerate((xa_ref, xb_ref))]
    return jnp.concatenate(halves, axis=0)


def _stage_bf16(w_hbm, w_ref, stage_ref, sem):
    rows, n = w_hbm.shape
    chunk = stage_ref.shape[2]
    starts = list(range(0, n, chunk))
    widths = [min(chunk, n - c0) for c0 in starts]
    copies = [pltpu.make_async_copy(w_hbm.at[:, pl.ds(c0, cw)],
                                    stage_ref.at[k % 2, pl.ds(0, rows), pl.ds(0, cw)], sem.at[k % 2])
              for k, (c0, cw) in enumerate(zip(starts, widths))]
    copies[0].start()
    for k, (c0, cw) in enumerate(zip(starts, widths)):
        if k + 1 < len(copies):
            copies[k + 1].start()
        copies[k].wait()
        w_ref[:, c0:c0 + cw] = stage_ref[k % 2, 0:rows, 0:cw].astype(BF16)


def _inproj_kernel(nb, lead_ref, xa_ref, xb_ref, g_ref, w_hbm, q_ref, kv_ref, u_ref, ga_ref, gp_ref,
                   w_ref, stage_ref, sem):
    @pl.when(pl.program_id(0) == 0)
    def _():
        _stage_bf16(w_hbm, w_ref, stage_ref, sem)

    x = _tile_rows(nb, lead_ref, xa_ref, xb_ref)
    ms = jnp.mean(x * x, axis=-1, keepdims=True)
    hn = (x * lax.rsqrt(ms + RMS_EPS) * g_ref[...]).astype(BF16)
    off = 0
    for ref in (q_ref, kv_ref, u_ref, ga_ref, gp_ref):
        width = ref.shape[1]
        for c in range(0, width, 1024):
            cw = min(1024, width - c)
            ref[:, c:c + cw] = jnp.dot(hn, w_ref[:, off + c:off + c + cw],
                                       preferred_element_type=F32).astype(BF16)
        off += width


def _inproj(lead, x2, nb, g, w, widths):
    d = x2.shape[1]
    tm = ROW_TILE
    per_batch = nb - 1
    rows = x2.shape[0] // per_batch * nb
    outs = [jax.ShapeDtypeStruct((rows, wd), BF16) for wd in widths]
    return pl.pallas_call(
        functools.partial(_inproj_kernel, nb),
        grid=(rows // tm,),
        in_specs=[_resident(lead.shape)] + _tile_block_specs(nb, per_batch, d)
                 + [_resident((1, d)), pl.BlockSpec(memory_space=pl.ANY)],
        out_specs=[pl.BlockSpec((tm, wd), lambda i: (i, 0)) for wd in widths],
        out_shape=outs,
        scratch_shapes=[pltpu.VMEM(w.shape, BF16), pltpu.VMEM((2, w.shape[0], STAGE_COLS), F32),
                        pltpu.SemaphoreType.DMA((2,))],
        compiler_params=_params(("arbitrary",)),
        name="inproj",
    )(lead, x2, x2, g, w)


PAIR = 2 * HEAD_DIM
PAIR_KEYS = 4 * BLOCK + PAIR
N_PAIRS = N_Q_HEADS // 2


def _attn_kernel(sink_ref, q_ref, kvp_ref, kvc_ref, kvm_ref, bias_ref, o_ref):
    kw = N_KV_HEADS * HEAD_DIM
    win = 2 * BLOCK
    lane = lax.broadcasted_iota(I32, (BLOCK, PAIR), 1)
    scale = jnp.asarray(HEAD_DIM ** -0.5, BF16)

    def pair_rows(x_win, x_meta, make):
        pad = PAIR - 2 * N_META
        return jnp.concatenate([make(x_win, 0), make(x_win, 1), make(x_meta, 0), make(x_meta, 1),
                                jnp.zeros((pad,) + make(x_meta, 0).shape[1:], BF16)], axis=0)

    for hk in range(N_KV_HEADS):
        ks = slice(hk * HEAD_DIM, (hk + 1) * HEAD_DIM)
        vs = slice(kw + hk * HEAD_DIM, kw + (hk + 1) * HEAD_DIM)
        k_win = jnp.concatenate([kvp_ref[:, ks], kvc_ref[:, ks]], axis=0) * scale
        v_win = jnp.concatenate([kvp_ref[:, vs], kvc_ref[:, vs]], axis=0)
        k_meta = kvm_ref[PAD:, ks] * scale
        v_meta = kvm_ref[PAD:, vs]

        def key_lanes(x, side):
            z = jnp.zeros_like(x)
            return jnp.concatenate([x, z] if side == 0 else [z, x], axis=1)

        def value_lanes(x, side):
            z, o = jnp.zeros_like(x), jnp.ones_like(x)
            return jnp.concatenate([x, z, o, z] if side == 0 else [z, x, z, o], axis=1)

        kbd = pair_rows(k_win, k_meta, key_lanes)
        vext = pair_rows(v_win, v_meta, value_lanes)
        for jp in range(Q_PER_KV // 2):
            j = hk * (Q_PER_KV // 2) + jp
            qp = q_ref[:, j * PAIR:(j + 1) * PAIR]
            s = lax.dot_general(qp, kbd, (((1,), (1,)), ((), ())),
                                preferred_element_type=F32) + bias_ref[0, j]
            s_meta = s[:, 2 * win:]
            mxs, sink_terms = [], []
            for side in range(2):
                mine = (lane >= side * N_META) & (lane < (side + 1) * N_META)
                sink = sink_ref[2 * j + side]
                mx = jnp.maximum(jnp.max(s[:, side * win:(side + 1) * win], axis=-1, keepdims=True),
                                 jnp.max(jnp.where(mine, s_meta, -jnp.inf), axis=-1, keepdims=True))
                mx = jnp.maximum(mx, sink)
                mxs.append(mx)
                sink_terms.append(jnp.exp(sink - mx))
            p = jnp.concatenate([jnp.exp(s[:, :win] - mxs[0]), jnp.exp(s[:, win:2 * win] - mxs[1]),
                                 jnp.exp(s_meta - jnp.where(lane < N_META, mxs[0], mxs[1]))],
                                axis=1).astype(BF16)
            r = jnp.dot(p, vext, preferred_element_type=F32)
            den = r[:, PAIR:] + jnp.where(lane < HEAD_DIM, sink_terms[0], sink_terms[1])
            o_ref[:, j * PAIR:(j + 1) * PAIR] = (r[:, :PAIR] / den).astype(o_ref.dtype)


def _attention(q, kv, bias, sinks, batch, nb):
    rows, aw = q.shape
    kvw = kv.shape[1]
    return pl.pallas_call(
        _attn_kernel,
        grid=(batch, nb),
        in_specs=[pl.BlockSpec(memory_space=pltpu.SMEM),
                  pl.BlockSpec((BLOCK, aw), lambda b, n: (b * nb + n, 0)),
                  pl.BlockSpec((BLOCK, kvw), lambda b, n: (b * nb + jnp.maximum(n - 1, 0), 0)),
                  pl.BlockSpec((BLOCK, kvw), lambda b, n: (b * nb + n, 0)),
                  pl.BlockSpec((BLOCK, kvw), lambda b, n: (b * nb, 0)),
                  pl.BlockSpec((1,) + bias.shape[1:], lambda b, n: (jnp.minimum(n, 2), 0, 0, 0))],
        out_specs=pl.BlockSpec((BLOCK, aw), lambda b, n: (b * nb + n, 0)),
        out_shape=jax.ShapeDtypeStruct((rows, aw), BF16),
        compiler_params=_params(("parallel", "arbitrary")),
        name="attention",
    )(sinks, q, kv, kv, kv, bias)


def _attn_bias_tables(rel_bias):
    max_exact = N_BUCKETS // 2
    qi = np.arange(BLOCK)[:, None]
    kj = np.arange(2 * BLOCK)[None, :]
    mj = PAD + np.arange(N_META)[None, :]
    dist = qi + BLOCK - kj
    d = np.maximum(dist, 0)
    large = max_exact + (np.log(np.maximum(d, 1).astype(np.float32) / max_exact)
                         / math.log(MAX_DISTANCE / max_exact) * (N_BUCKETS - max_exact)).astype(np.int32)
    bucket = np.where(d < max_exact, d, np.minimum(large, N_BUCKETS - 1))
    onehot = (bucket[..., None] == np.arange(N_BUCKETS)).astype(np.float32)
    rb = rel_bias.astype(F32)
    win = jnp.einsum("qkb,bh->hqk", onehot, rb, precision=lax.Precision.HIGHEST)
    meta = jnp.broadcast_to(rb[N_BUCKETS - 1][:, None, None], (N_Q_HEADS, BLOCK, N_META))
    win_ok, meta_ok = [], []
    for n in range(3):
        win_ok.append((dist >= 0) & (dist < WINDOW) & ((n - 1) * BLOCK + kj >= PAD))
        meta_ok.append(np.broadcast_to(n * BLOCK + qi - mj >= WINDOW, (BLOCK, N_META)))
    win_t = jnp.where(np.stack(win_ok)[:, None], win[None], -jnp.inf)
    meta_t = jnp.where(np.stack(meta_ok)[:, None], meta[None], -jnp.inf)
    win_t = win_t.reshape(3, N_PAIRS, 2, BLOCK, 2 * BLOCK).transpose(0, 1, 3, 2, 4).reshape(3, N_PAIRS, BLOCK, -1)
    meta_t = meta_t.reshape(3, N_PAIRS, 2, BLOCK, N_META).transpose(0, 1, 3, 2, 4).reshape(3, N_PAIRS, BLOCK, -1)
    unused = jnp.full((3, N_PAIRS, BLOCK, PAIR - 2 * N_META), -jnp.inf, F32)
    return jnp.concatenate([win_t, meta_t, unused], axis=3)


MIX_CHUNKS = 4


def _interleave(pattern, **streams):
    for key in pattern:
        next(streams[key], None)
    for s in streams.values():
        for _ in s:
            pass


def _mixer_kernel(lp, n_tiles, attn_ref, u_ref, uh_ref, ga_ref, gp_ref, lead_ref, xa_ref, xb_ref,
                  pmix_ref, pscale_ref, wa_hbm, wp_hbm, wo_hbm, g2_ref, wr_ref, br_ref,
                  h2_ref, xp_ref, route_ref, route_t_ref, cnt_ref, run_ref, h2_s, a_s, p_s, m_s,
                  wa_ref, wp_ref, wo_ref, stage_ref, stage_sem):
    i = pl.program_id(0)
    tm = h2_ref.shape[0]
    d = h2_ref.shape[1]
    cw = d // MIX_CHUNKS

    @pl.when(i == 0)
    def _():
        run_ref[...] = jnp.zeros_like(run_ref)
        h2_s[...] = jnp.zeros_like(h2_s)
        for w_hbm, w_ref in ((wa_hbm, wa_ref), (wp_hbm, wp_ref), (wo_hbm, wo_ref)):
            _stage_bf16(w_hbm, w_ref, stage_ref, stage_sem)

    def chain():
        ic = jnp.minimum(i, n_tiles - 1)
        t = (ic * tm + lax.broadcasted_iota(I32, (tm, 1), 0)) % lp
        valid = t >= PAD
        tx = (ic * tm - POOL_HALO + lp + lax.broadcasted_iota(I32, (tm + POOL_HALO, 1), 0)) % lp
        n_rows = (t - PAD + 1).astype(F32)
        gw = u_ref.shape[1] // len(POOL_WINDOWS)
        attn = attn_ref[...]
        aw = d // len(POOL_WINDOWS)
        for gi, w in enumerate(POOL_WINDOWS):
            a_s[:, gi * aw:(gi + 1) * aw] = jnp.dot(attn, wa_ref[:, gi * aw:(gi + 1) * aw],
                                                    preferred_element_type=F32)
            cols = slice(gi * gw, (gi + 1) * gw)
            uext = jnp.concatenate([uh_ref[:, cols], u_ref[:, cols]], axis=0).astype(F32)
            c = jnp.where(tx >= PAD, uext, 0.0)
            s, span = c, 1
            while span < w:
                s = s[span:] + s[:-span]
                span *= 2
            win = s[POOL_HALO + 1 - w:POOL_HALO + 1 - w + tm]
            n_valid = jnp.clip(n_rows, 1.0, float(w))
            mixed = jnp.where(valid, win / n_valid - c[POOL_HALO:], 0.0)
            m_s[:, cols] = (jnp.dot(mixed.astype(BF16), pmix_ref[gi], preferred_element_type=F32)
                            * pscale_ref[:, cols]).astype(BF16)
            yield
        pool = m_s[:, :u_ref.shape[1]]
        for c in range(MIX_CHUNKS):
            cols = slice(c * cw, (c + 1) * cw)
            p_s[:, cols] = jnp.dot(pool, wp_ref[:, cols], preferred_element_type=F32)
            yield
        nb = lp // BLOCK
        halves = [jnp.where((2 * ic + hf) % nb == 0, lead_ref[...], ref[...])
                  for hf, ref in enumerate((xa_ref, xb_ref))]
        h2 = jnp.concatenate(halves, axis=0)
        for c in range(MIX_CHUNKS):
            cols = slice(c * cw, (c + 1) * cw)
            merged = (jax.nn.sigmoid(ga_ref[:, cols].astype(F32)) * a_s[:, cols]
                      + jax.nn.sigmoid(gp_ref[:, cols].astype(F32)) * p_s[:, cols]).astype(BF16)
            yield
            h2 = h2 + jnp.dot(merged, wo_ref[cols, :], preferred_element_type=F32)
            yield
        h2_ref[...] = h2
        h2_s[...] = h2
        yield

    def tail():
        h2p = h2_s[...]
        tp = ((i - 1) * tm + lp + lax.broadcasted_iota(I32, (tm, 1), 0)) % lp
        valid_p = (tp >= PAD) & (i >= 1)
        ms = jnp.mean(h2p * h2p, axis=-1, keepdims=True)
        inv = lax.rsqrt(ms + RMS_EPS)
        yield
        half = d // 2
        hi_prod = jnp.zeros((tm, 2 * LANES), F32)
        lo_prod = jnp.zeros((tm, LANES), F32)
        n_pack = MIX_CHUNKS // 2
        pw = half // n_pack
        for c in range(n_pack):
            parts = []
            for base in (c * pw, half + c * pw):
                hn = h2p[:, base:base + pw] * inv * g2_ref[:, base:base + pw]
                x_hi = hn.astype(BF16)
                x_hi32 = x_hi.astype(F32)
                x_lo = (hn - x_hi32).astype(BF16)
                hi_prod = hi_prod + jnp.dot(x_hi, wr_ref[base:base + pw, :], preferred_element_type=F32)
                lo_prod = lo_prod + jnp.dot(x_lo, wr_ref[base:base + pw, :LANES], preferred_element_type=F32)
                parts.append(lax.bitcast_convert_type(x_hi32, U32))
            xp_ref[:, c * pw:(c + 1) * pw] = (parts[1] & jnp.uint32(0xFFFF0000)) | (parts[0] >> 16)
            yield
        logits = hi_prod[:, :LANES] + (hi_prod[:, LANES:] + lo_prod) + br_ref[...]
        col = lax.broadcasted_iota(I32, logits.shape, 1).astype(F32)
        neg = -jnp.inf
        gl = jnp.where(col < N_GROUPS, logits, neg)
        gmax = jnp.max(gl, axis=-1, keepdims=True)
        grp = jnp.min(jnp.where(gl == gmax, col, float(LANES)), axis=-1, keepdims=True)
        p_grp = 1.0 / jnp.sum(jnp.exp(gl - gmax), axis=-1, keepdims=True)
        yield
        e_lo = N_GROUPS + grp * EXPERTS_PER_GROUP
        el = jnp.where((col >= e_lo) & (col < e_lo + EXPERTS_PER_GROUP), logits, neg)
        m1 = jnp.max(el, axis=-1, keepdims=True)
        i1 = jnp.min(jnp.where(el == m1, col, float(LANES)), axis=-1, keepdims=True)
        el2 = jnp.where(col == i1, neg, el)
        m2 = jnp.max(el2, axis=-1, keepdims=True)
        i2 = jnp.min(jnp.where(el2 == m2, col, float(LANES)), axis=-1, keepdims=True)
        z = jnp.exp(m2 - m1)
        w1 = p_grp / (1.0 + z)
        w2 = p_grp * z / (1.0 + z)
        e1 = i1 - N_GROUPS
        e2 = i2 - N_GROUPS
        yield
        oh1 = jnp.where((col == e1) & valid_p, 1.0, 0.0)
        oh2 = jnp.where((col == e2) & valid_p, 1.0, 0.0)
        lower = (lax.broadcasted_iota(I32, (tm, tm), 0) > lax.broadcasted_iota(I32, (tm, tm), 1))
        lower = jnp.where(lower, 1.0, 0.0).astype(BF16)
        before1 = jnp.dot(lower, oh1.astype(BF16), preferred_element_type=F32)
        before2 = jnp.dot(lower, oh2.astype(BF16), preferred_element_type=F32)
        tot1 = jnp.sum(oh1, axis=0, keepdims=True)
        tot2 = jnp.sum(oh2, axis=0, keepdims=True)
        run = run_ref[...]
        r1 = jnp.sum(oh1 * (run + before1), axis=-1, keepdims=True)
        r2 = jnp.sum(oh2 * (run + tot1 + before2), axis=-1, keepdims=True)
        run = run + tot1 + tot2
        run_ref[...] = run
        cnt_ref[...] = run
        yield
        slab = jnp.zeros(logits.shape, F32)
        for cidx, val in ((COL_E0, jnp.where(valid_p, e1, -1.0)),
                          (COL_E1, jnp.where(valid_p, e2, -1.0)),
                          (COL_W0, w1), (COL_W1, w2), (COL_R0, r1), (COL_R1, r2)):
            slab = jnp.where(col == cidx, val, slab)
        route_ref[...] = slab
        route_t_ref[...] = slab.T[:ROUTE_ROWS, :]
        yield

    _interleave("cccc" + "ctctctct" + "cctcctcctcc", c=chain(), t=tail())


def _mixer(lp, attn, u, ga, gp, lead, x2, pmix, pscale, wa, wp, wo, g2, wr, br):
    rows = attn.shape[0]
    d = x2.shape[1]
    tm = ROW_TILE
    nb = lp // BLOCK
    n_tiles = rows // tm
    halo_blocks = tm // POOL_HALO
    any_space = pl.BlockSpec(memory_space=pl.ANY)
    cur = lambda i: (jnp.minimum(i, n_tiles - 1), 0)
    prev = lambda i: (jnp.maximum(i - 1, 0), 0)

    def x_spec(half):
        def index(i):
            g = 2 * jnp.minimum(i, n_tiles - 1) + half
            return ((g // nb) * (nb - 1) + jnp.maximum(g % nb - 1, 0), 0)
        return pl.BlockSpec((BLOCK, d), index)

    return pl.pallas_call(
        functools.partial(_mixer_kernel, lp, n_tiles),
        grid=(n_tiles + 1,),
        in_specs=[pl.BlockSpec((tm, attn.shape[1]), cur),
                  pl.BlockSpec((tm, u.shape[1]), cur),
                  pl.BlockSpec((POOL_HALO, u.shape[1]),
                               lambda i: (jnp.maximum(jnp.minimum(i, n_tiles - 1) * halo_blocks - 1, 0), 0)),
                  pl.BlockSpec((tm, d), cur),
                  pl.BlockSpec((tm, d), cur),
                  _resident(lead.shape), x_spec(0), x_spec(1),
                  _resident(pmix.shape), _resident(pscale.shape), any_space, any_space, any_space,
                  _resident(g2.shape), _resident(wr.shape), _resident(br.shape)],
        out_specs=[pl.BlockSpec((tm, d), cur),
                   pl.BlockSpec((tm, d // 2), prev),
                   pl.BlockSpec((tm, LANES), prev),
                   pl.BlockSpec((ROUTE_ROWS, tm), lambda i: (0, jnp.maximum(i - 1, 0))),
                   pl.BlockSpec((1, LANES), lambda i: (0, 0))],
        out_shape=[jax.ShapeDtypeStruct((rows, d), F32),
                   jax.ShapeDtypeStruct((rows, d // 2), U32),
                   jax.ShapeDtypeStruct((rows, LANES), F32),
                   jax.ShapeDtypeStruct((ROUTE_ROWS, rows), F32),
                   jax.ShapeDtypeStruct((1, LANES), F32)],
        scratch_shapes=[pltpu.VMEM((1, LANES), F32), pltpu.VMEM((tm, d), F32),
                        pltpu.VMEM((tm, d), F32), pltpu.VMEM((tm, d), F32), pltpu.VMEM((tm, d), BF16),
                        pltpu.VMEM(wa.shape, BF16), pltpu.VMEM(wp.shape, BF16), pltpu.VMEM(wo.shape, BF16),
                        pltpu.VMEM((2, d, STAGE_COLS), F32), pltpu.SemaphoreType.DMA((2,))],
        compiler_params=_params(("arbitrary",)),
        name="mixer",
    )(attn, u, u, ga, gp, lead, x2, x2, pmix, pscale, wa, wp, wo, g2, wr, br)


def _zero_unused_slots(gap_lo_ref, gap_hi_ref, used_ref, xs_ref, zero_ref, sem):
    bm = zero_ref.shape[0]
    n_rows = xs_ref.shape[0]
    n_blocks = (n_rows - SPARE_ROWS) // bm
    zero_ref[...] = jnp.zeros_like(zero_ref)

    def piece(start, size):
        return pltpu.make_async_copy(zero_ref.at[pl.ds(0, size), :], xs_ref.at[pl.ds(start, size), :], sem)

    def for_each_piece(op):
        def tail(e, carry):
            lo, hi = gap_lo_ref[e], gap_hi_ref[e]
            length = hi - lo
            for bit in range(int(math.log2(bm)) - 1, 2, -1):
                size = 1 << bit
                above = (length >> (bit + 1)) << (bit + 1)

                @pl.when((length >> bit) & 1 == 1)
                def _():
                    op(piece(pl.multiple_of(hi - above - size, 8), size))
            for j in range(7):
                @pl.when(j < (length & 7))
                def _():
                    op(piece(lo + j, 1))
            return carry

        def block(b, carry):
            op(piece(pl.multiple_of(b * bm, bm), bm))
            return carry

        lax.fori_loop(0, N_EXPERTS, tail, 0)
        lax.fori_loop(used_ref[0], n_blocks + SPARE_ROWS // bm, block, 0)

    for_each_piece(lambda c: c.start())
    for_each_piece(lambda c: c.wait())


def _scatter_kernel(gap_lo_ref, gap_hi_ref, used_ref, pos0_ref, pos1_ref, x_ref, xs_ref, stage_ref, zero_ref,
                    sem, zsem):
    i = pl.program_id(0)
    last = pl.num_programs(0) - 1
    tm = x_ref.shape[0]
    slot = i % 2

    @pl.when(i == 0)
    def _():
        _zero_unused_slots(gap_lo_ref, gap_hi_ref, used_ref, xs_ref, zero_ref, zsem)

    def retire(s):
        for k in range(2):
            pltpu.make_async_copy(stage_ref.at[s], xs_ref.at[pl.ds(0, tm), :], sem.at[s]).wait()

    @pl.when(i >= 2)
    def _():
        retire(slot)

    stage_ref[slot] = x_ref[...]

    for r in range(tm):
        for k in range(2):
            p = (pos0_ref, pos1_ref)[k][i * tm + r]
            pltpu.make_async_copy(stage_ref.at[slot, pl.ds(r, 1), :], xs_ref.at[pl.ds(p, 1), :],
                                  sem.at[slot]).start(priority=k)

    @pl.when(i == last)
    def _():
        @pl.when(i >= 1)
        def _():
            retire(1 - slot)
        retire(slot)


def _scatter(gap_lo, gap_hi, n_used, pos, xp, slots):
    rows, half = xp.shape
    tm = ROW_TILE
    assert SPARE_ROWS % EXPERT_ROWS == 0 and slots % EXPERT_ROWS == 0
    return pl.pallas_call(
        _scatter_kernel,
        grid_spec=pltpu.PrefetchScalarGridSpec(
            num_scalar_prefetch=5,
            grid=(rows // tm,),
            in_specs=[pl.BlockSpec((tm, half), lambda i, *_: (i, 0))],
            out_specs=pl.BlockSpec(memory_space=pl.ANY),
            scratch_shapes=[pltpu.VMEM((2, tm, half), U32), pltpu.VMEM((EXPERT_ROWS, half), U32),
                            pltpu.SemaphoreType.DMA((2,)), pltpu.SemaphoreType.DMA(())],
        ),
        out_shape=jax.ShapeDtypeStruct((slots + SPARE_ROWS, half), U32),
        compiler_params=_params(("arbitrary",)),
        name="scatter",
    )(gap_lo, gap_hi, n_used, pos[0], pos[1], xp)


def _pack_halves(x):
    n = x.shape[1] // 2
    r = x.astype(BF16).astype(F32)
    lo = lax.bitcast_convert_type(r[:, :n], U32)
    hi = lax.bitcast_convert_type(r[:, n:], U32)
    return (hi & jnp.uint32(0xFFFF0000)) | (lo >> 16)


def _unpack_halves(p):
    return (lax.bitcast_convert_type(p << 16, F32),
            lax.bitcast_convert_type(p & jnp.uint32(0xFFFF0000), F32))


def _expert_kernel(be_ref, nrow_ref, last_ref, ord_ref, used_ref, xs_ref, wg_hbm, wu_hbm, wd_hbm, ys_ref,
                   wg_f, wu_f, wd_f, sem):
    b = pl.program_id(0)
    n_valid_rows = nrow_ref[b]
    expert = be_ref[b]
    ordinal = ord_ref[b]
    n_slots = wg_f.shape[0]
    slot = ordinal % n_slots

    def weight_copies(e, s):
        return [pltpu.make_async_copy(hbm.at[e], buf.at[s], sem.at[s, n])
                for n, (hbm, buf) in enumerate(((wg_hbm, wg_f), (wu_hbm, wu_f), (wd_hbm, wd_f)))]

    def start_fetch(o):
        e = used_ref[o]

        @pl.when(e >= 0)
        def _():
            for c in weight_copies(e, o % n_slots):
                c.start()

    @pl.when(b == 0)
    def _():
        for o in range(WEIGHT_LOOKAHEAD):
            start_fetch(o)

    @pl.when((b == 0) | (expert != be_ref[jnp.maximum(b - 1, 0)]))
    def _():
        start_fetch(ordinal + WEIGHT_LOOKAHEAD)
        for c in weight_copies(expert, slot):
            c.wait()

    @pl.when(n_valid_rows == 0)
    def _():
        ys_ref[...] = jnp.zeros_like(ys_ref)

    @pl.when(n_valid_rows > 0)
    def _():
        lo, hi = _unpack_halves(xs_ref[...])
        x = jnp.concatenate([lo, hi], axis=1).astype(BF16)
        gate = jnp.dot(x, wg_f[slot].astype(BF16), preferred_element_type=F32)
        up = jnp.dot(x, wu_f[slot].astype(BF16), preferred_element_type=F32)
        hb = (jax.nn.silu(gate) * up).astype(BF16)
        ys_ref[...] = _pack_halves(jnp.dot(hb, wd_f[slot].astype(BF16), preferred_element_type=F32))


def _experts(block_expert, block_rows, last_block, block_ord, used_experts, xs, w_gate, w_up, w_down):
    half = xs.shape[1]
    n_exp, d, de = w_gate.shape
    bm = EXPERT_ROWS
    n_blocks = block_expert.shape[0]
    slots = n_blocks * bm
    n_slots = WEIGHT_LOOKAHEAD + 1
    any_space = pl.BlockSpec(memory_space=pl.ANY)
    return pl.pallas_call(
        _expert_kernel,
        grid_spec=pltpu.PrefetchScalarGridSpec(
            num_scalar_prefetch=5,
            grid=(n_blocks,),
            in_specs=[pl.BlockSpec((bm, half), lambda b, be, nr, last, *_: (jnp.minimum(b, last[0]), 0)),
                      any_space, any_space, any_space],
            out_specs=pl.BlockSpec((bm, half), lambda b, *_: (b, 0)),
            scratch_shapes=[pltpu.VMEM((n_slots, d, de), F32), pltpu.VMEM((n_slots, d, de), F32),
                            pltpu.VMEM((n_slots, de, d), F32),
                            pltpu.SemaphoreType.DMA((n_slots, 3))],
        ),
        out_shape=jax.ShapeDtypeStruct((slots, half), U32),
        compiler_params=_params(("arbitrary",)),
        name="experts",
    )(block_expert, block_rows, last_block, block_ord, used_experts, xs, w_gate, w_up, w_down)


def _combine_kernel(nb, per_batch, pos0_ref, pos1_ref, h2_ref, route_ref, g_ref, ys_ref, o_ref, y_ref, sem):
    t = pl.program_id(0)
    tm = h2_ref.shape[0]

    def issue(tile, slot):
        base = ((tile // per_batch) * nb + 1 + tile % per_batch) * tm

        for r in range(tm):
            for k in range(2):
                p = (pos0_ref, pos1_ref)[k][base + r]
                pltpu.make_async_copy(ys_ref.at[pl.ds(p, 1), :], y_ref.at[slot, k, pl.ds(r, 1), :],
                                      sem.at[slot]).start(priority=k)

    @pl.when(t == 0)
    def _():
        issue(0, 0)

    @pl.when(t + 1 < pl.num_programs(0))
    def _():
        issue(t + 1, (t + 1) % 2)

    slot = t % 2
    for k in range(2):
        pltpu.make_async_copy(ys_ref.at[pl.ds(0, tm), :], y_ref.at[slot, k], sem.at[slot]).wait()
    route = route_ref[...]
    w0 = route[:, COL_W0:COL_W0 + 1]
    w1 = route[:, COL_W1:COL_W1 + 1]
    half = y_ref.shape[-1]
    y0 = _unpack_halves(y_ref[slot, 0])
    y1 = _unpack_halves(y_ref[slot, 1])
    hs = [h2_ref[:, s * half:(s + 1) * half] + (w0 * y0[s] + w1 * y1[s]) for s in range(2)]
    ms = sum(jnp.sum(h * h, axis=-1, keepdims=True) for h in hs) / (2 * half)
    inv = lax.rsqrt(ms + RMS_EPS)
    for s in range(2):
        o_ref[0, :, s * half:(s + 1) * half] = hs[s] * inv * g_ref[:, s * half:(s + 1) * half]


def _combine(pos, h2, route, g, ys, batch, nb, seq):
    rows, d = h2.shape
    tm = BLOCK
    per_batch = seq // tm
    tile = lambda t, *_: ((t // per_batch) * nb + 1 + t % per_batch, 0)
    return pl.pallas_call(
        functools.partial(_combine_kernel, nb, per_batch),
        grid_spec=pltpu.PrefetchScalarGridSpec(
            num_scalar_prefetch=2,
            grid=(batch * per_batch,),
            in_specs=[pl.BlockSpec((tm, d), tile),
                      pl.BlockSpec((tm, LANES), tile),
                      pl.BlockSpec((1, d), lambda t, *_: (0, 0)),
                      pl.BlockSpec(memory_space=pl.ANY)],
            out_specs=pl.BlockSpec((1, tm, d), lambda t, *_: (t // per_batch, t % per_batch, 0)),
            scratch_shapes=[pltpu.VMEM((2, 2, tm, ys.shape[1]), U32), pltpu.SemaphoreType.DMA((2,))],
        ),
        out_shape=jax.ShapeDtypeStruct((batch, seq, d), F32),
        compiler_params=_params(("arbitrary",)),
        name="combine",
    )(pos[0], pos[1], h2, route, g, ys)


def kernel(x, meta_tokens, rel_bias, norm_mix, w_in, attn_sinks, pool_mix, pool_scale,
           w_attn_branch, w_pool_branch, w_out, norm_ffn, w_router_group, b_router_group,
           w_router_expert, b_router_expert, w_gate, w_up, w_down, norm_final):
    batch, seq, d = x.shape
    depth = w_in.shape[0]
    assert depth == 1, "single-layer stack"
    aw = w_attn_branch.shape[1]
    pw = w_pool_branch.shape[1]
    kvw = 2 * N_KV_HEADS * HEAD_DIM
    assert aw == N_Q_HEADS * HEAD_DIM and w_in.shape[2] == aw + kvw + pw + 2 * d
    assert seq % BLOCK == 0
    lp = seq + BLOCK
    nb = lp // BLOCK
    rows = batch * lp
    assert rows % ROW_TILE == 0

    assert ROW_TILE == 2 * BLOCK
    lead = jnp.concatenate([jnp.zeros((PAD, d), x.dtype), meta_tokens.astype(x.dtype)], axis=0)
    x2 = x.reshape(batch * seq, d)

    q, kv, u, ga, gp = _inproj(lead, x2, nb, norm_mix[0][None], w_in[0], (aw, kvw, pw, d, d))

    attn = _attention(q, kv, _attn_bias_tables(rel_bias), attn_sinks[0].astype(F32), batch, nb)

    n_router = N_GROUPS + N_EXPERTS
    wr = jnp.concatenate([w_router_group[0], w_router_expert[0], jnp.zeros((d, LANES - n_router), F32)], axis=1)
    br = jnp.concatenate([b_router_group[0], b_router_expert[0], jnp.zeros((LANES - n_router,), F32)])[None]
    wr_hi = wr.astype(BF16)
    wr_split = jnp.concatenate([wr_hi, (wr - wr_hi.astype(F32)).astype(BF16)], axis=1)
    h2, xp, route, route_t, counts = _mixer(
        lp, attn, u, ga, gp, lead, x2, pool_mix[0].astype(BF16), pool_scale[0][None].astype(F32),
        w_attn_branch[0], w_pool_branch[0], w_out[0], norm_ffn[0][None], wr_split, br)

    bm = EXPERT_ROWS
    n_tok = batch * (seq + N_META)
    n_blocks = (2 * n_tok) // bm + N_EXPERTS
    cnt = counts[0, :N_EXPERTS].astype(I32)
    blocks_e = (cnt + bm - 1) // bm
    bend = jnp.cumsum(blocks_e)
    bstart = bend - blocks_e
    ord_e = jnp.cumsum((blocks_e > 0).astype(I32)) - 1
    n_used = bend[-1]
    last_block = jnp.maximum(n_used - 1, 0)
    blk = jnp.arange(n_blocks, dtype=I32)

    def expert_of(block):
        return jnp.minimum(jnp.sum((bend[None, :] <= block[:, None]).astype(I32), axis=1), N_EXPERTS - 1)

    block_expert = expert_of(jnp.minimum(blk, last_block))
    own = block_expert[:, None] == jnp.arange(N_EXPERTS, dtype=I32)[None, :]
    pick = lambda v: jnp.sum(jnp.where(own, v[None, :], 0), axis=1)
    block_rows = jnp.clip(pick(cnt) - (blk - pick(bstart)) * bm, 0, bm)
    block_rows = jnp.where(blk < n_used, block_rows, 0).astype(I32)
    block_ord = pick(ord_e).astype(I32)
    experts = jnp.arange(N_EXPERTS, dtype=I32)
    is_kth = (ord_e[None, :] == jnp.arange(N_EXPERTS + WEIGHT_LOOKAHEAD, dtype=I32)[:, None]) & (blocks_e > 0)[None, :]
    used_experts = jnp.where(jnp.any(is_kth, axis=1), jnp.sum(jnp.where(is_kth, experts[None, :], 0), axis=1), -1)
    gap_lo = (bstart * bm + cnt).astype(I32)
    gap_hi = (bend * bm).astype(I32)
    e = route_t[COL_E0:COL_E1 + 1].astype(I32)
    rank = route_t[COL_R0:COL_R1 + 1].astype(I32)
    first_row = jnp.sum(jnp.where(e[:, None, :] == experts[None, :, None], (bstart * bm)[None, :, None], 0), axis=1)
    spare = n_blocks * bm + (2 * jnp.arange(rows, dtype=I32)[None, :] + jnp.arange(2, dtype=I32)[:, None]) % SPARE_ROWS
    pos = jnp.where(e >= 0, first_row + rank, spare).astype(I32)

    xs = _scatter(gap_lo, gap_hi, n_used.astype(I32).reshape(1), pos, xp, n_blocks * bm)
    ys = _experts(block_expert.astype(I32), block_rows, last_block.astype(I32).reshape(1), block_ord,
                  used_experts.astype(I32), xs, w_gate[0], w_up[0], w_down[0])
    return _combine(pos, h2, route, norm_final[None].astype(F32), ys, batch, nb, seq)
```

```python
import functools
import math

import numpy as np
import jax
import jax.numpy as jnp
from jax import lax
from jax.experimental import pallas as pl
from jax.experimental.pallas import tpu as pltpu

F32 = jnp.float32
BF16 = jnp.bfloat16
I32 = jnp.int32
U32 = jnp.uint32

BLOCK = 128
N_META = 16
PAD = BLOCK - N_META
HEAD_DIM = 64
N_KV_HEADS = 2
Q_PER_KV = 8
N_Q_HEADS = N_KV_HEADS * Q_PER_KV
WINDOW = 128
POOL_WINDOWS = (2, 4, 8, 16)
POOL_HALO = 16
N_BUCKETS = 32
MAX_DISTANCE = 128
N_GROUPS = 8
EXPERTS_PER_GROUP = 8
N_EXPERTS = N_GROUPS * EXPERTS_PER_GROUP
RMS_EPS = 1e-6
LANES = 128
VMEM_LIMIT = 56 * 1024 * 1024

ROW_TILE = 256
EXPERT_ROWS = 256
STAGE_COLS = 640
WEIGHT_LOOKAHEAD = 2
SPARE_ROWS = 4 * ROW_TILE
COL_E0, COL_E1, COL_W0, COL_W1, COL_R0, COL_R1 = 0, 1, 2, 3, 4, 5
ROUTE_ROWS = 8


def _params(sem, vmem=VMEM_LIMIT):
    return pltpu.CompilerParams(dimension_semantics=sem, vmem_limit_bytes=vmem)


def _resident(shape):
    nd = len(shape)
    return pl.BlockSpec(shape, lambda *_: (0,) * nd, pipeline_mode=pl.Buffered(1))


def _tile_block_specs(nb, per_batch, d):
    def spec(half):
        def index(i, *_):
            g = 2 * i + half
            return ((g // nb) * per_batch + jnp.maximum(g % nb - 1, 0), 0)
        return pl.BlockSpec((BLOCK, d), index)
    return [spec(0), spec(1)]


def _tile_rows(nb, lead_ref, xa_ref, xb_ref):
    i = pl.program_id(0)
    halves = [jnp.where((2 * i + half) % nb == 0, lead_ref[...], ref[...])
              for half, ref in enumerate((xa_ref, xb_ref))]
    return jnp.concatenate(halves, axis=0)


def _stage_bf16(w_hbm, w_ref, stage_ref, sem):
    rows, n = w_hbm.shape
    chunk = stage_ref.shape[2]
    starts = list(range(0, n, chunk))
    widths = [min(chunk, n - c0) for c0 in starts]
    copies = [pltpu.make_async_copy(w_hbm.at[:, pl.ds(c0, cw)],
                                    stage_ref.at[k % 2, pl.ds(0, rows), pl.ds(0, cw)], sem.at[k % 2])
              for k, (c0, cw) in enumerate(zip(starts, widths))]
    copies[0].start()
    for k, (c0, cw) in enumerate(zip(starts, widths)):
        if k + 1 < len(copies):
            copies[k + 1].start()
        copies[k].wait()
        w_ref[:, c0:c0 + cw] = stage_ref[k % 2, 0:rows, 0:cw].astype(BF16)


def _proj_layout(aw, kvw, pw, d):
    src = {"q": 0, "kv": aw, "u": aw + kvw, "ga": aw + kvw + pw, "gp": aw + kvw + pw + d}
    dst = {"ga": 0, "gp": d, "q": 2 * d, "u": 2 * d + aw, "kv": 2 * d + aw + pw}
    width = {"q": aw, "kv": kvw, "u": pw, "ga": d, "gp": d}
    assert all(dst[k] % width[k] == 0 for k in ("q", "u", "kv")) and aw == pw
    return {k: (src[k], dst[k], width[k]) for k in src}


def _inproj_kernel(nb, layout, lead_ref, xa_ref, xb_ref, g_ref, w_hbm, out_ref, w_ref, stage_ref, sem):
    @pl.when(pl.program_id(0) == 0)
    def _():
        _stage_bf16(w_hbm, w_ref, stage_ref, sem)

    x = _tile_rows(nb, lead_ref, xa_ref, xb_ref)
    ms = jnp.mean(x * x, axis=-1, keepdims=True)
    hn = (x * lax.rsqrt(ms + RMS_EPS) * g_ref[...]).astype(BF16)
    for src, dst, width in layout.values():
        for c in range(0, width, 1024):
            cw = min(1024, width - c)
            out_ref[:, dst + c:dst + c + cw] = jnp.dot(hn, w_ref[:, src + c:src + c + cw],
                                                       preferred_element_type=F32).astype(BF16)


def _inproj(lead, x2, nb, g, w, layout):
    d = x2.shape[1]
    tm = ROW_TILE
    per_batch = nb - 1
    rows = x2.shape[0] // per_batch * nb
    total = w.shape[1]
    return pl.pallas_call(
        functools.partial(_inproj_kernel, nb, layout),
        grid=(rows // tm,),
        in_specs=[_resident(lead.shape)] + _tile_block_specs(nb, per_batch, d)
                 + [_resident((1, d)), pl.BlockSpec(memory_space=pl.ANY)],
        out_specs=pl.BlockSpec((tm, total), lambda i: (i, 0)),
        out_shape=jax.ShapeDtypeStruct((rows, total), BF16),
        scratch_shapes=[pltpu.VMEM(w.shape, BF16), pltpu.VMEM((2, w.shape[0], STAGE_COLS), F32),
                        pltpu.SemaphoreType.DMA((2,))],
        compiler_params=_params(("arbitrary",)),
        name="inproj",
    )(lead, x2, x2, g, w)


PAIR = 2 * HEAD_DIM
PAIR_KEYS = 4 * BLOCK + PAIR
N_PAIRS = N_Q_HEADS // 2


def _attn_kernel(sink_ref, q_ref, kvp_ref, kvc_ref, kvm_ref, bias_ref, o_ref):
    kw = N_KV_HEADS * HEAD_DIM
    win = 2 * BLOCK
    lane = lax.broadcasted_iota(I32, (BLOCK, PAIR), 1)
    scale = jnp.asarray(HEAD_DIM ** -0.5, BF16)

    def pair_rows(x_win, x_meta, make):
        pad = PAIR - 2 * N_META
        return jnp.concatenate([make(x_win, 0), make(x_win, 1), make(x_meta, 0), make(x_meta, 1),
                                jnp.zeros((pad,) + make(x_meta, 0).shape[1:], BF16)], axis=0)

    for hk in range(N_KV_HEADS):
        ks = slice(hk * HEAD_DIM, (hk + 1) * HEAD_DIM)
        vs = slice(kw + hk * HEAD_DIM, kw + (hk + 1) * HEAD_DIM)
        k_win = jnp.concatenate([kvp_ref[:, ks], kvc_ref[:, ks]], axis=0) * scale
        v_win = jnp.concatenate([kvp_ref[:, vs], kvc_ref[:, vs]], axis=0)
        k_meta = kvm_ref[PAD:, ks] * scale
        v_meta = kvm_ref[PAD:, vs]

        def key_lanes(x, side):
            z = jnp.zeros_like(x)
            return jnp.concatenate([x, z] if side == 0 else [z, x], axis=1)

        def value_lanes(x, side):
            z, o = jnp.zeros_like(x), jnp.ones_like(x)
            return jnp.concatenate([x, z, o, z] if side == 0 else [z, x, z, o], axis=1)

        kbd = pair_rows(k_win, k_meta, key_lanes)
        vext = pair_rows(v_win, v_meta, value_lanes)
        for jp in range(Q_PER_KV // 2):
            j = hk * (Q_PER_KV // 2) + jp
            qp = q_ref[:, j * PAIR:(j + 1) * PAIR]
            s = lax.dot_general(qp, kbd, (((1,), (1,)), ((), ())),
                                preferred_element_type=F32) + bias_ref[0, j]
            s_meta = s[:, 2 * win:]
            mxs, sink_terms = [], []
            for side in range(2):
                mine = (lane >= side * N_META) & (lane < (side + 1) * N_META)
                sink = sink_ref[2 * j + side]
                mx = jnp.maximum(jnp.max(s[:, side * win:(side + 1) * win], axis=-1, keepdims=True),
                                 jnp.max(jnp.where(mine, s_meta, -jnp.inf), axis=-1, keepdims=True))
                mx = jnp.maximum(mx, sink)
                mxs.append(mx)
                sink_terms.append(jnp.exp(sink - mx))
            p = jnp.concatenate([jnp.exp(s[:, :win] - mxs[0]), jnp.exp(s[:, win:2 * win] - mxs[1]),
                                 jnp.exp(s_meta - jnp.where(lane < N_META, mxs[0], mxs[1]))],
                                axis=1).astype(BF16)
            r = jnp.dot(p, vext, preferred_element_type=F32)
            den = r[:, PAIR:] + jnp.where(lane < HEAD_DIM, sink_terms[0], sink_terms[1])
            o_ref[:, j * PAIR:(j + 1) * PAIR] = (r[:, :PAIR] / den).astype(o_ref.dtype)


def _attention(proj, layout, bias, sinks, batch, nb):
    rows = proj.shape[0]
    _, q_off, aw = layout["q"]
    _, kv_off, kvw = layout["kv"]
    qc, kc = q_off // aw, kv_off // kvw
    q = kv = proj
    return pl.pallas_call(
        _attn_kernel,
        grid=(batch, nb),
        in_specs=[pl.BlockSpec(memory_space=pltpu.SMEM),
                  pl.BlockSpec((BLOCK, aw), lambda b, n: (b * nb + n, qc)),
                  pl.BlockSpec((BLOCK, kvw), lambda b, n: (b * nb + jnp.maximum(n - 1, 0), kc)),
                  pl.BlockSpec((BLOCK, kvw), lambda b, n: (b * nb + n, kc)),
                  pl.BlockSpec((BLOCK, kvw), lambda b, n: (b * nb, kc)),
                  pl.BlockSpec((1,) + bias.shape[1:], lambda b, n: (jnp.minimum(n, 2), 0, 0, 0))],
        out_specs=pl.BlockSpec((BLOCK, aw), lambda b, n: (b * nb + n, 0)),
        out_shape=jax.ShapeDtypeStruct((rows, aw), BF16),
        compiler_params=_params(("parallel", "arbitrary")),
        name="attention",
    )(sinks, q, kv, kv, kv, bias)


def _attn_bias_tables(rel_bias):
    max_exact = N_BUCKETS // 2
    qi = np.arange(BLOCK)[:, None]
    kj = np.arange(2 * BLOCK)[None, :]
    mj = PAD + np.arange(N_META)[None, :]
    dist = qi + BLOCK - kj
    d = np.maximum(dist, 0)
    large = max_exact + (np.log(np.maximum(d, 1).astype(np.float32) / max_exact)
                         / math.log(MAX_DISTANCE / max_exact) * (N_BUCKETS - max_exact)).astype(np.int32)
    bucket = np.where(d < max_exact, d, np.minimum(large, N_BUCKETS - 1))
    onehot = (bucket[..., None] == np.arange(N_BUCKETS)).astype(np.float32)
    rb = rel_bias.astype(F32)
    win = jnp.einsum("qkb,bh->hqk", onehot, rb, precision=lax.Precision.HIGHEST)
    meta = jnp.broadcast_to(rb[N_BUCKETS - 1][:, None, None], (N_Q_HEADS, BLOCK, N_META))
    win_ok, meta_ok = [], []
    for n in range(3):
        win_ok.append((dist >= 0) & (dist < WINDOW) & ((n - 1) * BLOCK + kj >= PAD))
        meta_ok.append(np.broadcast_to(n * BLOCK + qi - mj >= WINDOW, (BLOCK, N_META)))
    win_t = jnp.where(np.stack(win_ok)[:, None], win[None], -jnp.inf)
    meta_t = jnp.where(np.stack(meta_ok)[:, None], meta[None], -jnp.inf)
    win_t = win_t.reshape(3, N_PAIRS, 2, BLOCK, 2 * BLOCK).transpose(0, 1, 3, 2, 4).reshape(3, N_PAIRS, BLOCK, -1)
    meta_t = meta_t.reshape(3, N_PAIRS, 2, BLOCK, N_META).transpose(0, 1, 3, 2, 4).reshape(3, N_PAIRS, BLOCK, -1)
    unused = jnp.full((3, N_PAIRS, BLOCK, PAIR - 2 * N_META), -jnp.inf, F32)
    return jnp.concatenate([win_t, meta_t, unused], axis=3)


MIX_CHUNKS = 4


def _interleave(pattern, **streams):
    for key in pattern:
        next(streams[key], None)
    for s in streams.values():
        for _ in s:
            pass


def _mixer_kernel(lp, n_tiles, attn_ref, u_ref, uh_ref, gates_ref, lead_ref, xa_ref, xb_ref,
                  pmix_ref, pscale_ref, wa_hbm, wp_hbm, wo_hbm, g2_ref, wr_ref, br_ref,
                  h2_ref, xp_ref, route_ref, route_t_ref, cnt_ref, run_ref, h2_s, a_s, p_s, m_s,
                  wa_ref, wp_ref, wo_ref, stage_ref, stage_sem):
    i = pl.program_id(0)
    tm = h2_ref.shape[0]
    d = h2_ref.shape[1]
    cw = d // MIX_CHUNKS

    @pl.when(i == 0)
    def _():
        run_ref[...] = jnp.zeros_like(run_ref)
        h2_s[...] = jnp.zeros_like(h2_s)
        for w_hbm, w_ref in ((wa_hbm, wa_ref), (wp_hbm, wp_ref), (wo_hbm, wo_ref)):
            _stage_bf16(w_hbm, w_ref, stage_ref, stage_sem)

    def chain():
        ic = jnp.minimum(i, n_tiles - 1)
        t = (ic * tm + lax.broadcasted_iota(I32, (tm, 1), 0)) % lp
        valid = t >= PAD
        tx = (ic * tm - POOL_HALO + lp + lax.broadcasted_iota(I32, (tm + POOL_HALO, 1), 0)) % lp
        n_rows = (t - PAD + 1).astype(F32)
        gw = u_ref.shape[1] // len(POOL_WINDOWS)
        attn = attn_ref[...]
        aw = d // len(POOL_WINDOWS)
        for gi, w in enumerate(POOL_WINDOWS):
            a_s[:, gi * aw:(gi + 1) * aw] = jnp.dot(attn, wa_ref[:, gi * aw:(gi + 1) * aw],
                                                    preferred_element_type=F32)
            cols = slice(gi * gw, (gi + 1) * gw)
            uext = jnp.concatenate([uh_ref[:, cols], u_ref[:, cols]], axis=0).astype(F32)
            c = jnp.where(tx >= PAD, uext, 0.0)
            s, span = c, 1
            while span < w:
                s = s[span:] + s[:-span]
                span *= 2
            win = s[POOL_HALO + 1 - w:POOL_HALO + 1 - w + tm]
            n_valid = jnp.clip(n_rows, 1.0, float(w))
            mixed = jnp.where(valid, win / n_valid - c[POOL_HALO:], 0.0)
            m_s[:, cols] = (jnp.dot(mixed.astype(BF16), pmix_ref[gi], preferred_element_type=F32)
                            * pscale_ref[:, cols]).astype(BF16)
            yield
        pool = m_s[:, :u_ref.shape[1]]
        for c in range(MIX_CHUNKS):
            cols = slice(c * cw, (c + 1) * cw)
            p_s[:, cols] = jnp.dot(pool, wp_ref[:, cols], preferred_element_type=F32)
            yield
        nb = lp // BLOCK
        halves = [jnp.where((2 * ic + hf) % nb == 0, lead_ref[...], ref[...])
                  for hf, ref in enumerate((xa_ref, xb_ref))]
        h2 = jnp.concatenate(halves, axis=0)
        for c in range(MIX_CHUNKS):
            cols = slice(c * cw, (c + 1) * cw)
            gate_a = gates_ref[:, cols]
            gate_p = gates_ref[:, d + c * cw:d + (c + 1) * cw]
            merged = (jax.nn.sigmoid(gate_a.astype(F32)) * a_s[:, cols]
                      + jax.nn.sigmoid(gate_p.astype(F32)) * p_s[:, cols]).astype(BF16)
            yield
            h2 = h2 + jnp.dot(merged, wo_ref[cols, :], preferred_element_type=F32)
            yield
        h2_ref[...] = h2
        h2_s[...] = h2
        yield

    def tail():
        h2p = h2_s[...]
        tp = ((i - 1) * tm + lp + lax.broadcasted_iota(I32, (tm, 1), 0)) % lp
        valid_p = (tp >= PAD) & (i >= 1)
        ms = jnp.mean(h2p * h2p, axis=-1, keepdims=True)
        inv = lax.rsqrt(ms + RMS_EPS)
        yield
        half = d // 2
        hi_prod = jnp.zeros((tm, 2 * LANES), F32)
        lo_prod = jnp.zeros((tm, LANES), F32)
        n_pack = MIX_CHUNKS // 2
        pw = half // n_pack
        for c in range(n_pack):
            parts = []
            for base in (c * pw, half + c * pw):
                hn = h2p[:, base:base + pw] * inv * g2_ref[:, base:base + pw]
                x_hi = hn.astype(BF16)
                x_hi32 = x_hi.astype(F32)
                x_lo = (hn - x_hi32).astype(BF16)
                hi_prod = hi_prod + jnp.dot(x_hi, wr_ref[base:base + pw, :], preferred_element_type=F32)
                lo_prod = lo_prod + jnp.dot(x_lo, wr_ref[base:base + pw, :LANES], preferred_element_type=F32)
                parts.append(lax.bitcast_convert_type(x_hi32, U32))
            xp_ref[:, c * pw:(c + 1) * pw] = (parts[1] & jnp.uint32(0xFFFF0000)) | (parts[0] >> 16)
            yield
        logits = hi_prod[:, :LANES] + (hi_prod[:, LANES:] + lo_prod) + br_ref[...]
        col = lax.broadcasted_iota(I32, logits.shape, 1).astype(F32)
        neg = -jnp.inf
        gl = jnp.where(col < N_GROUPS, logits, neg)
        gmax = jnp.max(gl, axis=-1, keepdims=True)
        grp = jnp.min(jnp.where(gl == gmax, col, float(LANES)), axis=-1, keepdims=True)
        p_grp = 1.0 / jnp.sum(jnp.exp(gl - gmax), axis=-1, keepdims=True)
        yield
        e_lo = N_GROUPS + grp * EXPERTS_PER_GROUP
        el = jnp.where((col >= e_lo) & (col < e_lo + EXPERTS_PER_GROUP), logits, neg)
        m1 = jnp.max(el, axis=-1, keepdims=True)
        i1 = jnp.min(jnp.where(el == m1, col, float(LANES)), axis=-1, keepdims=True)
        el2 = jnp.where(col == i1, neg, el)
        m2 = jnp.max(el2, axis=-1, keepdims=True)
        i2 = jnp.min(jnp.where(el2 == m2, col, float(LANES)), axis=-1, keepdims=True)
        z = jnp.exp(m2 - m1)
        w1 = p_grp / (1.0 + z)
        w2 = p_grp * z / (1.0 + z)
        e1 = i1 - N_GROUPS
        e2 = i2 - N_GROUPS
        yield
        oh1 = jnp.where((col == e1) & valid_p, 1.0, 0.0)
        oh2 = jnp.where((col == e2) & valid_p, 1.0, 0.0)
        lower = (lax.broadcasted_iota(I32, (tm, tm), 0) > lax.broadcasted_iota(I32, (tm, tm), 1))
        lower = jnp.where(lower, 1.0, 0.0).astype(BF16)
        before1 = jnp.dot(lower, oh1.astype(BF16), preferred_element_type=F32)
        before2 = jnp.dot(lower, oh2.astype(BF16), preferred_element_type=F32)
        tot1 = jnp.sum(oh1, axis=0, keepdims=True)
        tot2 = jnp.sum(oh2, axis=0, keepdims=True)
        run = run_ref[...]
        r1 = jnp.sum(oh1 * (run + before1), axis=-1, keepdims=True)
        r2 = jnp.sum(oh2 * (run + tot1 + before2), axis=-1, keepdims=True)
        run = run + tot1 + tot2
        run_ref[...] = run
        cnt_ref[...] = run
        yield
        slab = jnp.zeros(logits.shape, F32)
        for cidx, val in ((COL_E0, jnp.where(valid_p, e1, -1.0)),
                          (COL_E1, jnp.where(valid_p, e2, -1.0)),
                          (COL_W0, w1), (COL_W1, w2), (COL_R0, r1), (COL_R1, r2)):
            slab = jnp.where(col == cidx, val, slab)
        route_ref[...] = slab
        route_t_ref[...] = slab.T[:ROUTE_ROWS, :]
        yield

    _interleave("cccc" + "ctctctct" + "cctcctcctcc", c=chain(), t=tail())


def _mixer(lp, attn, proj, layout, lead, x2, pmix, pscale, wa, wp, wo, g2, wr, br):
    rows = attn.shape[0]
    d = x2.shape[1]
    _, u_off, pw = layout["u"]
    uc = u_off // pw
    assert layout["ga"][1] == 0 and layout["gp"][1] == d
    tm = ROW_TILE
    nb = lp // BLOCK
    n_tiles = rows // tm
    halo_blocks = tm // POOL_HALO
    any_space = pl.BlockSpec(memory_space=pl.ANY)
    cur = lambda i: (jnp.minimum(i, n_tiles - 1), 0)
    prev = lambda i: (jnp.maximum(i - 1, 0), 0)

    def x_spec(half):
        def index(i):
            g = 2 * jnp.minimum(i, n_tiles - 1) + half
            return ((g // nb) * (nb - 1) + jnp.maximum(g % nb - 1, 0), 0)
        return pl.BlockSpec((BLOCK, d), index)

    return pl.pallas_call(
        functools.partial(_mixer_kernel, lp, n_tiles),
        grid=(n_tiles + 1,),
        in_specs=[pl.BlockSpec((tm, attn.shape[1]), cur),
                  pl.BlockSpec((tm, pw), lambda i: (jnp.minimum(i, n_tiles - 1), uc)),
                  pl.BlockSpec((POOL_HALO, pw),
                               lambda i: (jnp.maximum(jnp.minimum(i, n_tiles - 1) * halo_blocks - 1, 0), uc)),
                  pl.BlockSpec((tm, 2 * d), cur),
                  _resident(lead.shape), x_spec(0), x_spec(1),
                  _resident(pmix.shape), _resident(pscale.shape), any_space, any_space, any_space,
                  _resident(g2.shape), _resident(wr.shape), _resident(br.shape)],
        out_specs=[pl.BlockSpec((tm, d), cur),
                   pl.BlockSpec((tm, d // 2), prev),
                   pl.BlockSpec((tm, LANES), prev),
                   pl.BlockSpec((ROUTE_ROWS, tm), lambda i: (0, jnp.maximum(i - 1, 0))),
                   pl.BlockSpec((1, LANES), lambda i: (0, 0))],
        out_shape=[jax.ShapeDtypeStruct((rows, d), F32),
                   jax.ShapeDtypeStruct((rows, d // 2), U32),
                   jax.ShapeDtypeStruct((rows, LANES), F32),
                   jax.ShapeDtypeStruct((ROUTE_ROWS, rows), F32),
                   jax.ShapeDtypeStruct((1, LANES), F32)],
        scratch_shapes=[pltpu.VMEM((1, LANES), F32), pltpu.VMEM((tm, d), F32),
                        pltpu.VMEM((tm, d), F32), pltpu.VMEM((tm, d), F32), pltpu.VMEM((tm, d), BF16),
                        pltpu.VMEM(wa.shape, BF16), pltpu.VMEM(wp.shape, BF16), pltpu.VMEM(wo.shape, BF16),
                        pltpu.VMEM((2, d, STAGE_COLS), F32), pltpu.SemaphoreType.DMA((2,))],
        compiler_params=_params(("arbitrary",)),
        name="mixer",
    )(attn, proj, proj, proj, lead, x2, x2, pmix, pscale, wa, wp, wo, g2, wr, br)


def _zero_unused_slots(gap_lo_ref, gap_hi_ref, used_ref, xs_ref, zero_ref, sem):
    bm = zero_ref.shape[0]
    n_rows = xs_ref.shape[0]
    n_blocks = (n_rows - SPARE_ROWS) // bm
    zero_ref[...] = jnp.zeros_like(zero_ref)

    def piece(start, size):
        return pltpu.make_async_copy(zero_ref.at[pl.ds(0, size), :], xs_ref.at[pl.ds(start, size), :], sem)

    def for_each_piece(op):
        def tail(e, carry):
            lo, hi = gap_lo_ref[e], gap_hi_ref[e]
            length = hi - lo
            for bit in range(int(math.log2(bm)) - 1, 2, -1):
                size = 1 << bit
                above = (length >> (bit + 1)) << (bit + 1)

                @pl.when((length >> bit) & 1 == 1)
                def _():
                    op(piece(pl.multiple_of(hi - above - size, 8), size))
            for j in range(7):
                @pl.when(j < (length & 7))
                def _():
                    op(piece(lo + j, 1))
            return carry

        def block(b, carry):
            op(piece(pl.multiple_of(b * bm, bm), bm))
            return carry

        lax.fori_loop(0, N_EXPERTS, tail, 0)
        lax.fori_loop(used_ref[0], n_blocks + SPARE_ROWS // bm, block, 0)

    for_each_piece(lambda c: c.start())
    for_each_piece(lambda c: c.wait())


def _scatter_kernel(gap_lo_ref, gap_hi_ref, used_ref, pos0_ref, pos1_ref, x_ref, xs_ref, stage_ref, zero_ref,
                    sem, zsem):
    i = pl.program_id(0)
    last = pl.num_programs(0) - 1
    tm = x_ref.shape[0]
    slot = i % 2

    @pl.when(i == 0)
    def _():
        _zero_unused_slots(gap_lo_ref, gap_hi_ref, used_ref, xs_ref, zero_ref, zsem)

    def retire(s):
        for k in range(2):
            pltpu.make_async_copy(stage_ref.at[s], xs_ref.at[pl.ds(0, tm), :], sem.at[s]).wait()

    @pl.when(i >= 2)
    def _():
        retire(slot)

    stage_ref[slot] = x_ref[...]

    for r in range(tm):
        for k in range(2):
            p = (pos0_ref, pos1_ref)[k][i * tm + r]
            pltpu.make_async_copy(stage_ref.at[slot, pl.ds(r, 1), :], xs_ref.at[pl.ds(p, 1), :],
                                  sem.at[slot]).start(priority=k)

    @pl.when(i == last)
    def _():
        @pl.when(i >= 1)
        def _():
            retire(1 - slot)
        retire(slot)


def _scatter(gap_lo, gap_hi, n_used, pos, xp, slots):
    rows, half = xp.shape
    tm = ROW_TILE
    assert SPARE_ROWS % EXPERT_ROWS == 0 and slots % EXPERT_ROWS == 0
    return pl.pallas_call(
        _scatter_kernel,
        grid_spec=pltpu.PrefetchScalarGridSpec(
            num_scalar_prefetch=5,
            grid=(rows // tm,),
            in_specs=[pl.BlockSpec((tm, half), lambda i, *_: (i, 0))],
            out_specs=pl.BlockSpec(memory_space=pl.ANY),
            scratch_shapes=[pltpu.VMEM((2, tm, half), U32), pltpu.VMEM((EXPERT_ROWS, half), U32),
                            pltpu.SemaphoreType.DMA((2,)), pltpu.SemaphoreType.DMA(())],
        ),
        out_shape=jax.ShapeDtypeStruct((slots + SPARE_ROWS, half), U32),
        compiler_params=_params(("arbitrary",)),
        name="scatter",
    )(gap_lo, gap_hi, n_used, pos[0], pos[1], xp)


def _pack_halves(x):
    n = x.shape[1] // 2
    r = x.astype(BF16).astype(F32)
    lo = lax.bitcast_convert_type(r[:, :n], U32)
    hi = lax.bitcast_convert_type(r[:, n:], U32)
    return (hi & jnp.uint32(0xFFFF0000)) | (lo >> 16)


def _unpack_halves(p):
    return (lax.bitcast_convert_type(p << 16, F32),
            lax.bitcast_convert_type(p & jnp.uint32(0xFFFF0000), F32))


def _expert_kernel(be_ref, nrow_ref, last_ref, ord_ref, used_ref, xs_ref, wg_hbm, wu_hbm, wd_hbm, ys_ref,
                   wg_f, wu_f, wd_f, sem):
    b = pl.program_id(0)
    n_valid_rows = nrow_ref[b]
    expert = be_ref[b]
    ordinal = ord_ref[b]
    n_slots = wg_f.shape[0]
    slot = ordinal % n_slots

    def weight_copies(e, s):
        return [pltpu.make_async_copy(hbm.at[e], buf.at[s], sem.at[s, n])
                for n, (hbm, buf) in enumerate(((wg_hbm, wg_f), (wu_hbm, wu_f), (wd_hbm, wd_f)))]

    def start_fetch(o):
        e = used_ref[o]

        @pl.when(e >= 0)
        def _():
            for c in weight_copies(e, o % n_slots):
                c.start()

    @pl.when(b == 0)
    def _():
        for o in range(WEIGHT_LOOKAHEAD):
            start_fetch(o)

    @pl.when((b == 0) | (expert != be_ref[jnp.maximum(b - 1, 0)]))
    def _():
        start_fetch(ordinal + WEIGHT_LOOKAHEAD)
        for c in weight_copies(expert, slot):
            c.wait()

    @pl.when(n_valid_rows == 0)
    def _():
        ys_ref[...] = jnp.zeros_like(ys_ref)

    @pl.when(n_valid_rows > 0)
    def _():
        lo, hi = _unpack_halves(xs_ref[...])
        x = jnp.concatenate([lo, hi], axis=1).astype(BF16)
        gate = jnp.dot(x, wg_f[slot].astype(BF16), preferred_element_type=F32)
        up = jnp.dot(x, wu_f[slot].astype(BF16), preferred_element_type=F32)
        hb = (jax.nn.silu(gate) * up).astype(BF16)
        ys_ref[...] = _pack_halves(jnp.dot(hb, wd_f[slot].astype(BF16), preferred_element_type=F32))


def _experts(block_expert, block_rows, last_block, block_ord, used_experts, xs, w_gate, w_up, w_down):
    half = xs.shape[1]
    n_exp, d, de = w_gate.shape
    bm = EXPERT_ROWS
    n_blocks = block_expert.shape[0]
    slots = n_blocks * bm
    n_slots = WEIGHT_LOOKAHEAD + 1
    any_space = pl.BlockSpec(memory_space=pl.ANY)
    return pl.pallas_call(
        _expert_kernel,
        grid_spec=pltpu.PrefetchScalarGridSpec(
            num_scalar_prefetch=5,
            grid=(n_blocks,),
            in_specs=[pl.BlockSpec((bm, half), lambda b, be, nr, last, *_: (jnp.minimum(b, last[0]), 0)),
                      any_space, any_space, any_space],
            out_specs=pl.BlockSpec((bm, half), lambda b, *_: (b, 0)),
            scratch_shapes=[pltpu.VMEM((n_slots, d, de), F32), pltpu.VMEM((n_slots, d, de), F32),
                            pltpu.VMEM((n_slots, de, d), F32),
                            pltpu.SemaphoreType.DMA((n_slots, 3))],
        ),
        out_shape=jax.ShapeDtypeStruct((slots, half), U32),
        compiler_params=_params(("arbitrary",)),
        name="experts",
    )(block_expert, block_rows, last_block, block_ord, used_experts, xs, w_gate, w_up, w_down)


def _combine_kernel(nb, per_batch, pos0_ref, pos1_ref, h2_ref, route_ref, g_ref, ys_ref, o_ref, y_ref, sem):
    t = pl.program_id(0)
    tm = h2_ref.shape[0]

    def issue(tile, slot):
        base = ((tile // per_batch) * nb + 1 + tile % per_batch) * tm

        for r in range(tm):
            for k in range(2):
                p = (pos0_ref, pos1_ref)[k][base + r]
                pltpu.make_async_copy(ys_ref.at[pl.ds(p, 1), :], y_ref.at[slot, k, pl.ds(r, 1), :],
                                      sem.at[slot]).start(priority=k)

    @pl.when(t == 0)
    def _():
        issue(0, 0)

    @pl.when(t + 1 < pl.num_programs(0))
    def _():
        issue(t + 1, (t + 1) % 2)

    slot = t % 2
    for k in range(2):
        pltpu.make_async_copy(ys_ref.at[pl.ds(0, tm), :], y_ref.at[slot, k], sem.at[slot]).wait()
    route = route_ref[...]
    w0 = route[:, COL_W0:COL_W0 + 1]
    w1 = route[:, COL_W1:COL_W1 + 1]
    half = y_ref.shape[-1]
    y0 = _unpack_halves(y_ref[slot, 0])
    y1 = _unpack_halves(y_ref[slot, 1])
    hs = [h2_ref[:, s * half:(s + 1) * half] + (w0 * y0[s] + w1 * y1[s]) for s in range(2)]
    ms = sum(jnp.sum(h * h, axis=-1, keepdims=True) for h in hs) / (2 * half)
    inv = lax.rsqrt(ms + RMS_EPS)
    for s in range(2):
        o_ref[0, :, s * half:(s + 1) * half] = hs[s] * inv * g_ref[:, s * half:(s + 1) * half]


def _combine(pos, h2, route, g, ys, batch, nb, seq):
    rows, d = h2.shape
    tm = BLOCK
    per_batch = seq // tm
    tile = lambda t, *_: ((t // per_batch) * nb + 1 + t % per_batch, 0)
    return pl.pallas_call(
        functools.partial(_combine_kernel, nb, per_batch),
        grid_spec=pltpu.PrefetchScalarGridSpec(
            num_scalar_prefetch=2,
            grid=(batch * per_batch,),
            in_specs=[pl.BlockSpec((tm, d), tile),
                      pl.BlockSpec((tm, LANES), tile),
                      pl.BlockSpec((1, d), lambda t, *_: (0, 0)),
                      pl.BlockSpec(memory_space=pl.ANY)],
            out_specs=pl.BlockSpec((1, tm, d), lambda t, *_: (t // per_batch, t % per_batch, 0)),
            scratch_shapes=[pltpu.VMEM((2, 2, tm, ys.shape[1]), U32), pltpu.SemaphoreType.DMA((2,))],
        ),
        out_shape=jax.ShapeDtypeStruct((batch, seq, d), F32),
        compiler_params=_params(("arbitrary",)),
        name="combine",
    )(pos[0], pos[1], h2, route, g, ys)


def kernel(x, meta_tokens, rel_bias, norm_mix, w_in, attn_sinks, pool_mix, pool_scale,
           w_attn_branch, w_pool_branch, w_out, norm_ffn, w_router_group, b_router_group,
           w_router_expert, b_router_expert, w_gate, w_up, w_down, norm_final):
    batch, seq, d = x.shape
    depth = w_in.shape[0]
    assert depth == 1, "single-layer stack"
    aw = w_attn_branch.shape[1]
    pw = w_pool_branch.shape[1]
    kvw = 2 * N_KV_HEADS * HEAD_DIM
    assert aw == N_Q_HEADS * HEAD_DIM and w_in.shape[2] == aw + kvw + pw + 2 * d
    assert seq % BLOCK == 0
    lp = seq + BLOCK
    nb = lp // BLOCK
    rows = batch * lp
    assert rows % ROW_TILE == 0

    assert ROW_TILE == 2 * BLOCK
    lead = jnp.concatenate([jnp.zeros((PAD, d), x.dtype), meta_tokens.astype(x.dtype)], axis=0)
    x2 = x.reshape(batch * seq, d)

    layout = _proj_layout(aw, kvw, pw, d)
    proj = _inproj(lead, x2, nb, norm_mix[0][None], w_in[0], layout)

    attn = _attention(proj, layout, _attn_bias_tables(rel_bias), attn_sinks[0].astype(F32), batch, nb)

    n_router = N_GROUPS + N_EXPERTS
    wr = jnp.concatenate([w_router_group[0], w_router_expert[0], jnp.zeros((d, LANES - n_router), F32)], axis=1)
    br = jnp.concatenate([b_router_group[0], b_router_expert[0], jnp.zeros((LANES - n_router,), F32)])[None]
    wr_hi = wr.astype(BF16)
    wr_split = jnp.concatenate([wr_hi, (wr - wr_hi.astype(F32)).astype(BF16)], axis=1)
    h2, xp, route, route_t, counts = _mixer(
        lp, attn, proj, layout, lead, x2, pool_mix[0].astype(BF16), pool_scale[0][None].astype(F32),
        w_attn_branch[0], w_pool_branch[0], w_out[0], norm_ffn[0][None], wr_split, br)

    bm = EXPERT_ROWS
    n_tok = batch * (seq + N_META)
    n_blocks = (2 * n_tok) // bm + N_EXPERTS
    cnt = counts[0, :N_EXPERTS].astype(I32)
    blocks_e = (cnt + bm - 1) // bm
    bend = jnp.cumsum(blocks_e)
    bstart = bend - blocks_e
    ord_e = jnp.cumsum((blocks_e > 0).astype(I32)) - 1
    n_used = bend[-1]
    last_block = jnp.maximum(n_used - 1, 0)
    blk = jnp.arange(n_blocks, dtype=I32)

    def expert_of(block):
        return jnp.minimum(jnp.sum((bend[None, :] <= block[:, None]).astype(I32), axis=1), N_EXPERTS - 1)

    block_expert = expert_of(jnp.minimum(blk, last_block))
    own = block_expert[:, None] == jnp.arange(N_EXPERTS, dtype=I32)[None, :]
    pick = lambda v: jnp.sum(jnp.where(own, v[None, :], 0), axis=1)
    block_rows = jnp.clip(pick(cnt) - (blk - pick(bstart)) * bm, 0, bm)
    block_rows = jnp.where(blk < n_used, block_rows, 0).astype(I32)
    block_ord = pick(ord_e).astype(I32)
    experts = jnp.arange(N_EXPERTS, dtype=I32)
    is_kth = (ord_e[None, :] == jnp.arange(N_EXPERTS + WEIGHT_LOOKAHEAD, dtype=I32)[:, None]) & (blocks_e > 0)[None, :]
    used_experts = jnp.where(jnp.any(is_kth, axis=1), jnp.sum(jnp.where(is_kth, experts[None, :], 0), axis=1), -1)
    gap_lo = (bstart * bm + cnt).astype(I32)
    gap_hi = (bend * bm).astype(I32)
    e = route_t[COL_E0:COL_E1 + 1].astype(I32)
    rank = route_t[COL_R0:COL_R1 + 1].astype(I32)
    first_row = jnp.sum(jnp.where(e[:, None, :] == experts[None, :, None], (bstart * bm)[None, :, None], 0), axis=1)
    spare = n_blocks * bm + (2 * jnp.arange(rows, dtype=I32)[None, :] + jnp.arange(2, dtype=I32)[:, None]) % SPARE_ROWS
    pos = jnp.where(e >= 0, first_row + rank, spare).astype(I32)

    xs = _scatter(gap_lo, gap_hi, n_used.astype(I32).reshape(1), pos, xp, n_blocks * bm)
    ys = _experts(block_expert.astype(I32), block_rows, last_block.astype(I32).reshape(1), block_ord,
                  used_experts.astype(I32), xs, w_gate[0], w_up[0], w_down[0])
    return _combine(pos, h2, route, norm_final[None].astype(F32), ys, batch, nb, seq)
```
